```python
import jax, jax.numpy as jnp
from jax import lax
import numpy as np


D_MODEL = 1024
BATCH = 16
SEQ = 2048
DEPTH = 1

CTX_LEN = 256
GRID_W = 64
D_MIX = D_MODEL
POOL_WIDTH = D_MIX // 4
POOL_WINDOWS = (2, 4, 8, 16)
POOL_GROUP = POOL_WIDTH // len(POOL_WINDOWS)
ATTN_WIDTH = D_MIX - POOL_WIDTH
HEAD_DIM = 64
N_HEADS = ATTN_WIDTH // HEAD_DIM
N_KV_HEADS = 4
GQA_GROUP = N_HEADS // N_KV_HEADS
KV_WIDTH = N_KV_HEADS * HEAD_DIM
PROJ_WIDTH = POOL_WIDTH + ATTN_WIDTH + 2 * KV_WIDTH
Q_BLOCK = 128
ROPE_THETA = 10000.0
AXIS_DIM = HEAD_DIM // 2
N_EXPERTS = 64
TOP_K = 8
N_GROUPS = 8
TOPK_GROUPS = 4
EXPERT_FF = 256
SHARED_FF = 256
ROUTED_SCALE = 2.5
EXPERT_BLOCK = 256
ALPHA = (2.0 * DEPTH) ** 0.25
BETA = (8.0 * DEPTH) ** -0.25
LN_EPS = 1e-5
RMS_EPS = 1e-6

kernel_name = 'hybrid_pool_gqa_moe_dit_layer'


def layer_norm(x, g, b):
    xf = x.astype(jnp.float32)
    mu = jnp.mean(xf, axis=-1, keepdims=True)
    var = jnp.mean(jnp.square(xf - mu), axis=-1, keepdims=True)
    return ((xf - mu) * lax.rsqrt(var + LN_EPS) * g + b).astype(x.dtype)


def head_rms_norm(x, g):
    xf = x.astype(jnp.float32)
    return (xf * lax.rsqrt(jnp.mean(jnp.square(xf), axis=-1, keepdims=True) + RMS_EPS) * g).astype(x.dtype)


def axial_rope_tables(rows):
    inv_freq = ROPE_THETA ** (-jnp.arange(0, AXIS_DIM, 2, dtype=jnp.float32) / AXIS_DIM)
    row = jnp.repeat(jnp.arange(rows, dtype=jnp.float32), GRID_W)
    col = jnp.tile(jnp.arange(GRID_W, dtype=jnp.float32), rows)
    ang = jnp.concatenate([row[:, None] * inv_freq, col[:, None] * inv_freq], axis=-1)
    return jnp.cos(ang), jnp.sin(ang)


def apply_rope(x, cos, sin):
    shape = x.shape
    xf = x.astype(jnp.float32).reshape(shape[:-1] + (HEAD_DIM // 2, 2))
    bshape = (1, shape[1]) + (1,) * (x.ndim - 3) + (HEAD_DIM // 2,)
    c = cos.reshape(bshape)
    s = sin.reshape(bshape)
    x0, x1 = xf[..., 0], xf[..., 1]
    out = jnp.stack([x0 * c - x1 * s, x0 * s + x1 * c], axis=-1)
    return out.reshape(shape).astype(x.dtype)


def split_projection(p):
    B_, L, _ = p.shape
    o1 = POOL_WIDTH
    o2 = o1 + ATTN_WIDTH
    o3 = o2 + KV_WIDTH
    u = p[..., :o1]
    q = p[..., o1:o2].reshape(B_, L, N_KV_HEADS, GQA_GROUP, HEAD_DIM)
    k = p[..., o2:o3].reshape(B_, L, N_KV_HEADS, HEAD_DIM)
    v = p[..., o3:].reshape(B_, L, N_KV_HEADS, HEAD_DIM)
    return u, q, k, v


def pool_mixer(u, pool_w, pool_scale):
    B_, L, _ = u.shape
    uf = u.astype(jnp.float32)
    cs = jnp.concatenate([jnp.zeros((B_, 1, POOL_WIDTH), jnp.float32), jnp.cumsum(uf, axis=1)], axis=1)
    t = jnp.arange(L)
    parts = []
    for gi, win in enumerate(POOL_WINDOWS):
        lo = jnp.clip(t - win // 2, 0, L)
        hi = jnp.clip(t + win - win // 2, 0, L)
        sl = slice(gi * POOL_GROUP, (gi + 1) * POOL_GROUP)
        window_mean = (cs[:, hi, sl] - cs[:, lo, sl]) / (hi - lo).astype(jnp.float32)[None, :, None]
        parts.append(window_mean - uf[:, :, sl])
    d = jnp.stack(parts, axis=2).astype(u.dtype)
    y = jnp.einsum('blgc,gcd->blgd', d, pool_w).reshape(B_, L, POOL_WIDTH)
    return y * pool_scale


def attend(q, k, v):
    s = jnp.einsum('bqhgd,bkhd->bhgqk', q, k, preferred_element_type=jnp.float32) * (HEAD_DIM ** -0.5)
    p = jax.nn.softmax(s, axis=-1).astype(v.dtype)
    return jnp.einsum('bhgqk,bkhd->bqhgd', p, v)


def attend_latent_blocks(q, k_all, v_all):
    B_, L = q.shape[:2]
    nb = L // Q_BLOCK
    qb = jnp.moveaxis(q.reshape((B_, nb, Q_BLOCK) + q.shape[2:]), 1, 0)
    o = lax.map(lambda qblk: attend(qblk, k_all, v_all), qb)
    return jnp.moveaxis(o, 0, 1).reshape(B_, L, ATTN_WIDTH)


def moe_ffn(h, w_router, router_bias, w_gate, w_up, w_down, ws_gate, ws_up, ws_down):
    B_, L, D = h.shape
    xt = h.reshape(-1, D)
    T = xt.shape[0]
    scores = jax.nn.sigmoid(jnp.dot(xt.astype(jnp.float32), w_router.astype(jnp.float32)))
    sel = scores + router_bias.astype(jnp.float32)
    grp = sel.reshape(T, N_GROUPS, N_EXPERTS // N_GROUPS)
    grp_score = lax.top_k(grp, 2)[0].sum(-1)
    _, top_groups = lax.top_k(grp_score, TOPK_GROUPS)
    group_mask = jnp.any(top_groups[:, :, None] == jnp.arange(N_GROUPS)[None, None, :], axis=1)
    expert_mask = jnp.repeat(group_mask, N_EXPERTS // N_GROUPS, axis=1)
    sel = jnp.where(expert_mask, sel, -jnp.inf)
    _, idx = lax.top_k(sel, TOP_K)
    wts = jnp.take_along_axis(scores, idx, axis=1)
    wts = wts / jnp.sum(wts, axis=-1, keepdims=True) * ROUTED_SCALE
    A = T * TOP_K
    e_flat = idx.reshape(-1)
    order = jnp.argsort(e_flat)
    e_sorted = e_flat[order]
    tok_sorted = (order // TOP_K).astype(jnp.int32)
    w_sorted = wts.reshape(-1)[order]
    counts = jnp.bincount(e_flat, length=N_EXPERTS)
    start = jnp.cumsum(counts) - counts
    padded = ((counts + EXPERT_BLOCK - 1) // EXPERT_BLOCK) * EXPERT_BLOCK
    pad_end = jnp.cumsum(padded)
    pad_start = pad_end - padded
    dest = pad_start[e_sorted] + (jnp.arange(A) - start[e_sorted])
    P = A + N_EXPERTS * EXPERT_BLOCK
    n_blocks = P // EXPERT_BLOCK
    row_tok = jnp.full((P,), T, jnp.int32).at[dest].set(tok_sorted)
    row_w = jnp.zeros((P,), jnp.float32).at[dest].set(w_sorted)
    block_expert = jnp.minimum(jnp.searchsorted(pad_end, jnp.arange(n_blocks) * EXPERT_BLOCK, side='right'), N_EXPERTS - 1)
    xpad = jnp.concatenate([xt, jnp.zeros((1, D), xt.dtype)], axis=0)

    def expert_block(out, blk):
        tok, wr, e = blk
        rows = xpad[tok]
        y = (jax.nn.silu(rows @ w_gate[e]) * (rows @ w_up[e])) @ w_down[e]
        return out.at[tok].add((y * wr[:, None]).astype(out.dtype)), None

    routed, _ = lax.scan(expert_block, jnp.zeros((T + 1, D), xt.dtype),
                         (row_tok.reshape(n_blocks, EXPERT_BLOCK), row_w.reshape(n_blocks, EXPERT_BLOCK), block_expert))
    shared = (jax.nn.silu(xt @ ws_gate) * (xt @ ws_up)) @ ws_down
    return (routed[:T] + shared).reshape(B_, L, D)


def setup_inputs(seed: int = 0) -> dict:
    key = jax.random.key(seed)
    ks = jax.random.split(key, 26)

    def nrm(k, shape, scale):
        return scale * jax.random.normal(k, shape, jnp.float32)

    Dm = D_MODEL
    return {
        'x': nrm(ks[0], (BATCH, SEQ, Dm), 1.0),
        'c': nrm(ks[1], (BATCH, Dm), 1.0),
        'ctx': nrm(ks[2], (BATCH, CTX_LEN, Dm), 1.0),
        'c_ctx': nrm(ks[3], (Dm,), 1.0),
        'w_mod': nrm(ks[4], (DEPTH, Dm, 6 * Dm), 0.5 * Dm ** -0.5),
        'b_mod': nrm(ks[5], (DEPTH, 6 * Dm), 0.02),
        'w_in': nrm(ks[6], (DEPTH, Dm, PROJ_WIDTH), Dm ** -0.5),
        'q_norm': 1.0 + nrm(ks[7], (DEPTH, HEAD_DIM), 0.05),
        'k_norm': 1.0 + nrm(ks[8], (DEPTH, HEAD_DIM), 0.05),
        'pool_w': nrm(ks[9], (DEPTH, len(POOL_WINDOWS), POOL_GROUP, POOL_GROUP), POOL_GROUP ** -0.5),
        'pool_scale': 1.0 + nrm(ks[10], (DEPTH, POOL_WIDTH), 0.05),
        'w_out': nrm(ks[11], (DEPTH, D_MIX, Dm), BETA * D_MIX ** -0.5),
        'ln1_g': 1.0 + nrm(ks[12], (DEPTH, Dm), 0.05),
        'ln1_b': nrm(ks[13], (DEPTH, Dm), 0.02),
        'w_router': nrm(ks[14], (DEPTH, Dm, N_EXPERTS), Dm ** -0.5),
        'router_bias': nrm(ks[15], (DEPTH, N_EXPERTS), 0.01),
        'w_gate': nrm(ks[16], (DEPTH, N_EXPERTS, Dm, EXPERT_FF), Dm ** -0.5),
        'w_up': nrm(ks[17], (DEPTH, N_EXPERTS, Dm, EXPERT_FF), Dm ** -0.5),
        'w_down': nrm(ks[18], (DEPTH, N_EXPERTS, EXPERT_FF, Dm), BETA * EXPERT_FF ** -0.5),
        'ws_gate': nrm(ks[19], (DEPTH, Dm, SHARED_FF), Dm ** -0.5),
        'ws_up': nrm(ks[20], (DEPTH, Dm, SHARED_FF), Dm ** -0.5),
        'ws_down': nrm(ks[21], (DEPTH, SHARED_FF, Dm), BETA * SHARED_FF ** -0.5),
        'ln2_g': 1.0 + nrm(ks[22], (DEPTH, Dm), 0.05),
        'ln2_b': nrm(ks[23], (DEPTH, Dm), 0.02),
    }


def reference(x, c, ctx, c_ctx, w_mod, b_mod, w_in, q_norm, k_norm, pool_w, pool_scale, w_out,
              ln1_g, ln1_b, w_router, router_bias, w_gate, w_up, w_down, ws_gate, ws_up, ws_down,
              ln2_g, ln2_b):
    B_, S_, _ = x.shape
    rows = S_ // GRID_W
    cos, sin = axial_rope_tables(rows)
    for i in range(DEPTH):
        last = i == DEPTH - 1
        mod_lat = (jax.nn.silu(c) @ w_mod[i] + b_mod[i])[:, None, :]
        mod_ctx = (jax.nn.silu(c_ctx) @ w_mod[i] + b_mod[i])[None, None, :]
        sh1, sc1, g1, sh2, sc2, g2 = jnp.split(mod_lat, 6, axis=-1)
        csh1, csc1, cg1, csh2, csc2, cg2 = jnp.split(mod_ctx, 6, axis=-1)

        u_l, q_l, k_l, v_l = split_projection((x * (1.0 + sc1) + sh1) @ w_in[i])
        u_c, q_c, k_c, v_c = split_projection((ctx * (1.0 + csc1) + csh1) @ w_in[i])
        q_l = apply_rope(head_rms_norm(q_l, q_norm[i]), cos, sin)
        k_l = apply_rope(head_rms_norm(k_l, k_norm[i]), cos, sin)
        k_c = head_rms_norm(k_c, k_norm[i])
        k_all = jnp.concatenate([k_c, k_l], axis=1)
        v_all = jnp.concatenate([v_c, v_l], axis=1)
        attn_l = attend_latent_blocks(q_l, k_all, v_all)
        mix_l = jnp.concatenate([pool_mixer(u_l, pool_w[i], pool_scale[i]), attn_l], axis=-1) @ w_out[i]
        x_mid = layer_norm(ALPHA * x + g1 * mix_l, ln1_g[i], ln1_b[i])
        ffn_l = moe_ffn(x_mid * (1.0 + sc2) + sh2, w_router[i], router_bias[i], w_gate[i], w_up[i],
                        w_down[i], ws_gate[i], ws_up[i], ws_down[i])
        x_new = layer_norm(ALPHA * x_mid + g2 * ffn_l, ln2_g[i], ln2_b[i])

        if not last:
            q_c = head_rms_norm(q_c, q_norm[i])
            attn_c = attend(q_c, k_c, v_c).reshape(B_, ctx.shape[1], ATTN_WIDTH)
            mix_c = jnp.concatenate([pool_mixer(u_c, pool_w[i], pool_scale[i]), attn_c], axis=-1) @ w_out[i]
            ctx_mid = layer_norm(ALPHA * ctx + cg1 * mix_c, ln1_g[i], ln1_b[i])
            ffn_c = moe_ffn(ctx_mid * (1.0 + csc2) + csh2, w_router[i], router_bias[i], w_gate[i], w_up[i],
                            w_down[i], ws_gate[i], ws_up[i], ws_down[i])
            ctx = layer_norm(ALPHA * ctx_mid + cg2 * ffn_c, ln2_g[i], ln2_b[i])
        x = x_new
    return x
```

```python
import functools

import jax
import jax.numpy as jnp
from jax import lax
from jax.experimental import pallas as pl
from jax.experimental.pallas import tpu as pltpu

F32 = jnp.float32
BF16 = jnp.bfloat16
I32 = jnp.int32

LANE = 128
SUBLANE = 8

D_MODEL = 1024
GRID_W = 64
POOL_WIDTH = 256
POOL_WINDOWS = (2, 4, 8, 16)
POOL_GROUP = 64
HEAD_DIM = 64
HALF = HEAD_DIM // 2
N_HEADS = 12
N_KV_HEADS = 4
GQA_GROUP = N_HEADS // N_KV_HEADS
ATTN_WIDTH = N_HEADS * HEAD_DIM
KV_WIDTH = N_KV_HEADS * HEAD_DIM
ROPE_THETA = 10000.0
N_EXPERTS = 64
TOP_K = 8
N_GROUPS = 8
GROUP_SIZE = N_EXPERTS // N_GROUPS
TOPK_GROUPS = 4
EXPERT_FF = 256
SHARED_FF = 256
ROUTED_SCALE = 2.5
DEPTH = 1
ALPHA = (2.0 * DEPTH) ** 0.25
LN_EPS = 1e-5
RMS_EPS = 1e-6

MOD_ROWS = 24
HALF_D = D_MODEL // 2
ROW_SUB = HALF_D // LANE
Q_COLS = N_HEADS * LANE
K_COLS = N_KV_HEADS * LANE
ATTN_PAD = N_KV_HEADS * 2 * LANE

TM_INPROJ = 512
TQ_ATTN = 256
TM_MIX = 256
BM_MOE = 256
TM_COMBINE = 256

_NT = (((1,), (1,)), ((), ()))


def _dot(a, b):
    return jnp.dot(a, b, preferred_element_type=F32)


def _pack2(lo, hi):
    lo_bits = lax.bitcast_convert_type(lo.astype(BF16).astype(F32), I32)
    hi_bits = lax.bitcast_convert_type(hi.astype(BF16).astype(F32), I32)
    return lax.shift_right_logical(lo_bits, 16) | (hi_bits & jnp.int32(-65536))


def _unpack2(w):
    lo = lax.bitcast_convert_type(lax.shift_left(w, 16), F32)
    hi = lax.bitcast_convert_type(w & jnp.int32(-65536), F32)
    return lo, hi


def _store_rows(ref, packed):
    n = packed.shape[0]
    for s in range(ROW_SUB):
        ref[pl.ds(s, n, stride=ROW_SUB), :] = packed[:, s * LANE:(s + 1) * LANE]


def _load_rows(ref, n):
    return [ref[pl.ds(s, n, stride=ROW_SUB), :] for s in range(ROW_SUB)]


def _silu(x):
    return x * jax.nn.sigmoid(x)


def _layer_norm(r, g, b):
    mu = jnp.mean(r, axis=-1, keepdims=True)
    d = r - mu
    var = jnp.mean(d * d, axis=-1, keepdims=True)
    return d * lax.rsqrt(var + LN_EPS) * g + b


def _mod_kernel(c_ref, w_ref, b_ref, o_ref):
    a = _silu(c_ref[...])
    o_ref[...] = jnp.dot(a, w_ref[...], precision=lax.Precision.HIGHEST,
                         preferred_element_type=F32) + b_ref[...]


def _mod_call(cc, w_mod, b_mod):
    n = w_mod.shape[1]
    tn = 512
    return pl.pallas_call(
        _mod_kernel,
        grid=(n // tn,),
        in_specs=[pl.BlockSpec((MOD_ROWS, D_MODEL), lambda j: (0, 0)),
                  pl.BlockSpec((D_MODEL, tn), lambda j: (0, j)),
                  pl.BlockSpec((1, tn), lambda j: (0, j))],
        out_specs=pl.BlockSpec((MOD_ROWS, tn), lambda j: (0, j)),
        out_shape=jax.ShapeDtypeStruct((MOD_ROWS, n), F32),
        name="mod",
    )(cc, w_mod, b_mod)


def _norm_head(seg, g):
    ms = jnp.sum(seg * seg, axis=-1, keepdims=True) * (1.0 / HEAD_DIM)
    return seg * lax.rsqrt(ms + RMS_EPS) * g


def _rope(xn, c, s):
    return xn * c + pltpu.roll(xn, LANE // 2, axis=1) * s


def _inproj_kernel(x_ref, mod_ref, w_ref, qg_ref, kg_ref, cq_ref, sq_ref, ck_ref, sk_ref,
                   u_ref, q_ref, k_ref, v_ref):
    sh = mod_ref[0, 0:1, :]
    sc = mod_ref[0, 1:2, :]
    xm = (x_ref[0] * (1.0 + sc) + sh).astype(BF16)
    u_ref[0] = _dot(xm, w_ref[:, 0:POOL_WIDTH])
    qg, kg = qg_ref[...], kg_ref[...]
    cq, sq, ck, sk = cq_ref[...], sq_ref[...], ck_ref[...], sk_ref[...]
    base = POOL_WIDTH
    for pair in range(N_HEADS // 2):
        p = _dot(xm, w_ref[:, base + pair * 2 * LANE: base + (pair + 1) * 2 * LANE])
        for j in range(2):
            h = pair * 2 + j
            q = _rope(_norm_head(p[:, j * LANE:(j + 1) * LANE], qg), cq, sq)
            q_ref[0, :, h * LANE:(h + 1) * LANE] = q.astype(BF16)
    base += Q_COLS
    for pair in range(N_KV_HEADS // 2):
        p = _dot(xm, w_ref[:, base + pair * 2 * LANE: base + (pair + 1) * 2 * LANE])
        for j in range(2):
            h = pair * 2 + j
            k = _rope(_norm_head(p[:, j * LANE:(j + 1) * LANE], kg), ck, sk)
            k_ref[0, :, h * LANE:(h + 1) * LANE] = k.astype(BF16)
    base += K_COLS
    v_ref[0] = _dot(xm, w_ref[:, base:base + KV_WIDTH]).astype(BF16)


def _inproj_call(x, mod3, w_all, qg, kg, cq, sq, ck, sk):
    B, S, _ = x.shape
    tm = TM_INPROJ
    ncol = w_all.shape[1]
    tab = pl.BlockSpec((tm, LANE), lambda b, i: (i, 0))
    vec = pl.BlockSpec((1, LANE), lambda b, i: (0, 0))
    return pl.pallas_call(
        _inproj_kernel,
        grid=(B, S // tm),
        in_specs=[pl.BlockSpec((1, tm, D_MODEL), lambda b, i: (b, i, 0)),
                  pl.BlockSpec((1, 6, D_MODEL), lambda b, i: (b, 0, 0)),
                  pl.BlockSpec((D_MODEL, ncol), lambda b, i: (0, 0)),
                  vec, vec, tab, tab, tab, tab],
        out_specs=[pl.BlockSpec((1, tm, POOL_WIDTH), lambda b, i: (b, i, 0)),
                   pl.BlockSpec((1, tm, Q_COLS), lambda b, i: (b, i, 0)),
                   pl.BlockSpec((1, tm, K_COLS), lambda b, i: (b, i, 0)),
                   pl.BlockSpec((1, tm, KV_WIDTH), lambda b, i: (b, i, 0))],
        out_shape=[jax.ShapeDtypeStruct((B, S, POOL_WIDTH), F32),
                   jax.ShapeDtypeStruct((B, S, Q_COLS), BF16),
                   jax.ShapeDtypeStruct((B, S, K_COLS), BF16),
                   jax.ShapeDtypeStruct((B, S, KV_WIDTH), BF16)],
        name="inproj",
    )(x, mod3, w_all, qg, kg, cq, sq, ck, sk)


def _ctx_kv_kernel(x_ref, mod_ref, w_ref, kg_ref, k_ref, v_ref):
    sh = mod_ref[0, 0:1, :]
    sc = mod_ref[0, 1:2, :]
    xm = (x_ref[0] * (1.0 + sc) + sh).astype(BF16)
    kg = kg_ref[...]
    for pair in range(N_KV_HEADS // 2):
        p = _dot(xm, w_ref[:, pair * 2 * LANE:(pair + 1) * 2 * LANE])
        for j in range(2):
            h = pair * 2 + j
            k_ref[0, :, h * LANE:(h + 1) * LANE] = _norm_head(p[:, j * LANE:(j + 1) * LANE], kg).astype(BF16)
    v_ref[0] = _dot(xm, w_ref[:, K_COLS:K_COLS + KV_WIDTH]).astype(BF16)


def _ctx_kv_call(ctx, mod3, w_kv, kg, ctx_row):
    B, C, _ = ctx.shape
    return pl.pallas_call(
        _ctx_kv_kernel,
        grid=(B,),
        in_specs=[pl.BlockSpec((1, C, D_MODEL), lambda b: (b, 0, 0)),
                  pl.BlockSpec((1, 6, D_MODEL), lambda b: (ctx_row, 0, 0)),
                  pl.BlockSpec((D_MODEL, K_COLS + KV_WIDTH), lambda b: (0, 0)),
                  pl.BlockSpec((1, LANE), lambda b: (0, 0))],
        out_specs=[pl.BlockSpec((1, C, K_COLS), lambda b: (b, 0, 0)),
                   pl.BlockSpec((1, C, KV_WIDTH), lambda b: (b, 0, 0))],
        out_shape=[jax.ShapeDtypeStruct((B, C, K_COLS), BF16),
                   jax.ShapeDtypeStruct((B, C, KV_WIDTH), BF16)],
        name="ctx_kv",
    )(ctx, mod3, w_kv, kg)


POOL_PAD = 8


def _pool_kernel(u_ref, bd_ref, ps_ref, y_ref):
    S = u_ref.shape[1]
    n = S + 2 * POOL_PAD
    t = lax.broadcasted_iota(I32, (S, LANE), 0)
    lane = lax.broadcasted_iota(I32, (S, LANE), 1)
    zpad = jnp.zeros((POOL_PAD, LANE), F32)
    for half in range(POOL_WIDTH // LANE):
        u = u_ref[0, :, half * LANE:(half + 1) * LANE]
        ue = jnp.concatenate([zpad, u, zpad], axis=0)
        fwd = {1: ue}
        w = 1
        while w < POOL_WINDOWS[2 * half + 1]:
            fwd[2 * w] = fwd[w] + pltpu.roll(fwd[w], n - w, axis=0)
            w *= 2
        ds = []
        for win in POOL_WINDOWS[2 * half: 2 * half + 2]:
            hw = win // 2
            centred = pltpu.roll(fwd[win], hw, axis=0)[POOL_PAD:POOL_PAD + S]
            cnt = (jnp.minimum(t + hw, S) - jnp.maximum(t - hw, 0)).astype(F32)
            ds.append(centred / cnt - u)
        d = jnp.where(lane < POOL_GROUP, ds[0], ds[1]).astype(BF16)
        sl = slice(half * LANE, (half + 1) * LANE)
        y = _dot(d, bd_ref[sl, sl]) * ps_ref[:, sl]
        y_ref[0, :, sl] = y.astype(BF16)


def _pool_call(u, bd, ps):
    B, S, _ = u.shape
    return pl.pallas_call(
        _pool_kernel,
        grid=(B,),
        in_specs=[pl.BlockSpec((1, S, POOL_WIDTH), lambda b: (b, 0, 0)),
                  pl.BlockSpec((POOL_WIDTH, POOL_WIDTH), lambda b: (0, 0)),
                  pl.BlockSpec((1, POOL_WIDTH), lambda b: (0, 0))],
        out_specs=pl.BlockSpec((1, S, POOL_WIDTH), lambda b: (b, 0, 0)),
        out_shape=jax.ShapeDtypeStruct((B, S, POOL_WIDTH), BF16),
        name="pool",
    )(u, bd, ps)


def _attn_kernel(q_ref, k_ref, v_ref, o_ref):
    k = k_ref[0]
    v = v_ref[0]
    outs = []
    for j in range(GQA_GROUP):
        q = q_ref[0, :, j * LANE:(j + 1) * LANE]
        s = lax.dot_general(q, k, _NT, preferred_element_type=F32)
        m = jnp.max(s, axis=-1, keepdims=True)
        p = jnp.exp(s - m)
        l = jnp.sum(p, axis=-1, keepdims=True)
        outs.append(_dot(p.astype(BF16), v) / l)
    o_ref[0, :, 0:LANE] = (outs[0] + pltpu.roll(outs[1], LANE // 2, axis=1)).astype(BF16)
    o_ref[0, :, LANE:2 * LANE] = outs[2].astype(BF16)


def _attn_call(q, k_all, v_all):
    B, S, _ = q.shape
    Lk = k_all.shape[1]
    tq = TQ_ATTN
    return pl.pallas_call(
        _attn_kernel,
        grid=(B, N_KV_HEADS, S // tq),
        in_specs=[pl.BlockSpec((1, tq, GQA_GROUP * LANE), lambda b, h, i: (b, i, h)),
                  pl.BlockSpec((1, Lk, LANE), lambda b, h, i: (b, 0, h)),
                  pl.BlockSpec((1, Lk, LANE), lambda b, h, i: (b, 0, h))],
        out_specs=pl.BlockSpec((1, tq, 2 * LANE), lambda b, h, i: (b, i, h)),
        out_shape=jax.ShapeDtypeStruct((B, S, ATTN_PAD), BF16),
        name="attn",
    )(q, k_all, v_all)


def _mix_kernel(x_ref, yp_ref, at_ref, wop_ref, woa_ref, mod_ref, g_ref, b_ref, wr_ref, rb_ref, tri_ref,
                xmid_ref, hp_ref, idx_ref, wts_ref, rank_ref, cnt_ref, carry):
    i = pl.program_id(0)
    tm = x_ref.shape[0]

    @pl.when(i == 0)
    def _():
        carry[...] = jnp.zeros_like(carry)

    mix = _dot(yp_ref[...], wop_ref[...]) + _dot(at_ref[...], woa_ref[...])
    g1 = mod_ref[0, 2:3, :]
    sh2 = mod_ref[0, 3:4, :]
    sc2 = mod_ref[0, 4:5, :]
    xmid = _layer_norm(ALPHA * x_ref[...] + g1 * mix, g_ref[...], b_ref[...])
    xmid_ref[...] = xmid
    h = xmid * (1.0 + sc2) + sh2
    _store_rows(hp_ref, _pack2(h[:, :HALF_D], h[:, HALF_D:]))

    logits = lax.dot_general(wr_ref[...], h, _NT, precision=lax.Precision.HIGHEST,
                             preferred_element_type=F32)
    score = jax.nn.sigmoid(logits)
    sel = score + rb_ref[...]
    neg = jnp.float32(-jnp.inf)
    sub = lax.broadcasted_iota(I32, (GROUP_SIZE, tm), 0)
    grp_sel, grp_score = [], []
    for g in range(N_GROUPS):
        sg = sel[g * GROUP_SIZE:(g + 1) * GROUP_SIZE, :]
        m1 = jnp.max(sg, axis=0, keepdims=True)
        i1 = jnp.min(jnp.where(sg == m1, sub, GROUP_SIZE), axis=0, keepdims=True)
        m2 = jnp.max(jnp.where(sub == i1, neg, sg), axis=0, keepdims=True)
        grp_sel.append(sg)
        grp_score.append(m1 + m2)
    masked = []
    for g in range(N_GROUPS):
        ahead = jnp.zeros((1, tm), I32)
        for o in range(N_GROUPS):
            if o == g:
                continue
            beats = (grp_score[o] > grp_score[g]) | ((grp_score[o] == grp_score[g]) & (o < g))
            ahead = ahead + beats.astype(I32)
        masked.append(jnp.where(ahead < TOPK_GROUPS, grp_sel[g], neg))
    cur = jnp.concatenate(masked, axis=0)
    row = lax.broadcasted_iota(I32, (N_EXPERTS, tm), 0)
    member = jnp.zeros((N_EXPERTS, tm), F32)
    picks, wsel = [], []
    for k in range(TOP_K):
        mx = jnp.max(cur, axis=0, keepdims=True)
        ei = jnp.min(jnp.where(cur == mx, row, N_EXPERTS), axis=0, keepdims=True)
        hit = row == ei
        picks.append(ei)
        wsel.append(jnp.sum(jnp.where(hit, score, 0.0), axis=0, keepdims=True))
        cur = jnp.where(hit, neg, cur)
        member = jnp.where(hit, 1.0, member)
    tot = wsel[0]
    for k in range(1, TOP_K):
        tot = tot + wsel[k]
    before = _dot(member.astype(BF16), tri_ref[...]) + carry[:, 0:1]
    for k in range(TOP_K):
        idx_ref[k:k + 1, :] = picks[k]
        wts_ref[k:k + 1, :] = wsel[k] / tot * ROUTED_SCALE
        rk = jnp.sum(jnp.where(row == picks[k], before, 0.0), axis=0, keepdims=True)
        rank_ref[k:k + 1, :] = rk.astype(I32)
    carry[...] = carry[...] + jnp.sum(member, axis=1, keepdims=True)
    cnt_ref[...] = carry[...].astype(I32)


def _mix_call(x2, ypool, attn, wop, woa, mod3, g, b, wr_t, rb, tri, tiles_per_seq):
    T = x2.shape[0]
    tm = TM_MIX
    row = lambda i: (i, 0)
    fixed = lambda i: (0, 0)
    tok = lambda i: (0, i)
    return pl.pallas_call(
        _mix_kernel,
        grid=(T // tm,),
        in_specs=[pl.BlockSpec((tm, D_MODEL), row),
                  pl.BlockSpec((tm, POOL_WIDTH), row),
                  pl.BlockSpec((tm, ATTN_PAD), row),
                  pl.BlockSpec((POOL_WIDTH, D_MODEL), fixed),
                  pl.BlockSpec((ATTN_PAD, D_MODEL), fixed),
                  pl.BlockSpec((1, 6, D_MODEL), lambda i: (i // tiles_per_seq, 0, 0)),
                  pl.BlockSpec((1, D_MODEL), fixed),
                  pl.BlockSpec((1, D_MODEL), fixed),
                  pl.BlockSpec((N_EXPERTS, D_MODEL), fixed),
                  pl.BlockSpec((N_EXPERTS, 1), fixed),
                  pl.BlockSpec((tm, tm), fixed)],
        out_specs=[pl.BlockSpec((tm, D_MODEL), row),
                   pl.BlockSpec((tm * ROW_SUB, LANE), row),
                   pl.BlockSpec((TOP_K, tm), tok),
                   pl.BlockSpec((TOP_K, tm), tok),
                   pl.BlockSpec((TOP_K, tm), tok),
                   pl.BlockSpec((N_EXPERTS, LANE), fixed)],
        out_shape=[jax.ShapeDtypeStruct((T, D_MODEL), F32),
                   jax.ShapeDtypeStruct((T * ROW_SUB, LANE), I32),
                   jax.ShapeDtypeStruct((TOP_K, T), I32),
                   jax.ShapeDtypeStruct((TOP_K, T), F32),
                   jax.ShapeDtypeStruct((TOP_K, T), I32),
                   jax.ShapeDtypeStruct((N_EXPERTS, LANE), I32)],
        scratch_shapes=[pltpu.VMEM((N_EXPERTS, LANE), F32)],
        compiler_params=pltpu.CompilerParams(dimension_semantics=("arbitrary",)),
        name="mix",
    )(x2, ypool, attn, wop, woa, mod3, g, b, wr_t, rb, tri)


def _moe_kernel(be_ref, nv_ref, nreal_ref, rtok_hbm, rdst_hbm, roww_ref, h_hbm, wg_ref, wu_ref, wd_ref,
                o_hbm, tok_s, dst_s, xbuf, ybuf, wgb, wub, wdb, isem, gsem, ssem):
    bm = tok_s.shape[1]
    i = pl.program_id(0)
    n = nreal_ref[0]
    slot = i % 2
    nxt = 1 - slot

    def idx_copies(blk, s):
        off = pl.multiple_of(blk * bm, bm)
        return (pltpu.make_async_copy(rtok_hbm.at[pl.ds(off, bm)], tok_s.at[s], isem.at[0, s]),
                pltpu.make_async_copy(rdst_hbm.at[pl.ds(off, bm)], dst_s.at[s], isem.at[1, s]))

    def slab(r):
        return pl.ds(pl.multiple_of(r * ROW_SUB, ROW_SUB), ROW_SUB)

    def gather_copy(s, r, src_row):
        return pltpu.make_async_copy(h_hbm.at[slab(src_row)], xbuf.at[s, slab(r)], gsem.at[s])

    def scatter_copy(s, r, dst_row):
        return pltpu.make_async_copy(ybuf.at[s, slab(r)], o_hbm.at[slab(dst_row)], ssem.at[s])

    def start_gather(s):
        def body(r, c):
            gather_copy(s, r, tok_s[s, r]).start()
            return c
        lax.fori_loop(0, bm, body, 0, unroll=8)

    def wait_gather(s):
        pltpu.make_async_copy(h_hbm.at[pl.ds(0, bm * ROW_SUB)], xbuf.at[s], gsem.at[s]).wait()

    def start_scatter(s, nv):
        def body(r, c):
            scatter_copy(s, r, dst_s[s, r]).start()
            return c

        @pl.when(nv == bm)
        def _():
            lax.fori_loop(0, bm, body, 0, unroll=8)

        @pl.when(nv < bm)
        def _():
            lax.fori_loop(0, nv, body, 0)

    def wait_scatter(s, nv):
        @pl.when(nv == bm)
        def _():
            pltpu.make_async_copy(ybuf.at[s], o_hbm.at[pl.ds(0, bm * ROW_SUB)], ssem.at[s]).wait()

        @pl.when(nv < bm)
        def _():
            def body(r, c):
                scatter_copy(s, r, 0).wait()
                return c
            lax.fori_loop(0, nv, body, 0)

    @pl.when((i == 0) & (n > 0))
    def _():
        for cp in idx_copies(0, 0):
            cp.start()
        for cp in idx_copies(0, 0):
            cp.wait()
        start_gather(0)

        @pl.when(n > 1)
        def _():
            for cp in idx_copies(1, 1):
                cp.start()

    @pl.when(i + 1 < n)
    def _():
        for cp in idx_copies(i + 1, nxt):
            cp.wait()
        start_gather(nxt)

    @pl.when(i < n)
    def _():
        e_changed = (i == 0) | (be_ref[i] != be_ref[jnp.maximum(i - 1, 0)])

        @pl.when(e_changed)
        def _():
            wgb[...] = wg_ref[0].astype(BF16)
            wub[...] = wu_ref[0].astype(BF16)
            wdb[...] = wd_ref[0].astype(BF16)

        wait_gather(slot)

        @pl.when(i >= 2)
        def _():
            wait_scatter(slot, nv_ref[jnp.maximum(i - 2, 0)])

        halves = [_unpack2(w) for w in _load_rows(xbuf.at[slot], bm)]
        xb = jnp.concatenate([lo.astype(BF16) for lo, _ in halves] + [hi.astype(BF16) for _, hi in halves], axis=1)
        a = _silu(_dot(xb, wgb[...])) * _dot(xb, wub[...]) * roww_ref[...]
        y = _dot(a.astype(BF16), wdb[...])
        _store_rows(ybuf.at[slot], _pack2(y[:, :HALF_D], y[:, HALF_D:]))
        start_scatter(slot, nv_ref[i])

        @pl.when(i + 2 < n)
        def _():
            for cp in idx_copies(i + 2, slot):
                cp.start()

        @pl.when(i == n - 1)
        def _():
            @pl.when(i >= 1)
            def _():
                wait_scatter(nxt, nv_ref[jnp.maximum(i - 1, 0)])
            wait_scatter(slot, nv_ref[i])


def _moe_call(block_expert, n_valid, n_real, row_tok, row_dst, row_w, hp, w_gate, w_up, w_down, out_rows):
    bm = BM_MOE
    nb = block_expert.shape[0]
    wmap = lambda i, be, nv, nr: (be[i], 0, 0)
    grid_spec = pltpu.PrefetchScalarGridSpec(
        num_scalar_prefetch=3,
        grid=(nb,),
        in_specs=[pl.BlockSpec(memory_space=pl.ANY),
                  pl.BlockSpec(memory_space=pl.ANY),
                  pl.BlockSpec((bm, 1), lambda i, be, nv, nr: (i, 0)),
                  pl.BlockSpec(memory_space=pl.ANY),
                  pl.BlockSpec((1, D_MODEL, EXPERT_FF), wmap),
                  pl.BlockSpec((1, D_MODEL, EXPERT_FF), wmap),
                  pl.BlockSpec((1, EXPERT_FF, D_MODEL), wmap)],
        out_specs=pl.BlockSpec(memory_space=pl.ANY),
        scratch_shapes=[pltpu.SMEM((2, bm), I32),
                        pltpu.SMEM((2, bm), I32),
                        pltpu.VMEM((2, bm * ROW_SUB, LANE), I32),
                        pltpu.VMEM((2, bm * ROW_SUB, LANE), I32),
                        pltpu.VMEM((D_MODEL, EXPERT_FF), BF16),
                        pltpu.VMEM((D_MODEL, EXPERT_FF), BF16),
                        pltpu.VMEM((EXPERT_FF, D_MODEL), BF16),
                        pltpu.SemaphoreType.DMA((2, 2)),
                        pltpu.SemaphoreType.DMA((2,)),
                        pltpu.SemaphoreType.DMA((2,))],
    )
    return pl.pallas_call(
        _moe_kernel,
        grid_spec=grid_spec,
        out_shape=jax.ShapeDtypeStruct((out_rows * ROW_SUB, LANE), I32),
        compiler_params=pltpu.CompilerParams(dimension_semantics=("arbitrary",)),
        name="moe",
    )(block_expert, n_valid, n_real, row_tok, row_dst, row_w, hp, w_gate, w_up, w_down)


def _combine_kernel(*refs):
    r_refs = refs[:TOP_K]
    xmid_ref, mod_ref, wsg_ref, wsu_ref, wsd_ref, g_ref, b_ref, o_ref = refs[TOP_K:]
    tm = xmid_ref.shape[0]
    acc = None
    for k in range(TOP_K):
        parts = [_unpack2(w) for w in _load_rows(r_refs[k], tm)]
        parts = [lo for lo, _ in parts] + [hi for _, hi in parts]
        acc = parts if acc is None else [a + p for a, p in zip(acc, parts)]
    routed = jnp.concatenate(acc, axis=1)
    xmid = xmid_ref[...]
    sh2 = mod_ref[0, 3:4, :]
    sc2 = mod_ref[0, 4:5, :]
    g2 = mod_ref[0, 5:6, :]
    hb = (xmid * (1.0 + sc2) + sh2).astype(BF16)
    a = _silu(_dot(hb, wsg_ref[...])) * _dot(hb, wsu_ref[...])
    shared = _dot(a.astype(BF16), wsd_ref[...])
    o_ref[...] = _layer_norm(ALPHA * xmid + g2 * (routed + shared), g_ref[...], b_ref[...])


def _combine_call(routed_rows, xmid, mod3, wsg, wsu, wsd, g, b, tiles_per_seq):
    T = xmid.shape[0]
    tm = TM_COMBINE
    nt = T // tm
    fixed = lambda i: (0, 0)
    r_specs = [pl.BlockSpec((tm * ROW_SUB, LANE), functools.partial(lambda i, k: (k * nt + i, 0), k=k))
               for k in range(TOP_K)]
    return pl.pallas_call(
        _combine_kernel,
        grid=(nt,),
        in_specs=r_specs + [pl.BlockSpec((tm, D_MODEL), lambda i: (i, 0)),
                            pl.BlockSpec((1, 6, D_MODEL), lambda i: (i // tiles_per_seq, 0, 0)),
                            pl.BlockSpec((D_MODEL, SHARED_FF), fixed),
                            pl.BlockSpec((D_MODEL, SHARED_FF), fixed),
                            pl.BlockSpec((SHARED_FF, D_MODEL), fixed),
                            pl.BlockSpec((1, D_MODEL), fixed),
                            pl.BlockSpec((1, D_MODEL), fixed)],
        out_specs=pl.BlockSpec((tm, D_MODEL), lambda i: (i, 0)),
        out_shape=jax.ShapeDtypeStruct((T, D_MODEL), F32),
        name="combine",
    )(*([routed_rows] * TOP_K), xmid, mod3, wsg, wsu, wsd, g, b)


def _head_cols(w, n_heads):
    d = w.shape[0]
    w = w.reshape(d, n_heads, HALF, 2)
    z = jnp.zeros((d, n_heads, HALF), w.dtype)
    return jnp.concatenate([w[..., 0], z, w[..., 1], z], axis=-1).reshape(d, n_heads * LANE)


def _head_vec(g):
    g = g.reshape(HALF, 2)
    z = jnp.zeros((HALF,), g.dtype)
    return jnp.concatenate([g[:, 0], z, g[:, 1], z]).reshape(1, LANE)


def _rope_tables(seq):
    inv_freq = ROPE_THETA ** (-jnp.arange(0, HALF, 2, dtype=F32) / HALF)
    pos = jnp.arange(seq)
    rowp = (pos // GRID_W).astype(F32)
    colp = (pos % GRID_W).astype(F32)
    ang = jnp.concatenate([rowp[:, None] * inv_freq, colp[:, None] * inv_freq], axis=-1)
    cos, sin = jnp.cos(ang), jnp.sin(ang)
    z = jnp.zeros_like(cos)
    return jnp.concatenate([cos, z, cos, z], axis=-1), jnp.concatenate([-sin, z, sin, z], axis=-1)


def kernel(x, c, ctx, c_ctx, w_mod, b_mod, w_in, q_norm, k_norm, pool_w, pool_scale, w_out, ln1_g, ln1_b,
           w_router, router_bias, w_gate, w_up, w_down, ws_gate, ws_up, ws_down, ln2_g, ln2_b):
    B, S, D = x.shape
    C = ctx.shape[1]
    T = B * S
    assert D == D_MODEL and w_mod.shape[0] == DEPTH and B + 1 <= MOD_ROWS
    assert S % TM_INPROJ == 0 and S % TQ_ATTN == 0 and S % TM_MIX == 0 and S % TM_COMBINE == 0
    assert S % GRID_W == 0 and C % SUBLANE == 0

    cc = jnp.concatenate([c, c_ctx[None, :], jnp.zeros((MOD_ROWS - B - 1, D), F32)], axis=0)
    mod3 = _mod_call(cc, w_mod[0], b_mod[0][None, :]).reshape(MOD_ROWS, 6, D)

    w = w_in[0]
    o1, o2, o3 = POOL_WIDTH, POOL_WIDTH + ATTN_WIDTH, POOL_WIDTH + ATTN_WIDTH + KV_WIDTH
    wq = _head_cols(w[:, o1:o2], N_HEADS)
    wk = _head_cols(w[:, o2:o3], N_KV_HEADS)
    w_all = jnp.concatenate([w[:, :o1], wq, wk, w[:, o3:]], axis=1).astype(BF16)
    w_kv = jnp.concatenate([wk, w[:, o3:]], axis=1).astype(BF16)
    qg, kg = _head_vec(q_norm[0]), _head_vec(k_norm[0])
    cos_t, sin_t = _rope_tables(S)
    qscale = HEAD_DIM ** -0.5
    u, q, k_l, v_l = _inproj_call(x, mod3, w_all, qg, kg, cos_t * qscale, sin_t * qscale, cos_t, sin_t)
    k_c, v_c = _ctx_kv_call(ctx, mod3, w_kv, kg, B)

    k_all = jnp.concatenate([k_c, k_l], axis=1)
    v_all = jnp.concatenate([v_c, v_l], axis=1).reshape(B, C + S, N_KV_HEADS, HEAD_DIM)
    v_all = jnp.pad(v_all, ((0, 0), (0, 0), (0, 0), (0, LANE - HEAD_DIM))).reshape(B, C + S, K_COLS)
    attn = _attn_call(q, k_all, v_all)

    bd = jax.scipy.linalg.block_diag(*[pool_w[0, g] for g in range(len(POOL_WINDOWS))]).astype(BF16)
    ypool = _pool_call(u, bd, pool_scale[0][None, :])

    wo = w_out[0]
    wop = wo[:POOL_WIDTH].astype(BF16)
    woa = wo[POOL_WIDTH:].reshape(N_KV_HEADS, GQA_GROUP * HEAD_DIM, D)
    woa = jnp.pad(woa, ((0, 0), (0, 2 * LANE - GQA_GROUP * HEAD_DIM), (0, 0))).reshape(ATTN_PAD, D).astype(BF16)
    tri = (jnp.arange(TM_MIX)[:, None] < jnp.arange(TM_MIX)[None, :]).astype(BF16)
    xmid, hp, idx_t, wts_t, rank_t, counts = _mix_call(
        x.reshape(T, D), ypool.reshape(T, POOL_WIDTH), attn.reshape(T, ATTN_PAD), wop, woa, mod3,
        ln1_g[0][None, :], ln1_b[0][None, :], w_router[0].T, router_bias[0][:, None], tri, S // TM_MIX)

    bm = BM_MOE
    counts = counts[:, 0]
    padded = ((counts + bm - 1) // bm) * bm
    pad_end = jnp.cumsum(padded)
    pad_start = pad_end - padded
    A = T * TOP_K
    nb = A // bm + N_EXPERTS
    P = nb * bm
    n_real = (pad_end[-1] // bm).astype(I32).reshape(1)
    block_expert = jnp.minimum(jnp.searchsorted(pad_end, jnp.arange(nb) * bm, side='right'),
                               N_EXPERTS - 1).astype(I32)
    dest = (pad_start[idx_t] + rank_t).reshape(-1)
    tok = jnp.broadcast_to(jnp.arange(T, dtype=I32)[None, :], (TOP_K, T)).reshape(-1)
    n_valid = jnp.clip((pad_start + counts)[block_expert] - jnp.arange(nb) * bm, 0, bm).astype(I32)
    row_tok = jnp.zeros((P,), I32).at[dest].set(tok)
    row_dst = jnp.zeros((P,), I32).at[dest].set(jnp.arange(A, dtype=I32))
    row_w = jnp.zeros((P,), F32).at[dest].set(wts_t.reshape(-1))

    routed_rows = _moe_call(block_expert, n_valid, n_real, row_tok, row_dst, row_w[:, None], hp,
                            w_gate[0], w_up[0], w_down[0], A)
    out = _combine_call(routed_rows, xmid, mod3, ws_gate[0].astype(BF16), ws_up[0].astype(BF16),
                        ws_down[0].astype(BF16), ln2_g[0][None, :], ln2_b[0][None, :], S // TM_COMBINE)
    return out.reshape(B, S, D)
```

```python
import functools

import jax
import jax.numpy as jnp
from jax import lax
from jax.experimental import pallas as pl
from jax.experimental.pallas import tpu as pltpu

F32 = jnp.float32
BF16 = jnp.bfloat16
I32 = jnp.int32

LANE = 128
SUBLANE = 8

D_MODEL = 1024
GRID_W = 64
POOL_WIDTH = 256
POOL_WINDOWS = (2, 4, 8, 16)
POOL_GROUP = 64
HEAD_DIM = 64
HALF = HEAD_DIM // 2
N_HEADS = 12
N_KV_HEADS = 4
GQA_GROUP = N_HEADS // N_KV_HEADS
ATTN_WIDTH = N_HEADS * HEAD_DIM
KV_WIDTH = N_KV_HEADS * HEAD_DIM
ROPE_THETA = 10000.0
N_EXPERTS = 64
TOP_K = 8
N_GROUPS = 8
GROUP_SIZE = N_EXPERTS // N_GROUPS
TOPK_GROUPS = 4
EXPERT_FF = 256
SHARED_FF = 256
ROUTED_SCALE = 2.5
DEPTH = 1
ALPHA = (2.0 * DEPTH) ** 0.25
LN_EPS = 1e-5
RMS_EPS = 1e-6

MOD_ROWS = 24
HALF_D = D_MODEL // 2
ROW_SUB = HALF_D // LANE
Q_COLS = N_HEADS * LANE
K_COLS = N_KV_HEADS * LANE
ATTN_PAD = N_KV_HEADS * 2 * LANE

TM_INPROJ = 512
TQ_ATTN = 256
TM_MIX = 256
BM_MOE = 256
TM_ROUTE = 256

_NT = (((1,), (1,)), ((), ()))


def _dot(a, b):
    return jnp.dot(a, b, preferred_element_type=F32)


def _pack2(lo, hi):
    lo_bits = lax.bitcast_convert_type(lo.astype(BF16).astype(F32), I32)
    hi_bits = lax.bitcast_convert_type(hi.astype(BF16).astype(F32), I32)
    return lax.shift_right_logical(lo_bits, 16) | (hi_bits & jnp.int32(-65536))


def _unpack2(w):
    lo = lax.bitcast_convert_type(lax.shift_left(w, 16), F32)
    hi = lax.bitcast_convert_type(w & jnp.int32(-65536), F32)
    return lo, hi


def _store_rows(ref, packed):
    n = packed.shape[0]
    for s in range(ROW_SUB):
        ref[pl.ds(s, n, stride=ROW_SUB), :] = packed[:, s * LANE:(s + 1) * LANE]


def _load_rows(ref, n):
    return [ref[pl.ds(s, n, stride=ROW_SUB), :] for s in range(ROW_SUB)]


def _silu(x):
    return x * jax.nn.sigmoid(x)


def _layer_norm(r, g, b):
    mu = jnp.mean(r, axis=-1, keepdims=True)
    d = r - mu
    var = jnp.mean(d * d, axis=-1, keepdims=True)
    return d * lax.rsqrt(var + LN_EPS) * g + b


def _mod_kernel(c_ref, w_ref, b_ref, o_ref):
    a = _silu(c_ref[...])
    o_ref[...] = jnp.dot(a, w_ref[...], precision=lax.Precision.HIGHEST,
                         preferred_element_type=F32) + b_ref[...]


def _mod_call(cc, w_mod, b_mod):
    n = w_mod.shape[1]
    tn = 512
    return pl.pallas_call(
        _mod_kernel,
        grid=(n // tn,),
        in_specs=[pl.BlockSpec((MOD_ROWS, D_MODEL), lambda j: (0, 0)),
                  pl.BlockSpec((D_MODEL, tn), lambda j: (0, j)),
                  pl.BlockSpec((1, tn), lambda j: (0, j))],
        out_specs=pl.BlockSpec((MOD_ROWS, tn), lambda j: (0, j)),
        out_shape=jax.ShapeDtypeStruct((MOD_ROWS, n), F32),
        name="mod",
    )(cc, w_mod, b_mod)


def _norm_head(seg, g):
    ms = jnp.sum(seg * seg, axis=-1, keepdims=True) * (1.0 / HEAD_DIM)
    return seg * lax.rsqrt(ms + RMS_EPS) * g


def _rope(xn, c, s):
    return xn * c + pltpu.roll(xn, LANE // 2, axis=1) * s


def _inproj_kernel(x_ref, mod_ref, w_ref, qg_ref, kg_ref, cq_ref, sq_ref, ck_ref, sk_ref,
                   u_ref, q_ref, k_ref, v_ref):
    sh = mod_ref[0, 0:1, :]
    sc = mod_ref[0, 1:2, :]
    xm = (x_ref[0] * (1.0 + sc) + sh).astype(BF16)
    u_ref[0] = _dot(xm, w_ref[:, 0:POOL_WIDTH])
    qg, kg = qg_ref[...], kg_ref[...]
    cq, sq, ck, sk = cq_ref[...], sq_ref[...], ck_ref[...], sk_ref[...]
    base = POOL_WIDTH
    for pair in range(N_HEADS // 2):
        p = _dot(xm, w_ref[:, base + pair * 2 * LANE: base + (pair + 1) * 2 * LANE])
        for j in range(2):
            h = pair * 2 + j
            q = _rope(_norm_head(p[:, j * LANE:(j + 1) * LANE], qg), cq, sq)
            q_ref[0, :, h * LANE:(h + 1) * LANE] = q.astype(BF16)
    base += Q_COLS
    for pair in range(N_KV_HEADS // 2):
        p = _dot(xm, w_ref[:, base + pair * 2 * LANE: base + (pair + 1) * 2 * LANE])
        for j in range(2):
            h = pair * 2 + j
            k = _rope(_norm_head(p[:, j * LANE:(j + 1) * LANE], kg), ck, sk)
            k_ref[0, :, h * LANE:(h + 1) * LANE] = k.astype(BF16)
    base += K_COLS
    v_ref[0] = _dot(xm, w_ref[:, base:base + KV_WIDTH]).astype(BF16)


def _inproj_call(x, mod3, w_all, qg, kg, cq, sq, ck, sk):
    B, S, _ = x.shape
    tm = TM_INPROJ
    ncol = w_all.shape[1]
    tab = pl.BlockSpec((tm, LANE), lambda b, i: (i, 0))
    vec = pl.BlockSpec((1, LANE), lambda b, i: (0, 0))
    return pl.pallas_call(
        _inproj_kernel,
        grid=(B, S // tm),
        in_specs=[pl.BlockSpec((1, tm, D_MODEL), lambda b, i: (b, i, 0)),
                  pl.BlockSpec((1, 6, D_MODEL), lambda b, i: (b, 0, 0)),
                  pl.BlockSpec((D_MODEL, ncol), lambda b, i: (0, 0)),
                  vec, vec, tab, tab, tab, tab],
        out_specs=[pl.BlockSpec((1, tm, POOL_WIDTH), lambda b, i: (b, i, 0)),
                   pl.BlockSpec((1, tm, Q_COLS), lambda b, i: (b, i, 0)),
                   pl.BlockSpec((1, tm, K_COLS), lambda b, i: (b, i, 0)),
                   pl.BlockSpec((1, tm, KV_WIDTH), lambda b, i: (b, i, 0))],
        out_shape=[jax.ShapeDtypeStruct((B, S, POOL_WIDTH), F32),
                   jax.ShapeDtypeStruct((B, S, Q_COLS), BF16),
                   jax.ShapeDtypeStruct((B, S, K_COLS), BF16),
                   jax.ShapeDtypeStruct((B, S, KV_WIDTH), BF16)],
        name="inproj",
    )(x, mod3, w_all, qg, kg, cq, sq, ck, sk)


def _ctx_kv_kernel(x_ref, mod_ref, w_ref, kg_ref, k_ref, v_ref):
    sh = mod_ref[0, 0:1, :]
    sc = mod_ref[0, 1:2, :]
    xm = (x_ref[0] * (1.0 + sc) + sh).astype(BF16)
    kg = kg_ref[...]
    for pair in range(N_KV_HEADS // 2):
        p = _dot(xm, w_ref[:, pair * 2 * LANE:(pair + 1) * 2 * LANE])
        for j in range(2):
            h = pair * 2 + j
            k_ref[0, :, h * LANE:(h + 1) * LANE] = _norm_head(p[:, j * LANE:(j + 1) * LANE], kg).astype(BF16)
    v_ref[0] = _dot(xm, w_ref[:, K_COLS:K_COLS + KV_WIDTH]).astype(BF16)


def _ctx_kv_call(ctx, mod3, w_kv, kg, ctx_row):
    B, C, _ = ctx.shape
    return pl.pallas_call(
        _ctx_kv_kernel,
        grid=(B,),
        in_specs=[pl.BlockSpec((1, C, D_MODEL), lambda b: (b, 0, 0)),
                  pl.BlockSpec((1, 6, D_MODEL), lambda b: (ctx_row, 0, 0)),
                  pl.BlockSpec((D_MODEL, K_COLS + KV_WIDTH), lambda b: (0, 0)),
                  pl.BlockSpec((1, LANE), lambda b: (0, 0))],
        out_specs=[pl.BlockSpec((1, C, K_COLS), lambda b: (b, 0, 0)),
                   pl.BlockSpec((1, C, KV_WIDTH), lambda b: (b, 0, 0))],
        out_shape=[jax.ShapeDtypeStruct((B, C, K_COLS), BF16),
                   jax.ShapeDtypeStruct((B, C, KV_WIDTH), BF16)],
        name="ctx_kv",
    )(ctx, mod3, w_kv, kg)


POOL_PAD = 8


def _pool_kernel(u_ref, bd_ref, ps_ref, y_ref):
    S = u_ref.shape[1]
    n = S + 2 * POOL_PAD
    t = lax.broadcasted_iota(I32, (S, LANE), 0)
    lane = lax.broadcasted_iota(I32, (S, LANE), 1)
    zpad = jnp.zeros((POOL_PAD, LANE), F32)
    for half in range(POOL_WIDTH // LANE):
        u = u_ref[0, :, half * LANE:(half + 1) * LANE]
        ue = jnp.concatenate([zpad, u, zpad], axis=0)
        fwd = {1: ue}
        w = 1
        while w < POOL_WINDOWS[2 * half + 1]:
            fwd[2 * w] = fwd[w] + pltpu.roll(fwd[w], n - w, axis=0)
            w *= 2
        ds = []
        for win in POOL_WINDOWS[2 * half: 2 * half + 2]:
            hw = win // 2
            centred = pltpu.roll(fwd[win], hw, axis=0)[POOL_PAD:POOL_PAD + S]
            cnt = (jnp.minimum(t + hw, S) - jnp.maximum(t - hw, 0)).astype(F32)
            ds.append(centred / cnt - u)
        d = jnp.where(lane < POOL_GROUP, ds[0], ds[1]).astype(BF16)
        sl = slice(half * LANE, (half + 1) * LANE)
        y = _dot(d, bd_ref[sl, sl]) * ps_ref[:, sl]
        y_ref[0, :, sl] = y.astype(BF16)


def _pool_call(u, bd, ps):
    B, S, _ = u.shape
    return pl.pallas_call(
        _pool_kernel,
        grid=(B,),
        in_specs=[pl.BlockSpec((1, S, POOL_WIDTH), lambda b: (b, 0, 0)),
                  pl.BlockSpec((POOL_WIDTH, POOL_WIDTH), lambda b: (0, 0)),
                  pl.BlockSpec((1, POOL_WIDTH), lambda b: (0, 0))],
        out_specs=pl.BlockSpec((1, S, POOL_WIDTH), lambda b: (b, 0, 0)),
        out_shape=jax.ShapeDtypeStruct((B, S, POOL_WIDTH), BF16),
        name="pool",
    )(u, bd, ps)


def _attn_kernel(q_ref, k_ref, v_ref, o_ref):
    k = k_ref[0]
    v = v_ref[0]
    outs = []
    for j in range(GQA_GROUP):
        q = q_ref[0, :, j * LANE:(j + 1) * LANE]
        s = lax.dot_general(q, k, _NT, preferred_element_type=F32)
        m = jnp.max(s, axis=-1, keepdims=True)
        p = jnp.exp(s - m)
        l = jnp.sum(p, axis=-1, keepdims=True)
        outs.append(_dot(p.astype(BF16), v) / l)
    o_ref[0, :, 0:LANE] = (outs[0] + pltpu.roll(outs[1], LANE // 2, axis=1)).astype(BF16)
    o_ref[0, :, LANE:2 * LANE] = outs[2].astype(BF16)


def _attn_call(q, k_all, v_all):
    B, S, _ = q.shape
    Lk = k_all.shape[1]
    tq = TQ_ATTN
    return pl.pallas_call(
        _attn_kernel,
        grid=(B, N_KV_HEADS, S // tq),
        in_specs=[pl.BlockSpec((1, tq, GQA_GROUP * LANE), lambda b, h, i: (b, i, h)),
                  pl.BlockSpec((1, Lk, LANE), lambda b, h, i: (b, 0, h)),
                  pl.BlockSpec((1, Lk, LANE), lambda b, h, i: (b, 0, h))],
        out_specs=pl.BlockSpec((1, tq, 2 * LANE), lambda b, h, i: (b, i, h)),
        out_shape=jax.ShapeDtypeStruct((B, S, ATTN_PAD), BF16),
        name="attn",
    )(q, k_all, v_all)


def _mix_kernel(x_ref, yp_ref, at_ref, wop_ref, woa_ref, mod_ref, g_ref, b_ref, wr_ref, rb_ref, tri_ref,
                xmid_ref, hp_ref, idx_ref, wts_ref, rank_ref, cnt_ref, carry):
    i = pl.program_id(0)
    tm = x_ref.shape[0]

    @pl.when(i == 0)
    def _():
        carry[...] = jnp.zeros_like(carry)

    mix = _dot(yp_ref[...], wop_ref[...]) + _dot(at_ref[...], woa_ref[...])
    g1 = mod_ref[0, 2:3, :]
    sh2 = mod_ref[0, 3:4, :]
    sc2 = mod_ref[0, 4:5, :]
    xmid = _layer_norm(ALPHA * x_ref[...] + g1 * mix, g_ref[...], b_ref[...])
    xmid_ref[...] = xmid
    h = xmid * (1.0 + sc2) + sh2
    _store_rows(hp_ref, _pack2(h[:, :HALF_D], h[:, HALF_D:]))

    logits = lax.dot_general(wr_ref[...], h, _NT, precision=lax.Precision.HIGHEST,
                             preferred_element_type=F32)
    score = jax.nn.sigmoid(logits)
    sel = score + rb_ref[...]
    neg = jnp.float32(-jnp.inf)
    sub = lax.broadcasted_iota(I32, (GROUP_SIZE, tm), 0)
    grp_sel, grp_score = [], []
    for g in range(N_GROUPS):
        sg = sel[g * GROUP_SIZE:(g + 1) * GROUP_SIZE, :]
        m1 = jnp.max(sg, axis=0, keepdims=True)
        i1 = jnp.min(jnp.where(sg == m1, sub, GROUP_SIZE), axis=0, keepdims=True)
        m2 = jnp.max(jnp.where(sub == i1, neg, sg), axis=0, keepdims=True)
        grp_sel.append(sg)
        grp_score.append(m1 + m2)
    masked = []
    for g in range(N_GROUPS):
        ahead = jnp.zeros((1, tm), I32)
        for o in range(N_GROUPS):
            if o == g:
                continue
            beats = (grp_score[o] > grp_score[g]) | ((grp_score[o] == grp_score[g]) & (o < g))
            ahead = ahead + beats.astype(I32)
        masked.append(jnp.where(ahead < TOPK_GROUPS, grp_sel[g], neg))
    cur = jnp.concatenate(masked, axis=0)
    row = lax.broadcasted_iota(I32, (N_EXPERTS, tm), 0)
    member = jnp.zeros((N_EXPERTS, tm), F32)
    picks, wsel = [], []
    for k in range(TOP_K):
        mx = jnp.max(cur, axis=0, keepdims=True)
        ei = jnp.min(jnp.where(cur == mx, row, N_EXPERTS), axis=0, keepdims=True)
        hit = row == ei
        picks.append(ei)
        wsel.append(jnp.sum(jnp.where(hit, score, 0.0), axis=0, keepdims=True))
        cur = jnp.where(hit, neg, cur)
        member = jnp.where(hit, 1.0, member)
    tot = wsel[0]
    for k in range(1, TOP_K):
        tot = tot + wsel[k]
    before = _dot(member.astype(BF16), tri_ref[...]) + carry[:, 0:1]
    for k in range(TOP_K):
        idx_ref[k:k + 1, :] = picks[k]
        rk = jnp.sum(jnp.where(row == picks[k], before, 0.0), axis=0, keepdims=True)
        rank_ref[k:k + 1, :] = rk.astype(I32)
    wrows = [wsel[k] / tot * ROUTED_SCALE for k in range(TOP_K)]
    wmat = jnp.concatenate(wrows + [jnp.zeros((LANE - TOP_K, tm), F32)], axis=0)
    wts_ref[...] = wmat.T
    carry[...] = carry[...] + jnp.sum(member, axis=1, keepdims=True)
    cnt_ref[...] = carry[...].astype(I32)


def _mix_call(x2, ypool, attn, wop, woa, mod3, g, b, wr_t, rb, tri, tiles_per_seq):
    T = x2.shape[0]
    tm = TM_MIX
    row = lambda i: (i, 0)
    fixed = lambda i: (0, 0)
    tok = lambda i: (0, i)
    return pl.pallas_call(
        _mix_kernel,
        grid=(T // tm,),
        in_specs=[pl.BlockSpec((tm, D_MODEL), row),
                  pl.BlockSpec((tm, POOL_WIDTH), row),
                  pl.BlockSpec((tm, ATTN_PAD), row),
                  pl.BlockSpec((POOL_WIDTH, D_MODEL), fixed),
                  pl.BlockSpec((ATTN_PAD, D_MODEL), fixed),
                  pl.BlockSpec((1, 6, D_MODEL), lambda i: (i // tiles_per_seq, 0, 0)),
                  pl.BlockSpec((1, D_MODEL), fixed),
                  pl.BlockSpec((1, D_MODEL), fixed),
                  pl.BlockSpec((N_EXPERTS, D_MODEL), fixed),
                  pl.BlockSpec((N_EXPERTS, 1), fixed),
                  pl.BlockSpec((tm, tm), fixed)],
        out_specs=[pl.BlockSpec((tm, D_MODEL), row),
                   pl.BlockSpec((tm * ROW_SUB, LANE), row),
                   pl.BlockSpec((TOP_K, tm), tok),
                   pl.BlockSpec((tm, LANE), row),
                   pl.BlockSpec((TOP_K, tm), tok),
                   pl.BlockSpec((N_EXPERTS, LANE), fixed)],
        out_shape=[jax.ShapeDtypeStruct((T, D_MODEL), F32),
                   jax.ShapeDtypeStruct((T * ROW_SUB, LANE), I32),
                   jax.ShapeDtypeStruct((TOP_K, T), I32),
                   jax.ShapeDtypeStruct((T, LANE), F32),
                   jax.ShapeDtypeStruct((TOP_K, T), I32),
                   jax.ShapeDtypeStruct((N_EXPERTS, LANE), I32)],
        scratch_shapes=[pltpu.VMEM((N_EXPERTS, LANE), F32)],
        compiler_params=pltpu.CompilerParams(dimension_semantics=("arbitrary",)),
        name="mix",
    )(x2, ypool, attn, wop, woa, mod3, g, b, wr_t, rb, tri)


def _slab(r):
    return pl.ds(pl.multiple_of(r * ROW_SUB, ROW_SUB), ROW_SUB)


def _dest_kernel(ps_ref, idx_ref, rank_ref, o_ref):
    idx = idx_ref[...]
    d = rank_ref[...]
    for e in range(N_EXPERTS):
        d = d + jnp.where(idx == e, ps_ref[e], 0)
    tm = o_ref.shape[1]
    for j in range(idx.shape[1] // tm):
        o_ref[j * TOP_K:(j + 1) * TOP_K, :] = d[:, j * tm:(j + 1) * tm]


def _dest_call(pad_start, idx_t, rank_t):
    T = idx_t.shape[1]
    tm = TM_ROUTE
    td = 2048 if T % 2048 == 0 else tm
    per = td // tm
    grid_spec = pltpu.PrefetchScalarGridSpec(
        num_scalar_prefetch=1,
        grid=(T // td,),
        in_specs=[pl.BlockSpec((TOP_K, td), lambda i, ps: (0, i)),
                  pl.BlockSpec((TOP_K, td), lambda i, ps: (0, i))],
        out_specs=pl.BlockSpec((per * TOP_K, tm), lambda i, ps: (i, 0)),
    )
    return pl.pallas_call(
        _dest_kernel,
        grid_spec=grid_spec,
        out_shape=jax.ShapeDtypeStruct((T // tm * TOP_K, tm), I32),
        name="dest",
    )(pad_start, idx_t, rank_t)


def _dispatch_kernel(pst_ref, plen_ref, nreal_ref, dest_hbm, hp_ref, xs_hbm, dst0, dst1, zbuf, isem, ssem, zsem):
    tm = hp_ref.shape[0] // ROW_SUB
    bm = zbuf.shape[0] // ROW_SUB
    n_idx = TOP_K * tm
    nb = xs_hbm.shape[0] // (bm * ROW_SUB)
    i = pl.program_id(0)
    nt = pl.num_programs(0)
    dst = (dst0, dst1)

    def idx_copy(tile, s):
        off = pl.multiple_of(tile * n_idx, n_idx)
        return pltpu.make_async_copy(dest_hbm.at[pl.ds(off, n_idx)], dst[s], isem.at[s])

    def pad_fill(e):
        n = plen_ref[e] * ROW_SUB
        start = pl.multiple_of(pst_ref[e] * ROW_SUB, ROW_SUB)
        return pltpu.make_async_copy(zbuf.at[pl.ds(0, n)], xs_hbm.at[pl.ds(start, n)], zsem)

    def dead_fill(j):
        return pltpu.make_async_copy(zbuf, xs_hbm.at[pl.ds(j * bm * ROW_SUB, bm * ROW_SUB)], zsem)

    def fills(act):
        for e in range(N_EXPERTS):
            pl.when(plen_ref[e] > 0)(functools.partial(lambda e: act(pad_fill(e)), e))
        for j in range(nb - N_EXPERTS, nb):
            pl.when(j >= nreal_ref[0])(functools.partial(lambda j: act(dead_fill(j)), j))

    @pl.when(i == 0)
    def _():
        zbuf[...] = jnp.zeros_like(zbuf)
        idx_copy(0, 0).start()
        idx_copy(0, 0).wait()
        fills(lambda cp: cp.start())

    def scatter(s):
        @pl.when(i + 1 < nt)
        def _():
            idx_copy(i + 1, 1 - s).start()

        def body(rb, c):
            for rr in range(SUBLANE):
                r = rb * SUBLANE + rr
                for k in range(TOP_K):
                    pltpu.make_async_copy(hp_ref.at[_slab(r)], xs_hbm.at[_slab(dst[s][k * tm + r])], ssem).start()
            return c
        lax.fori_loop(0, tm // SUBLANE, body, 0)
        for k in range(TOP_K):
            pltpu.make_async_copy(hp_ref, xs_hbm.at[pl.ds(0, tm * ROW_SUB)], ssem).wait()

        @pl.when(i + 1 < nt)
        def _():
            idx_copy(i + 1, 1 - s).wait()

    for s in range(2):
        pl.when(i % 2 == s)(functools.partial(scatter, s))

    @pl.when(i == 0)
    def _():
        fills(lambda cp: cp.wait())


def _dispatch_call(pad_row_start, pad_len, n_real, dest, hp, nb):
    tm = TM_ROUTE
    bm = BM_MOE
    T = hp.shape[0] // ROW_SUB
    grid_spec = pltpu.PrefetchScalarGridSpec(
        num_scalar_prefetch=3,
        grid=(T // tm,),
        in_specs=[pl.BlockSpec(memory_space=pl.ANY),
                  pl.BlockSpec((tm * ROW_SUB, LANE), lambda i, a, b, c: (i, 0))],
        out_specs=pl.BlockSpec(memory_space=pl.ANY),
        scratch_shapes=[pltpu.SMEM((TOP_K * tm,), I32),
                        pltpu.SMEM((TOP_K * tm,), I32),
                        pltpu.VMEM((bm * ROW_SUB, LANE), I32),
                        pltpu.SemaphoreType.DMA((2,)),
                        pltpu.SemaphoreType.DMA,
                        pltpu.SemaphoreType.DMA],
    )
    return pl.pallas_call(
        _dispatch_kernel,
        grid_spec=grid_spec,
        out_shape=jax.ShapeDtypeStruct((nb * bm * ROW_SUB, LANE), I32),
        compiler_params=pltpu.CompilerParams(dimension_semantics=("arbitrary",)),
        name="dispatch",
    )(pad_row_start, pad_len, n_real, dest, hp)


def _moe_kernel(be_ref, nreal_ref, x_ref, wg_ref, wu_ref, wd_ref, y_ref, wgb, wub, wdb):
    bm = x_ref.shape[0] // ROW_SUB
    i = pl.program_id(0)
    n = nreal_ref[0]

    @pl.when(i < n)
    def _():
        e_changed = (i == 0) | (be_ref[i] != be_ref[jnp.maximum(i - 1, 0)])

        @pl.when(e_changed)
        def _():
            wgb[...] = wg_ref[0].astype(BF16)
            wub[...] = wu_ref[0].astype(BF16)
            wdb[...] = wd_ref[0].astype(BF16)

        halves = [_unpack2(w) for w in _load_rows(x_ref, bm)]
        xb = jnp.concatenate([lo.astype(BF16) for lo, _ in halves] + [hi.astype(BF16) for _, hi in halves], axis=1)
        a = _silu(_dot(xb, wgb[...])) * _dot(xb, wub[...])
        y = _dot(a.astype(BF16), wdb[...])
        _store_rows(y_ref, _pack2(y[:, :HALF_D], y[:, HALF_D:]))

    @pl.when(i >= n)
    def _():
        y_ref[...] = jnp.zeros_like(y_ref)


def _moe_call(block_expert, n_real, xs, w_gate, w_up, w_down):
    bm = BM_MOE
    nb = block_expert.shape[0]
    wmap = lambda i, be, nr: (be[i], 0, 0)
    rows = pl.BlockSpec((bm * ROW_SUB, LANE), lambda i, be, nr: (i, 0))
    grid_spec = pltpu.PrefetchScalarGridSpec(
        num_scalar_prefetch=2,
        grid=(nb,),
        in_specs=[rows,
                  pl.BlockSpec((1, D_MODEL, EXPERT_FF), wmap),
                  pl.BlockSpec((1, D_MODEL, EXPERT_FF), wmap),
                  pl.BlockSpec((1, EXPERT_FF, D_MODEL), wmap)],
        out_specs=rows,
        scratch_shapes=[pltpu.VMEM((D_MODEL, EXPERT_FF), BF16),
                        pltpu.VMEM((D_MODEL, EXPERT_FF), BF16),
                        pltpu.VMEM((EXPERT_FF, D_MODEL), BF16)],
    )
    return pl.pallas_call(
        _moe_kernel,
        grid_spec=grid_spec,
        out_shape=jax.ShapeDtypeStruct(xs.shape, I32),
        compiler_params=pltpu.CompilerParams(dimension_semantics=("arbitrary",)),
        name="moe",
    )(block_expert, n_real, xs, w_gate, w_up, w_down)


def _combine_kernel(dest_hbm, ys_hbm, wc_ref, xmid_ref, mod_ref, wsg_ref, wsu_ref, wsd_ref, g_ref, b_ref,
                    o_ref, dst0, dst1, rbuf, isem, gsem):
    tm = xmid_ref.shape[0]
    n_idx = TOP_K * tm
    i = pl.program_id(0)
    nt = pl.num_programs(0)
    dst = (dst0, dst1)

    def idx_copy(tile, s):
        off = pl.multiple_of(tile * n_idx, n_idx)
        return pltpu.make_async_copy(dest_hbm.at[pl.ds(off, n_idx)], dst[s], isem.at[s])

    def start_gather(s):
        def body(rb, c):
            for rr in range(SUBLANE):
                r = rb * SUBLANE + rr
                for k in range(TOP_K):
                    pltpu.make_async_copy(ys_hbm.at[_slab(dst[s][k * tm + r])], rbuf.at[s, k, _slab(r)],
                                          gsem.at[s]).start()
            return c
        lax.fori_loop(0, tm // SUBLANE, body, 0)

    def wait_gather(s):
        for k in range(TOP_K):
            pltpu.make_async_copy(ys_hbm.at[pl.ds(0, tm * ROW_SUB)], rbuf.at[s, k], gsem.at[s]).wait()

    @pl.when(i == 0)
    def _():
        idx_copy(0, 0).start()
        idx_copy(0, 0).wait()
        start_gather(0)

        @pl.when(nt > 1)
        def _():
            idx_copy(1, 1).start()

    def prefetch(s):
        @pl.when(i + 2 < nt)
        def _():
            idx_copy(i + 2, s).start()

        @pl.when(i + 1 < nt)
        def _():
            idx_copy(i + 1, 1 - s).wait()
            start_gather(1 - s)

        wait_gather(s)

    for s in range(2):
        pl.when(i % 2 == s)(functools.partial(prefetch, s))

    wc = wc_ref[...]
    slot = i % 2
    acc = None
    for k in range(TOP_K):
        parts = [_unpack2(w) for w in _load_rows(rbuf.at[slot, k], tm)]
        parts = [lo for lo, _ in parts] + [hi for _, hi in parts]
        wk = wc[:, k:k + 1]
        acc = [wk * p for p in parts] if acc is None else [a + wk * p for a, p in zip(acc, parts)]
    routed = jnp.concatenate(acc, axis=1)
    xmid = xmid_ref[...]
    sh2 = mod_ref[0, 3:4, :]
    sc2 = mod_ref[0, 4:5, :]
    g2 = mod_ref[0, 5:6, :]
    hb = (xmid * (1.0 + sc2) + sh2).astype(BF16)
    a = _silu(_dot(hb, wsg_ref[...])) * _dot(hb, wsu_ref[...])
    shared = _dot(a.astype(BF16), wsd_ref[...])
    o_ref[...] = _layer_norm(ALPHA * xmid + g2 * (routed + shared), g_ref[...], b_ref[...])


def _combine_call(dest, ys, wcol, xmid, mod3, wsg, wsu, wsd, g, b, tiles_per_seq):
    T = xmid.shape[0]
    tm = TM_ROUTE
    fixed = lambda i: (0, 0)
    row = lambda i: (i, 0)
    return pl.pallas_call(
        _combine_kernel,
        grid=(T // tm,),
        in_specs=[pl.BlockSpec(memory_space=pl.ANY),
                  pl.BlockSpec(memory_space=pl.ANY),
                  pl.BlockSpec((tm, LANE), row),
                  pl.BlockSpec((tm, D_MODEL), row),
                  pl.BlockSpec((1, 6, D_MODEL), lambda i: (i // tiles_per_seq, 0, 0)),
                  pl.BlockSpec((D_MODEL, SHARED_FF), fixed),
                  pl.BlockSpec((D_MODEL, SHARED_FF), fixed),
                  pl.BlockSpec((SHARED_FF, D_MODEL), fixed),
                  pl.BlockSpec((1, D_MODEL), fixed),
                  pl.BlockSpec((1, D_MODEL), fixed)],
        out_specs=pl.BlockSpec((tm, D_MODEL), row),
        out_shape=jax.ShapeDtypeStruct((T, D_MODEL), F32),
        scratch_shapes=[pltpu.SMEM((TOP_K * tm,), I32),
                        pltpu.SMEM((TOP_K * tm,), I32),
                        pltpu.VMEM((2, TOP_K, tm * ROW_SUB, LANE), I32),
                        pltpu.SemaphoreType.DMA((2,)),
                        pltpu.SemaphoreType.DMA((2,))],
        compiler_params=pltpu.CompilerParams(dimension_semantics=("arbitrary",)),
        name="combine",
    )(dest, ys, wcol, xmid, mod3, wsg, wsu, wsd, g, b)


def _head_cols(w, n_heads):
    d = w.shape[0]
    w = w.reshape(d, n_heads, HALF, 2)
    z = jnp.zeros((d, n_heads, HALF), w.dtype)
    return jnp.concatenate([w[..., 0], z, w[..., 1], z], axis=-1).reshape(d, n_heads * LANE)


def _head_vec(g):
    g = g.reshape(HALF, 2)
    z = jnp.zeros((HALF,), g.dtype)
    return jnp.concatenate([g[:, 0], z, g[:, 1], z]).reshape(1, LANE)


def _rope_tables(seq):
    inv_freq = ROPE_THETA ** (-jnp.arange(0, HALF, 2, dtype=F32) / HALF)
    pos = jnp.arange(seq)
    rowp = (pos // GRID_W).astype(F32)
    colp = (pos % GRID_W).astype(F32)
    ang = jnp.concatenate([rowp[:, None] * inv_freq, colp[:, None] * inv_freq], axis=-1)
    cos, sin = jnp.cos(ang), jnp.sin(ang)
    z = jnp.zeros_like(cos)
    return jnp.concatenate([cos, z, cos, z], axis=-1), jnp.concatenate([-sin, z, sin, z], axis=-1)


def kernel(x, c, ctx, c_ctx, w_mod, b_mod, w_in, q_norm, k_norm, pool_w, pool_scale, w_out, ln1_g, ln1_b,
           w_router, router_bias, w_gate, w_up, w_down, ws_gate, ws_up, ws_down, ln2_g, ln2_b):
    B, S, D = x.shape
    C = ctx.shape[1]
    T = B * S
    assert D == D_MODEL and w_mod.shape[0] == DEPTH and B + 1 <= MOD_ROWS
    assert S % TM_INPROJ == 0 and S % TQ_ATTN == 0 and S % TM_MIX == 0 and S % TM_ROUTE == 0
    assert S % GRID_W == 0 and C % SUBLANE == 0

    cc = jnp.concatenate([c, c_ctx[None, :], jnp.zeros((MOD_ROWS - B - 1, D), F32)], axis=0)
    mod3 = _mod_call(cc, w_mod[0], b_mod[0][None, :]).reshape(MOD_ROWS, 6, D)

    w = w_in[0]
    o1, o2, o3 = POOL_WIDTH, POOL_WIDTH + ATTN_WIDTH, POOL_WIDTH + ATTN_WIDTH + KV_WIDTH
    wq = _head_cols(w[:, o1:o2], N_HEADS)
    wk = _head_cols(w[:, o2:o3], N_KV_HEADS)
    w_all = jnp.concatenate([w[:, :o1], wq, wk, w[:, o3:]], axis=1).astype(BF16)
    w_kv = jnp.concatenate([wk, w[:, o3:]], axis=1).astype(BF16)
    qg, kg = _head_vec(q_norm[0]), _head_vec(k_norm[0])
    cos_t, sin_t = _rope_tables(S)
    qscale = HEAD_DIM ** -0.5
    u, q, k_l, v_l = _inproj_call(x, mod3, w_all, qg, kg, cos_t * qscale, sin_t * qscale, cos_t, sin_t)
    k_c, v_c = _ctx_kv_call(ctx, mod3, w_kv, kg, B)

    k_all = jnp.concatenate([k_c, k_l], axis=1)
    v_all = jnp.concatenate([v_c, v_l], axis=1).reshape(B, C + S, N_KV_HEADS, HEAD_DIM)
    v_all = jnp.pad(v_all, ((0, 0), (0, 0), (0, 0), (0, LANE - HEAD_DIM))).reshape(B, C + S, K_COLS)
    attn = _attn_call(q, k_all, v_all)

    bd = jax.scipy.linalg.block_diag(*[pool_w[0, g] for g in range(len(POOL_WINDOWS))]).astype(BF16)
    ypool = _pool_call(u, bd, pool_scale[0][None, :])

    wo = w_out[0]
    wop = wo[:POOL_WIDTH].astype(BF16)
    woa = wo[POOL_WIDTH:].reshape(N_KV_HEADS, GQA_GROUP * HEAD_DIM, D)
    woa = jnp.pad(woa, ((0, 0), (0, 2 * LANE - GQA_GROUP * HEAD_DIM), (0, 0))).reshape(ATTN_PAD, D).astype(BF16)
    tri = (jnp.arange(TM_MIX)[:, None] < jnp.arange(TM_MIX)[None, :]).astype(BF16)
    xmid, hp, idx_t, wcol, rank_t, counts = _mix_call(
        x.reshape(T, D), ypool.reshape(T, POOL_WIDTH), attn.reshape(T, ATTN_PAD), wop, woa, mod3,
        ln1_g[0][None, :], ln1_b[0][None, :], w_router[0].T, router_bias[0][:, None], tri, S // TM_MIX)

    bm = BM_MOE
    counts = counts[:, 0]
    padded = ((counts + bm - 1) // bm) * bm
    pad_end = jnp.cumsum(padded)
    pad_start = pad_end - padded
    nb = T * TOP_K // bm + N_EXPERTS
    n_real = (pad_end[-1] // bm).astype(I32).reshape(1)
    blk_row = jnp.arange(nb, dtype=I32) * bm
    block_expert = jnp.minimum(jnp.sum((pad_end[None, :] <= blk_row[:, None]).astype(I32), axis=1), N_EXPERTS - 1)

    dest = _dest_call(pad_start.astype(I32), idx_t, rank_t).reshape(-1)
    xs = _dispatch_call((pad_start + counts).astype(I32), (padded - counts).astype(I32), n_real, dest, hp, nb)
    ys = _moe_call(block_expert.astype(I32), n_real, xs, w_gate[0], w_up[0], w_down[0])
    out = _combine_call(dest, ys, wcol, xmid, mod3, ws_gate[0].astype(BF16), ws_up[0].astype(BF16),
                        ws_down[0].astype(BF16), ln2_g[0][None, :], ln2_b[0][None, :], S // TM_ROUTE)
    return out.reshape(B, S, D)
```

```python
import functools

import jax
import jax.numpy as jnp
from jax import lax
from jax.experimental import pallas as pl
from jax.experimental.pallas import tpu as pltpu

F32 = jnp.float32
BF16 = jnp.bfloat16
I32 = jnp.int32

LANE = 128
SUBLANE = 8

D_MODEL = 1024
GRID_W = 64
POOL_WIDTH = 256
POOL_WINDOWS = (2, 4, 8, 16)
POOL_GROUP = 64
HEAD_DIM = 64
HALF = HEAD_DIM // 2
N_HEADS = 12
N_KV_HEADS = 4
GQA_GROUP = N_HEADS // N_KV_HEADS
ATTN_WIDTH = N_HEADS * HEAD_DIM
KV_WIDTH = N_KV_HEADS * HEAD_DIM
ROPE_THETA = 10000.0
N_EXPERTS = 64
TOP_K = 8
N_GROUPS = 8
GROUP_SIZE = N_EXPERTS // N_GROUPS
TOPK_GROUPS = 4
EXPERT_FF = 256
SHARED_FF = 256
ROUTED_SCALE = 2.5
DEPTH = 1
ALPHA = (2.0 * DEPTH) ** 0.25
LN_EPS = 1e-5
RMS_EPS = 1e-6

MOD_ROWS = 24
HALF_D = D_MODEL // 2
ROW_SUB = HALF_D // LANE
Q_COLS = N_HEADS * LANE
K_COLS = N_KV_HEADS * LANE
ATTN_PAD = N_KV_HEADS * 2 * LANE

TM_INPROJ = 512
TQ_ATTN = 256
TM_MIX = 256
BM_MOE = 512
TM_ROUTE = 256

_NT = (((1,), (1,)), ((), ()))


def _dot(a, b):
    return jnp.dot(a, b, preferred_element_type=F32)


def _pack2(lo, hi):
    lo_bits = lax.bitcast_convert_type(lo.astype(BF16).astype(F32), I32)
    hi_bits = lax.bitcast_convert_type(hi.astype(BF16).astype(F32), I32)
    return lax.shift_right_logical(lo_bits, 16) | (hi_bits & jnp.int32(-65536))


def _unpack2(w):
    lo = lax.bitcast_convert_type(lax.shift_left(w, 16), F32)
    hi = lax.bitcast_convert_type(w & jnp.int32(-65536), F32)
    return lo, hi


def _store_rows(ref, packed):
    n = packed.shape[0]
    for s in range(ROW_SUB):
        ref[pl.ds(s, n, stride=ROW_SUB), :] = packed[:, s * LANE:(s + 1) * LANE]


def _load_rows(ref, n):
    return [ref[pl.ds(s, n, stride=ROW_SUB), :] for s in range(ROW_SUB)]


def _silu(x):
    return x * jax.nn.sigmoid(x)


def _layer_norm(r, g, b):
    mu = jnp.mean(r, axis=-1, keepdims=True)
    d = r - mu
    var = jnp.mean(d * d, axis=-1, keepdims=True)
    return d * lax.rsqrt(var + LN_EPS) * g + b


def _mod_kernel(c_ref, w_ref, b_ref, o_ref):
    a = _silu(c_ref[...])
    o_ref[...] = jnp.dot(a, w_ref[...], precision=lax.Precision.HIGHEST,
                         preferred_element_type=F32) + b_ref[...]


def _mod_call(cc, w_mod, b_mod):
    n = w_mod.shape[1]
    tn = 512
    return pl.pallas_call(
        _mod_kernel,
        grid=(n // tn,),
        in_specs=[pl.BlockSpec((MOD_ROWS, D_MODEL), lambda j: (0, 0)),
                  pl.BlockSpec((D_MODEL, tn), lambda j: (0, j)),
                  pl.BlockSpec((1, tn), lambda j: (0, j))],
        out_specs=pl.BlockSpec((MOD_ROWS, tn), lambda j: (0, j)),
        out_shape=jax.ShapeDtypeStruct((MOD_ROWS, n), F32),
        name="mod",
    )(cc, w_mod, b_mod)


def _norm_head(seg, g):
    ms = jnp.sum(seg * seg, axis=-1, keepdims=True) * (1.0 / HEAD_DIM)
    return seg * lax.rsqrt(ms + RMS_EPS) * g


def _rope(xn, c, s):
    return xn * c + pltpu.roll(xn, LANE // 2, axis=1) * s


def _inproj_kernel(x_ref, mod_ref, w_ref, qg_ref, kg_ref, cq_ref, sq_ref, ck_ref, sk_ref,
                   u_ref, q_ref, k_ref, v_ref):
    sh = mod_ref[0, 0:1, :]
    sc = mod_ref[0, 1:2, :]
    xm = (x_ref[0] * (1.0 + sc) + sh).astype(BF16)
    u_ref[0] = _dot(xm, w_ref[:, 0:POOL_WIDTH])
    qg, kg = qg_ref[...], kg_ref[...]
    cq, sq, ck, sk = cq_ref[...], sq_ref[...], ck_ref[...], sk_ref[...]
    base = POOL_WIDTH
    for pair in range(N_HEADS // 2):
        p = _dot(xm, w_ref[:, base + pair * 2 * LANE: base + (pair + 1) * 2 * LANE])
        for j in range(2):
            h = pair * 2 + j
            q = _rope(_norm_head(p[:, j * LANE:(j + 1) * LANE], qg), cq, sq)
            q_ref[0, :, h * LANE:(h + 1) * LANE] = q.astype(BF16)
    base += Q_COLS
    for pair in range(N_KV_HEADS // 2):
        p = _dot(xm, w_ref[:, base + pair * 2 * LANE: base + (pair + 1) * 2 * LANE])
        for j in range(2):
            h = pair * 2 + j
            k = _rope(_norm_head(p[:, j * LANE:(j + 1) * LANE], kg), ck, sk)
            k_ref[0, :, h * LANE:(h + 1) * LANE] = k.astype(BF16)
    base += K_COLS
    v_ref[0] = _dot(xm, w_ref[:, base:base + KV_WIDTH]).astype(BF16)


def _inproj_call(x, mod3, w_all, qg, kg, cq, sq, ck, sk):
    B, S, _ = x.shape
    tm = TM_INPROJ
    ncol = w_all.shape[1]
    tab = pl.BlockSpec((tm, LANE), lambda b, i: (i, 0))
    vec = pl.BlockSpec((1, LANE), lambda b, i: (0, 0))
    return pl.pallas_call(
        _inproj_kernel,
        grid=(B, S // tm),
        in_specs=[pl.BlockSpec((1, tm, D_MODEL), lambda b, i: (b, i, 0)),
                  pl.BlockSpec((1, 6, D_MODEL), lambda b, i: (b, 0, 0)),
                  pl.BlockSpec((D_MODEL, ncol), lambda b, i: (0, 0)),
                  vec, vec, tab, tab, tab, tab],
        out_specs=[pl.BlockSpec((1, tm, POOL_WIDTH), lambda b, i: (b, i, 0)),
                   pl.BlockSpec((1, tm, Q_COLS), lambda b, i: (b, i, 0)),
                   pl.BlockSpec((1, tm, K_COLS), lambda b, i: (b, i, 0)),
                   pl.BlockSpec((1, tm, KV_WIDTH), lambda b, i: (b, i, 0))],
        out_shape=[jax.ShapeDtypeStruct((B, S, POOL_WIDTH), F32),
                   jax.ShapeDtypeStruct((B, S, Q_COLS), BF16),
                   jax.ShapeDtypeStruct((B, S, K_COLS), BF16),
                   jax.ShapeDtypeStruct((B, S, KV_WIDTH), BF16)],
        name="inproj",
    )(x, mod3, w_all, qg, kg, cq, sq, ck, sk)


def _ctx_kv_kernel(x_ref, mod_ref, w_ref, kg_ref, k_ref, v_ref):
    sh = mod_ref[0, 0:1, :]
    sc = mod_ref[0, 1:2, :]
    xm = (x_ref[0] * (1.0 + sc) + sh).astype(BF16)
    kg = kg_ref[...]
    for pair in range(N_KV_HEADS // 2):
        p = _dot(xm, w_ref[:, pair * 2 * LANE:(pair + 1) * 2 * LANE])
        for j in range(2):
            h = pair * 2 + j
            k_ref[0, :, h * LANE:(h + 1) * LANE] = _norm_head(p[:, j * LANE:(j + 1) * LANE], kg).astype(BF16)
    v_ref[0] = _dot(xm, w_ref[:, K_COLS:K_COLS + KV_WIDTH]).astype(BF16)


def _ctx_kv_call(ctx, mod3, w_kv, kg, ctx_row):
    B, C, _ = ctx.shape
    return pl.pallas_call(
        _ctx_kv_kernel,
        grid=(B,),
        in_specs=[pl.BlockSpec((1, C, D_MODEL), lambda b: (b, 0, 0)),
                  pl.BlockSpec((1, 6, D_MODEL), lambda b: (ctx_row, 0, 0)),
                  pl.BlockSpec((D_MODEL, K_COLS + KV_WIDTH), lambda b: (0, 0)),
                  pl.BlockSpec((1, LANE), lambda b: (0, 0))],
        out_specs=[pl.BlockSpec((1, C, K_COLS), lambda b: (b, 0, 0)),
                   pl.BlockSpec((1, C, KV_WIDTH), lambda b: (b, 0, 0))],
        out_shape=[jax.ShapeDtypeStruct((B, C, K_COLS), BF16),
                   jax.ShapeDtypeStruct((B, C, KV_WIDTH), BF16)],
        name="ctx_kv",
    )(ctx, mod3, w_kv, kg)


POOL_PAD = 8


def _pool_kernel(u_ref, bd_ref, ps_ref, y_ref):
    S = u_ref.shape[1]
    n = S + 2 * POOL_PAD
    t = lax.broadcasted_iota(I32, (S, LANE), 0)
    lane = lax.broadcasted_iota(I32, (S, LANE), 1)
    zpad = jnp.zeros((POOL_PAD, LANE), F32)
    for half in range(POOL_WIDTH // LANE):
        u = u_ref[0, :, half * LANE:(half + 1) * LANE]
        ue = jnp.concatenate([zpad, u, zpad], axis=0)
        fwd = {1: ue}
        w = 1
        while w < POOL_WINDOWS[2 * half + 1]:
            fwd[2 * w] = fwd[w] + pltpu.roll(fwd[w], n - w, axis=0)
            w *= 2
        ds = []
        for win in POOL_WINDOWS[2 * half: 2 * half + 2]:
            hw = win // 2
            centred = pltpu.roll(fwd[win], hw, axis=0)[POOL_PAD:POOL_PAD + S]
            cnt = (jnp.minimum(t + hw, S) - jnp.maximum(t - hw, 0)).astype(F32)
            ds.append(centred / cnt - u)
        d = jnp.where(lane < POOL_GROUP, ds[0], ds[1]).astype(BF16)
        sl = slice(half * LANE, (half + 1) * LANE)
        y = _dot(d, bd_ref[sl, sl]) * ps_ref[:, sl]
        y_ref[0, :, sl] = y.astype(BF16)


def _pool_call(u, bd, ps):
    B, S, _ = u.shape
    return pl.pallas_call(
        _pool_kernel,
        grid=(B,),
        in_specs=[pl.BlockSpec((1, S, POOL_WIDTH), lambda b: (b, 0, 0)),
                  pl.BlockSpec((POOL_WIDTH, POOL_WIDTH), lambda b: (0, 0)),
                  pl.BlockSpec((1, POOL_WIDTH), lambda b: (0, 0))],
        out_specs=pl.BlockSpec((1, S, POOL_WIDTH), lambda b: (b, 0, 0)),
        out_shape=jax.ShapeDtypeStruct((B, S, POOL_WIDTH), BF16),
        name="pool",
    )(u, bd, ps)


ROW_REDUCE_WAYS = 16


def _reduce_rows(op, x):
    rows, lanes = x.shape
    if rows % (ROW_REDUCE_WAYS * SUBLANE) == 0:
        x = op(x.reshape(ROW_REDUCE_WAYS, rows // ROW_REDUCE_WAYS, lanes), axis=0)
    return op(x, axis=0, keepdims=True)


def _attn_kernel(q_ref, k_ref, vt_ref, o_ref):
    k = k_ref[0]
    vt = vt_ref[0, 0]
    tq = q_ref.shape[1]
    ss = [lax.dot_general(k, q_ref[0, :, j * LANE:(j + 1) * LANE], _NT, preferred_element_type=F32)
          for j in range(GQA_GROUP)]
    outs = []
    for s in ss:
        m = _reduce_rows(jnp.max, s)
        p = jnp.exp(s - m)
        l = _reduce_rows(jnp.sum, p)
        outs.append(_dot(vt, p.astype(BF16)) / l)
    outs.append(jnp.zeros((2 * LANE - GQA_GROUP * HEAD_DIM, tq), F32))
    o_ref[0] = jnp.concatenate(outs, axis=0).T.astype(BF16)


def _attn_call(q, k_all, vt_all):
    B, S, _ = q.shape
    Lk = k_all.shape[1]
    tq = TQ_ATTN
    return pl.pallas_call(
        _attn_kernel,
        grid=(B, N_KV_HEADS, S // tq),
        in_specs=[pl.BlockSpec((1, tq, GQA_GROUP * LANE), lambda b, h, i: (b, i, h)),
                  pl.BlockSpec((1, Lk, LANE), lambda b, h, i: (b, 0, h)),
                  pl.BlockSpec((1, 1, HEAD_DIM, Lk), lambda b, h, i: (b, h, 0, 0))],
        out_specs=pl.BlockSpec((1, tq, 2 * LANE), lambda b, h, i: (b, i, h)),
        out_shape=jax.ShapeDtypeStruct((B, S, ATTN_PAD), BF16),
        name="attn",
    )(q, k_all, vt_all)


def _mix_kernel(x_ref, yp_ref, at_ref, wop_ref, woa_ref, mod_ref, g_ref, b_ref, wr_ref, rb_ref, tri_ref,
                xmid_ref, hp_ref, idx_ref, wts_ref, rank_ref, cnt_ref, carry):
    i = pl.program_id(0)
    tm = x_ref.shape[0]

    @pl.when(i == 0)
    def _():
        carry[...] = jnp.zeros_like(carry)

    mix = _dot(yp_ref[...], wop_ref[...]) + _dot(at_ref[...], woa_ref[...])
    g1 = mod_ref[0, 2:3, :]
    sh2 = mod_ref[0, 3:4, :]
    sc2 = mod_ref[0, 4:5, :]
    xmid = _layer_norm(ALPHA * x_ref[...] + g1 * mix, g_ref[...], b_ref[...])
    xmid_ref[...] = xmid
    h = xmid * (1.0 + sc2) + sh2
    _store_rows(hp_ref, _pack2(h[:, :HALF_D], h[:, HALF_D:]))

    logits = lax.dot_general(wr_ref[...], h, _NT, precision=lax.Precision.HIGHEST,
                             preferred_element_type=F32)
    score = jax.nn.sigmoid(logits)
    sel = score + rb_ref[...]
    neg = jnp.float32(-jnp.inf)
    sub = lax.broadcasted_iota(I32, (GROUP_SIZE, tm), 0)
    grp_sel, grp_score = [], []
    for g in range(N_GROUPS):
        sg = sel[g * GROUP_SIZE:(g + 1) * GROUP_SIZE, :]
        m1 = jnp.max(sg, axis=0, keepdims=True)
        i1 = jnp.min(jnp.where(sg == m1, sub, GROUP_SIZE), axis=0, keepdims=True)
        m2 = jnp.max(jnp.where(sub == i1, neg, sg), axis=0, keepdims=True)
        grp_sel.append(sg)
        grp_score.append(m1 + m2)
    masked = []
    for g in range(N_GROUPS):
        ahead = jnp.zeros((1, tm), I32)
        for o in range(N_GROUPS):
            if o == g:
                continue
            beats = (grp_score[o] > grp_score[g]) | ((grp_score[o] == grp_score[g]) & (o < g))
            ahead = ahead + beats.astype(I32)
        masked.append(jnp.where(ahead < TOPK_GROUPS, grp_sel[g], neg))
    cur = jnp.concatenate(masked, axis=0)
    row = lax.broadcasted_iota(I32, (N_EXPERTS, tm), 0)
    member = jnp.zeros((N_EXPERTS, tm), F32)
    picks, wsel = [], []
    for k in range(TOP_K):
        mx = jnp.max(cur, axis=0, keepdims=True)
        ei = jnp.min(jnp.where(cur == mx, row, N_EXPERTS), axis=0, keepdims=True)
        hit = row == ei
        picks.append(ei)
        wsel.append(jnp.sum(jnp.where(hit, score, 0.0), axis=0, keepdims=True))
        cur = jnp.where(hit, neg, cur)
        member = jnp.where(hit, 1.0, member)
    tot = wsel[0]
    for k in range(1, TOP_K):
        tot = tot + wsel[k]
    before = _dot(member.astype(BF16), tri_ref[...]) + carry[:, 0:1]
    for k in range(TOP_K):
        idx_ref[k:k + 1, :] = picks[k]
        rk = jnp.sum(jnp.where(row == picks[k], before, 0.0), axis=0, keepdims=True)
        rank_ref[k:k + 1, :] = rk.astype(I32)
    wrows = [wsel[k] / tot * ROUTED_SCALE for k in range(TOP_K)]
    wmat = jnp.concatenate(wrows + [jnp.zeros((LANE - TOP_K, tm), F32)], axis=0)
    wts_ref[...] = wmat.T
    carry[...] = carry[...] + jnp.sum(member, axis=1, keepdims=True)
    cnt_ref[...] = carry[...].astype(I32)


def _mix_call(x2, ypool, attn, wop, woa, mod3, g, b, wr_t, rb, tri, tiles_per_seq):
    T = x2.shape[0]
    tm = TM_MIX
    row = lambda i: (i, 0)
    fixed = lambda i: (0, 0)
    tok = lambda i: (0, i)
    return pl.pallas_call(
        _mix_kernel,
        grid=(T // tm,),
        in_specs=[pl.BlockSpec((tm, D_MODEL), row),
                  pl.BlockSpec((tm, POOL_WIDTH), row),
                  pl.BlockSpec((tm, ATTN_PAD), row),
                  pl.BlockSpec((POOL_WIDTH, D_MODEL), fixed),
                  pl.BlockSpec((ATTN_PAD, D_MODEL), fixed),
                  pl.BlockSpec((1, 6, D_MODEL), lambda i: (i // tiles_per_seq, 0, 0)),
                  pl.BlockSpec((1, D_MODEL), fixed),
                  pl.BlockSpec((1, D_MODEL), fixed),
                  pl.BlockSpec((N_EXPERTS, D_MODEL), fixed),
                  pl.BlockSpec((N_EXPERTS, 1), fixed),
                  pl.BlockSpec((tm, tm), fixed)],
        out_specs=[pl.BlockSpec((tm, D_MODEL), row),
                   pl.BlockSpec((tm * ROW_SUB, LANE), row),
                   pl.BlockSpec((TOP_K, tm), tok),
                   pl.BlockSpec((tm, LANE), row),
                   pl.BlockSpec((TOP_K, tm), tok),
                   pl.BlockSpec((N_EXPERTS, LANE), fixed)],
        out_shape=[jax.ShapeDtypeStruct((T, D_MODEL), F32),
                   jax.ShapeDtypeStruct((T * ROW_SUB, LANE), I32),
                   jax.ShapeDtypeStruct((TOP_K, T), I32),
                   jax.ShapeDtypeStruct((T, LANE), F32),
                   jax.ShapeDtypeStruct((TOP_K, T), I32),
                   jax.ShapeDtypeStruct((N_EXPERTS, LANE), I32)],
        scratch_shapes=[pltpu.VMEM((N_EXPERTS, LANE), F32)],
        compiler_params=pltpu.CompilerParams(dimension_semantics=("arbitrary",)),
        name="mix",
    )(x2, ypool, attn, wop, woa, mod3, g, b, wr_t, rb, tri)


def _slab(r):
    return pl.ds(pl.multiple_of(r * ROW_SUB, ROW_SUB), ROW_SUB)


def _dest_kernel(ps_ref, idx_ref, rank_ref, o_ref):
    idx = idx_ref[...]
    d = rank_ref[...]
    for e in range(N_EXPERTS):
        d = d + jnp.where(idx == e, ps_ref[e], 0)
    tm = o_ref.shape[1]
    for j in range(idx.shape[1] // tm):
        o_ref[j * TOP_K:(j + 1) * TOP_K, :] = d[:, j * tm:(j + 1) * tm]


def _dest_call(pad_start, idx_t, rank_t):
    T = idx_t.shape[1]
    tm = TM_ROUTE
    td = 2048 if T % 2048 == 0 else tm
    per = td // tm
    grid_spec = pltpu.PrefetchScalarGridSpec(
        num_scalar_prefetch=1,
        grid=(T // td,),
        in_specs=[pl.BlockSpec((TOP_K, td), lambda i, ps: (0, i)),
                  pl.BlockSpec((TOP_K, td), lambda i, ps: (0, i))],
        out_specs=pl.BlockSpec((per * TOP_K, tm), lambda i, ps: (i, 0)),
    )
    return pl.pallas_call(
        _dest_kernel,
        grid_spec=grid_spec,
        out_shape=jax.ShapeDtypeStruct((T // tm * TOP_K, tm), I32),
        name="dest",
    )(pad_start, idx_t, rank_t)


def _dispatch_kernel(pst_ref, plen_ref, nreal_ref, dest_hbm, hp_ref, xs_hbm, dst0, dst1, zbuf, isem, ssem, zsem):
    tm = hp_ref.shape[0] // ROW_SUB
    bm = zbuf.shape[0] // ROW_SUB
    n_idx = TOP_K * tm
    nb = xs_hbm.shape[0] // (bm * ROW_SUB)
    i = pl.program_id(0)
    nt = pl.num_programs(0)
    dst = (dst0, dst1)

    def idx_copy(tile, s):
        off = pl.multiple_of(tile * n_idx, n_idx)
        return pltpu.make_async_copy(dest_hbm.at[pl.ds(off, n_idx)], dst[s], isem.at[s])

    def pad_fill(e):
        n = plen_ref[e] * ROW_SUB
        start = pl.multiple_of(pst_ref[e] * ROW_SUB, ROW_SUB)
        return pltpu.make_async_copy(zbuf.at[pl.ds(0, n)], xs_hbm.at[pl.ds(start, n)], zsem)

    def dead_fill(j):
        return pltpu.make_async_copy(zbuf, xs_hbm.at[pl.ds(j * bm * ROW_SUB, bm * ROW_SUB)], zsem)

    def fills(act):
        for e in range(N_EXPERTS):
            pl.when(plen_ref[e] > 0)(functools.partial(lambda e: act(pad_fill(e)), e))
        for j in range(nb - N_EXPERTS, nb):
            pl.when(j >= nreal_ref[0])(functools.partial(lambda j: act(dead_fill(j)), j))

    @pl.when(i == 0)
    def _():
        zbuf[...] = jnp.zeros_like(zbuf)
        idx_copy(0, 0).start()
        idx_copy(0, 0).wait()
        fills(lambda cp: cp.start())

    def scatter(s):
        @pl.when(i + 1 < nt)
        def _():
            idx_copy(i + 1, 1 - s).start()

        def body(rb, c):
            for rr in range(SUBLANE):
                r = rb * SUBLANE + rr
                for k in range(TOP_K):
                    pltpu.make_async_copy(hp_ref.at[_slab(r)], xs_hbm.at[_slab(dst[s][k * tm + r])],
                                          ssem).start(priority=k % 2)
            return c
        lax.fori_loop(0, tm // SUBLANE, body, 0)
        for k in range(TOP_K):
            pltpu.make_async_copy(hp_ref, xs_hbm.at[pl.ds(0, tm * ROW_SUB)], ssem).wait()

        @pl.when(i + 1 < nt)
        def _():
            idx_copy(i + 1, 1 - s).wait()

    for s in range(2):
        pl.when(i % 2 == s)(functools.partial(scatter, s))

    @pl.when(i == 0)
    def _():
        fills(lambda cp: cp.wait())


def _dispatch_call(pad_row_start, pad_len, n_real, dest, hp, nb):
    tm = TM_ROUTE
    bm = BM_MOE
    T = hp.shape[0] // ROW_SUB
    grid_spec = pltpu.PrefetchScalarGridSpec(
        num_scalar_prefetch=3,
        grid=(T // tm,),
        in_specs=[pl.BlockSpec(memory_space=pl.ANY),
                  pl.BlockSpec((tm * ROW_SUB, LANE), lambda i, a, b, c: (i, 0))],
        out_specs=pl.BlockSpec(memory_space=pl.ANY),
        scratch_shapes=[pltpu.SMEM((TOP_K * tm,), I32),
                        pltpu.SMEM((TOP_K * tm,), I32),
                        pltpu.VMEM((bm * ROW_SUB, LANE), I32),
                        pltpu.SemaphoreType.DMA((2,)),
                        pltpu.SemaphoreType.DMA,
                        pltpu.SemaphoreType.DMA],
    )
    return pl.pallas_call(
        _dispatch_kernel,
        grid_spec=grid_spec,
        out_shape=jax.ShapeDtypeStruct((nb * bm * ROW_SUB, LANE), I32),
        compiler_params=pltpu.CompilerParams(dimension_semantics=("arbitrary",)),
        name="dispatch",
    )(pad_row_start, pad_len, n_real, dest, hp)


def _moe_kernel(be_ref, nreal_ref, x_ref, wg_ref, wu_ref, wd_ref, y_ref, wgb, wub, wdb):
    bm = x_ref.shape[0] // ROW_SUB
    i = pl.program_id(0)
    n = nreal_ref[0]

    @pl.when(i < n)
    def _():
        e_changed = (i == 0) | (be_ref[i] != be_ref[jnp.maximum(i - 1, 0)])

        @pl.when(e_changed)
        def _():
            wgb[...] = wg_ref[0].astype(BF16)
            wub[...] = wu_ref[0].astype(BF16)
            wdb[...] = wd_ref[0].astype(BF16)

        halves = [_unpack2(w) for w in _load_rows(x_ref, bm)]
        xb = jnp.concatenate([lo.astype(BF16) for lo, _ in halves] + [hi.astype(BF16) for _, hi in halves], axis=1)
        a = _silu(_dot(xb, wgb[...])) * _dot(xb, wub[...])
        y = _dot(a.astype(BF16), wdb[...])
        _store_rows(y_ref, _pack2(y[:, :HALF_D], y[:, HALF_D:]))

    @pl.when(i >= n)
    def _():
        y_ref[...] = jnp.zeros_like(y_ref)


def _moe_call(block_expert, n_real, xs, w_gate, w_up, w_down):
    bm = BM_MOE
    nb = block_expert.shape[0]
    wmap = lambda i, be, nr: (be[i], 0, 0)
    rows = pl.BlockSpec((bm * ROW_SUB, LANE), lambda i, be, nr: (i, 0))
    grid_spec = pltpu.PrefetchScalarGridSpec(
        num_scalar_prefetch=2,
        grid=(nb,),
        in_specs=[rows,
                  pl.BlockSpec((1, D_MODEL, EXPERT_FF), wmap),
                  pl.BlockSpec((1, D_MODEL, EXPERT_FF), wmap),
                  pl.BlockSpec((1, EXPERT_FF, D_MODEL), wmap)],
        out_specs=rows,
        scratch_shapes=[pltpu.VMEM((D_MODEL, EXPERT_FF), BF16),
                        pltpu.VMEM((D_MODEL, EXPERT_FF), BF16),
                        pltpu.VMEM((EXPERT_FF, D_MODEL), BF16)],
    )
    return pl.pallas_call(
        _moe_kernel,
        grid_spec=grid_spec,
        out_shape=jax.ShapeDtypeStruct(xs.shape, I32),
        compiler_params=pltpu.CompilerParams(dimension_semantics=("arbitrary",)),
        name="moe",
    )(block_expert, n_real, xs, w_gate, w_up, w_down)


def _combine_kernel(dest_hbm, ys_hbm, wc_ref, xmid_ref, mod_ref, wsg_ref, wsu_ref, wsd_ref, g_ref, b_ref,
                    o_ref, dst0, dst1, rbuf, isem, gsem):
    tm = xmid_ref.shape[0]
    n_idx = TOP_K * tm
    i = pl.program_id(0)
    nt = pl.num_programs(0)
    dst = (dst0, dst1)

    def idx_copy(tile, s):
        off = pl.multiple_of(tile * n_idx, n_idx)
        return pltpu.make_async_copy(dest_hbm.at[pl.ds(off, n_idx)], dst[s], isem.at[s])

    def start_gather(s):
        def body(rb, c):
            for rr in range(SUBLANE):
                r = rb * SUBLANE + rr
                for k in range(TOP_K):
                    pltpu.make_async_copy(ys_hbm.at[_slab(dst[s][k * tm + r])], rbuf.at[s, k, _slab(r)],
                                          gsem.at[s]).start(priority=k % 2)
            return c
        lax.fori_loop(0, tm // SUBLANE, body, 0)

    def wait_gather(s):
        for k in range(TOP_K):
            pltpu.make_async_copy(ys_hbm.at[pl.ds(0, tm * ROW_SUB)], rbuf.at[s, k], gsem.at[s]).wait()

    @pl.when(i == 0)
    def _():
        idx_copy(0, 0).start()
        idx_copy(0, 0).wait()
        start_gather(0)

        @pl.when(nt > 1)
        def _():
            idx_copy(1, 1).start()

    def prefetch(s):
        @pl.when(i + 2 < nt)
        def _():
            idx_copy(i + 2, s).start()

        @pl.when(i + 1 < nt)
        def _():
            idx_copy(i + 1, 1 - s).wait()
            start_gather(1 - s)

        wait_gather(s)

    for s in range(2):
        pl.when(i % 2 == s)(functools.partial(prefetch, s))

    wc = wc_ref[...]
    slot = i % 2
    acc = None
    for k in range(TOP_K):
        parts = [_unpack2(w) for w in _load_rows(rbuf.at[slot, k], tm)]
        parts = [lo for lo, _ in parts] + [hi for _, hi in parts]
        wk = wc[:, k:k + 1]
        acc = [wk * p for p in parts] if acc is None else [a + wk * p for a, p in zip(acc, parts)]
    routed = jnp.concatenate(acc, axis=1)
    xmid = xmid_ref[...]
    sh2 = mod_ref[0, 3:4, :]
    sc2 = mod_ref[0, 4:5, :]
    g2 = mod_ref[0, 5:6, :]
    hb = (xmid * (1.0 + sc2) + sh2).astype(BF16)
    a = _silu(_dot(hb, wsg_ref[...])) * _dot(hb, wsu_ref[...])
    shared = _dot(a.astype(BF16), wsd_ref[...])
    o_ref[...] = _layer_norm(ALPHA * xmid + g2 * (routed + shared), g_ref[...], b_ref[...])


def _combine_call(dest, ys, wcol, xmid, mod3, wsg, wsu, wsd, g, b, tiles_per_seq):
    T = xmid.shape[0]
    tm = TM_ROUTE
    fixed = lambda i: (0, 0)
    row = lambda i: (i, 0)
    return pl.pallas_call(
        _combine_kernel,
        grid=(T // tm,),
        in_specs=[pl.BlockSpec(memory_space=pl.ANY),
                  pl.BlockSpec(memory_space=pl.ANY),
                  pl.BlockSpec((tm, LANE), row),
                  pl.BlockSpec((tm, D_MODEL), row),
                  pl.BlockSpec((1, 6, D_MODEL), lambda i: (i // tiles_per_seq, 0, 0)),
                  pl.BlockSpec((D_MODEL, SHARED_FF), fixed),
                  pl.BlockSpec((D_MODEL, SHARED_FF), fixed),
                  pl.BlockSpec((SHARED_FF, D_MODEL), fixed),
                  pl.BlockSpec((1, D_MODEL), fixed),
                  pl.BlockSpec((1, D_MODEL), fixed)],
        out_specs=pl.BlockSpec((tm, D_MODEL), row),
        out_shape=jax.ShapeDtypeStruct((T, D_MODEL), F32),
        scratch_shapes=[pltpu.SMEM((TOP_K * tm,), I32),
                        pltpu.SMEM((TOP_K * tm,), I32),
                        pltpu.VMEM((2, TOP_K, tm * ROW_SUB, LANE), I32),
                        pltpu.SemaphoreType.DMA((2,)),
                        pltpu.SemaphoreType.DMA((2,))],
        compiler_params=pltpu.CompilerParams(dimension_semantics=("arbitrary",)),
        name="combine",
    )(dest, ys, wcol, xmid, mod3, wsg, wsu, wsd, g, b)


def _head_cols(w, n_heads):
    d = w.shape[0]
    w = w.reshape(d, n_heads, HALF, 2)
    z = jnp.zeros((d, n_heads, HALF), w.dtype)
    return jnp.concatenate([w[..., 0], z, w[..., 1], z], axis=-1).reshape(d, n_heads * LANE)


def _head_vec(g):
    g = g.reshape(HALF, 2)
    z = jnp.zeros((HALF,), g.dtype)
    return jnp.concatenate([g[:, 0], z, g[:, 1], z]).reshape(1, LANE)


def _rope_tables(seq):
    inv_freq = ROPE_THETA ** (-jnp.arange(0, HALF, 2, dtype=F32) / HALF)
    pos = jnp.arange(seq)
    rowp = (pos // GRID_W).astype(F32)
    colp = (pos % GRID_W).astype(F32)
    ang = jnp.concatenate([rowp[:, None] * inv_freq, colp[:, None] * inv_freq], axis=-1)
    cos, sin = jnp.cos(ang), jnp.sin(ang)
    z = jnp.zeros_like(cos)
    return jnp.concatenate([cos, z, cos, z], axis=-1), jnp.concatenate([-sin, z, sin, z], axis=-1)


def kernel(x, c, ctx, c_ctx, w_mod, b_mod, w_in, q_norm, k_norm, pool_w, pool_scale, w_out, ln1_g, ln1_b,
           w_router, router_bias, w_gate, w_up, w_down, ws_gate, ws_up, ws_down, ln2_g, ln2_b):
    B, S, D = x.shape
    C = ctx.shape[1]
    T = B * S
    assert D == D_MODEL and w_mod.shape[0] == DEPTH and B + 1 <= MOD_ROWS
    assert S % TM_INPROJ == 0 and S % TQ_ATTN == 0 and S % TM_MIX == 0 and S % TM_ROUTE == 0
    assert S % GRID_W == 0 and C % SUBLANE == 0

    cc = jnp.concatenate([c, c_ctx[None, :], jnp.zeros((MOD_ROWS - B - 1, D), F32)], axis=0)
    mod3 = _mod_call(cc, w_mod[0], b_mod[0][None, :]).reshape(MOD_ROWS, 6, D)

    w = w_in[0]
    o1, o2, o3 = POOL_WIDTH, POOL_WIDTH + ATTN_WIDTH, POOL_WIDTH + ATTN_WIDTH + KV_WIDTH
    wq = _head_cols(w[:, o1:o2], N_HEADS)
    wk = _head_cols(w[:, o2:o3], N_KV_HEADS)
    w_all = jnp.concatenate([w[:, :o1], wq, wk, w[:, o3:]], axis=1).astype(BF16)
    w_kv = jnp.concatenate([wk, w[:, o3:]], axis=1).astype(BF16)
    qg, kg = _head_vec(q_norm[0]), _head_vec(k_norm[0])
    cos_t, sin_t = _rope_tables(S)
    qscale = HEAD_DIM ** -0.5
    u, q, k_l, v_l = _inproj_call(x, mod3, w_all, qg, kg, cos_t * qscale, sin_t * qscale, cos_t, sin_t)
    k_c, v_c = _ctx_kv_call(ctx, mod3, w_kv, kg, B)

    k_all = jnp.concatenate([k_c, k_l], axis=1)
    v_all = jnp.concatenate([v_c, v_l], axis=1).reshape(B, C + S, N_KV_HEADS, HEAD_DIM)
    attn = _attn_call(q, k_all, jnp.transpose(v_all, (0, 2, 3, 1)))

    bd = jax.scipy.linalg.block_diag(*[pool_w[0, g] for g in range(len(POOL_WINDOWS))]).astype(BF16)
    ypool = _pool_call(u, bd, pool_scale[0][None, :])

    wo = w_out[0]
    wop = wo[:POOL_WIDTH].astype(BF16)
    woa = wo[POOL_WIDTH:].reshape(N_KV_HEADS, GQA_GROUP * HEAD_DIM, D)
    woa = jnp.pad(woa, ((0, 0), (0, 2 * LANE - GQA_GROUP * HEAD_DIM), (0, 0))).reshape(ATTN_PAD, D).astype(BF16)
    tri = (jnp.arange(TM_MIX)[:, None] < jnp.arange(TM_MIX)[None, :]).astype(BF16)
    xmid, hp, idx_t, wcol, rank_t, counts = _mix_call(
        x.reshape(T, D), ypool.reshape(T, POOL_WIDTH), attn.reshape(T, ATTN_PAD), wop, woa, mod3,
        ln1_g[0][None, :], ln1_b[0][None, :], w_router[0].T, router_bias[0][:, None], tri, S // TM_MIX)

    bm = BM_MOE
    counts = counts[:, 0]
    padded = ((counts + bm - 1) // bm) * bm
    pad_end = jnp.cumsum(padded)
    pad_start = pad_end - padded
    nb = T * TOP_K // bm + N_EXPERTS
    n_real = (pad_end[-1] // bm).astype(I32).reshape(1)
    blk_row = jnp.arange(nb, dtype=I32) * bm
    block_expert = jnp.minimum(jnp.sum((pad_end[None, :] <= blk_row[:, None]).astype(I32), axis=1), N_EXPERTS - 1)

    dest = _dest_call(pad_start.astype(I32), idx_t, rank_t).reshape(-1)
    xs = _dispatch_call((pad_start + counts).astype(I32), (padded - counts).astype(I32), n_real, dest, hp, nb)
    ys = _moe_call(block_expert.astype(I32), n_real, xs, w_gate[0], w_up[0], w_down[0])
    out = _combine_call(dest, ys, wcol, xmid, mod3, ws_gate[0].astype(BF16), ws_up[0].astype(BF16),
                        ws_down[0].astype(BF16), ln2_g[0][None, :], ln2_b[0][None, :], S // TM_ROUTE)
    return out.reshape(B, S, D)
```

```python
import functools

import jax
import jax.numpy as jnp
from jax import lax
from jax.experimental import pallas as pl
from jax.experimental.pallas import tpu as pltpu

F32 = jnp.float32
BF16 = jnp.bfloat16
I32 = jnp.int32

LANE = 128
SUBLANE = 8

D_MODEL = 1024
GRID_W = 64
POOL_WIDTH = 256
POOL_WINDOWS = (2, 4, 8, 16)
POOL_GROUP = 64
HEAD_DIM = 64
HALF = HEAD_DIM // 2
N_HEADS = 12
N_KV_HEADS = 4
GQA_GROUP = N_HEADS // N_KV_HEADS
ATTN_WIDTH = N_HEADS * HEAD_DIM
KV_WIDTH = N_KV_HEADS * HEAD_DIM
ROPE_THETA = 10000.0
N_EXPERTS = 64
TOP_K = 8
N_GROUPS = 8
GROUP_SIZE = N_EXPERTS // N_GROUPS
TOPK_GROUPS = 4
EXPERT_FF = 256
SHARED_FF = 256
ROUTED_SCALE = 2.5
DEPTH = 1
ALPHA = (2.0 * DEPTH) ** 0.25
LN_EPS = 1e-5
RMS_EPS = 1e-6

MOD_ROWS = 24
HALF_D = D_MODEL // 2
ROW_SUB = HALF_D // LANE
Q_COLS = N_HEADS * LANE
K_COLS = N_KV_HEADS * LANE
ATTN_PAD = N_KV_HEADS * 2 * LANE
VT_ROWS = HEAD_DIM + 16
LOG2_E = 1.4426950408889634

TM_INPROJ = 512
TQ_ATTN = 256
TM_MIX = 256
BM_MOE = 512
TM_ROUTE = 256

_NT = (((1,), (1,)), ((), ()))


def _dot(a, b):
    return jnp.dot(a, b, preferred_element_type=F32)


def _pack2(lo, hi):
    lo_bits = lax.bitcast_convert_type(lo.astype(BF16).astype(F32), I32)
    hi_bits = lax.bitcast_convert_type(hi.astype(BF16).astype(F32), I32)
    return lax.shift_right_logical(lo_bits, 16) | (hi_bits & jnp.int32(-65536))


def _unpack2(w):
    lo = lax.bitcast_convert_type(lax.shift_left(w, 16), F32)
    hi = lax.bitcast_convert_type(w & jnp.int32(-65536), F32)
    return lo, hi


def _store_rows(ref, packed):
    n = packed.shape[0]
    for s in range(ROW_SUB):
        ref[pl.ds(s, n, stride=ROW_SUB), :] = packed[:, s * LANE:(s + 1) * LANE]


def _load_rows(ref, n):
    return [ref[pl.ds(s, n, stride=ROW_SUB), :] for s in range(ROW_SUB)]


def _silu(x):
    return x * jax.nn.sigmoid(x)


def _layer_norm(r, g, b):
    mu = jnp.mean(r, axis=-1, keepdims=True)
    d = r - mu
    var = jnp.mean(d * d, axis=-1, keepdims=True)
    return d * lax.rsqrt(var + LN_EPS) * g + b


def _mod_kernel(c_ref, w_ref, b_ref, o_ref):
    a = _silu(c_ref[...])
    o_ref[...] = jnp.dot(a, w_ref[...], precision=lax.Precision.HIGHEST,
                         preferred_element_type=F32) + b_ref[...]


def _mod_call(cc, w_mod, b_mod):
    n = w_mod.shape[1]
    tn = 512
    return pl.pallas_call(
        _mod_kernel,
        grid=(n // tn,),
        in_specs=[pl.BlockSpec((MOD_ROWS, D_MODEL), lambda j: (0, 0)),
                  pl.BlockSpec((D_MODEL, tn), lambda j: (0, j)),
                  pl.BlockSpec((1, tn), lambda j: (0, j))],
        out_specs=pl.BlockSpec((MOD_ROWS, tn), lambda j: (0, j)),
        out_shape=jax.ShapeDtypeStruct((MOD_ROWS, n), F32),
        name="mod",
    )(cc, w_mod, b_mod)


def _norm_head(seg, g):
    ms = jnp.sum(seg * seg, axis=-1, keepdims=True) * (1.0 / HEAD_DIM)
    return seg * lax.rsqrt(ms + RMS_EPS) * g


def _rope(xn, c, s):
    return xn * c + pltpu.roll(xn, LANE // 2, axis=1) * s


def _inproj_kernel(x_ref, mod_ref, w_ref, qg_ref, kg_ref, cq_ref, sq_ref, ck_ref, sk_ref,
                   u_ref, q_ref, k_ref, v_ref):
    sh = mod_ref[0, 0:1, :]
    sc = mod_ref[0, 1:2, :]
    xm = (x_ref[0] * (1.0 + sc) + sh).astype(BF16)
    u_ref[0] = _dot(xm, w_ref[:, 0:POOL_WIDTH])
    qg, kg = qg_ref[...], kg_ref[...]
    cq, sq, ck, sk = cq_ref[...], sq_ref[...], ck_ref[...], sk_ref[...]
    base = POOL_WIDTH
    for pair in range(N_HEADS // 2):
        p = _dot(xm, w_ref[:, base + pair * 2 * LANE: base + (pair + 1) * 2 * LANE])
        for j in range(2):
            h = pair * 2 + j
            q = _rope(_norm_head(p[:, j * LANE:(j + 1) * LANE], qg), cq, sq)
            q_ref[0, :, h * LANE:(h + 1) * LANE] = q.astype(BF16)
    base += Q_COLS
    for pair in range(N_KV_HEADS // 2):
        p = _dot(xm, w_ref[:, base + pair * 2 * LANE: base + (pair + 1) * 2 * LANE])
        for j in range(2):
            h = pair * 2 + j
            k = _rope(_norm_head(p[:, j * LANE:(j + 1) * LANE], kg), ck, sk)
            k_ref[0, :, h * LANE:(h + 1) * LANE] = k.astype(BF16)
    base += K_COLS
    v_ref[0] = _dot(xm, w_ref[:, base:base + KV_WIDTH]).astype(BF16)


def _inproj_call(x, mod3, w_all, qg, kg, cq, sq, ck, sk):
    B, S, _ = x.shape
    tm = TM_INPROJ
    ncol = w_all.shape[1]
    tab = pl.BlockSpec((tm, LANE), lambda b, i: (i, 0))
    vec = pl.BlockSpec((1, LANE), lambda b, i: (0, 0))
    return pl.pallas_call(
        _inproj_kernel,
        grid=(B, S // tm),
        in_specs=[pl.BlockSpec((1, tm, D_MODEL), lambda b, i: (b, i, 0)),
                  pl.BlockSpec((1, 6, D_MODEL), lambda b, i: (b, 0, 0)),
                  pl.BlockSpec((D_MODEL, ncol), lambda b, i: (0, 0)),
                  vec, vec, tab, tab, tab, tab],
        out_specs=[pl.BlockSpec((1, tm, POOL_WIDTH), lambda b, i: (b, i, 0)),
                   pl.BlockSpec((1, tm, Q_COLS), lambda b, i: (b, i, 0)),
                   pl.BlockSpec((1, tm, K_COLS), lambda b, i: (b, i, 0)),
                   pl.BlockSpec((1, tm, KV_WIDTH), lambda b, i: (b, i, 0))],
        out_shape=[jax.ShapeDtypeStruct((B, S, POOL_WIDTH), F32),
                   jax.ShapeDtypeStruct((B, S, Q_COLS), BF16),
                   jax.ShapeDtypeStruct((B, S, K_COLS), BF16),
                   jax.ShapeDtypeStruct((B, S, KV_WIDTH), BF16)],
        name="inproj",
    )(x, mod3, w_all, qg, kg, cq, sq, ck, sk)


def _ctx_kv_kernel(x_ref, mod_ref, w_ref, kg_ref, k_ref, v_ref):
    sh = mod_ref[0, 0:1, :]
    sc = mod_ref[0, 1:2, :]
    xm = (x_ref[0] * (1.0 + sc) + sh).astype(BF16)
    kg = kg_ref[...]
    for pair in range(N_KV_HEADS // 2):
        p = _dot(xm, w_ref[:, pair * 2 * LANE:(pair + 1) * 2 * LANE])
        for j in range(2):
            h = pair * 2 + j
            k_ref[0, :, h * LANE:(h + 1) * LANE] = _norm_head(p[:, j * LANE:(j + 1) * LANE], kg).astype(BF16)
    v_ref[0] = _dot(xm, w_ref[:, K_COLS:K_COLS + KV_WIDTH]).astype(BF16)


def _ctx_kv_call(ctx, mod3, w_kv, kg, ctx_row):
    B, C, _ = ctx.shape
    return pl.pallas_call(
        _ctx_kv_kernel,
        grid=(B,),
        in_specs=[pl.BlockSpec((1, C, D_MODEL), lambda b: (b, 0, 0)),
                  pl.BlockSpec((1, 6, D_MODEL), lambda b: (ctx_row, 0, 0)),
                  pl.BlockSpec((D_MODEL, K_COLS + KV_WIDTH), lambda b: (0, 0)),
                  pl.BlockSpec((1, LANE), lambda b: (0, 0))],
        out_specs=[pl.BlockSpec((1, C, K_COLS), lambda b: (b, 0, 0)),
                   pl.BlockSpec((1, C, KV_WIDTH), lambda b: (b, 0, 0))],
        out_shape=[jax.ShapeDtypeStruct((B, C, K_COLS), BF16),
                   jax.ShapeDtypeStruct((B, C, KV_WIDTH), BF16)],
        name="ctx_kv",
    )(ctx, mod3, w_kv, kg)


POOL_PAD = 8


def _pool_kernel(u_ref, bd_ref, ps_ref, y_ref):
    S = u_ref.shape[1]
    n = S + 2 * POOL_PAD
    t = lax.broadcasted_iota(I32, (S, LANE), 0)
    lane = lax.broadcasted_iota(I32, (S, LANE), 1)
    zpad = jnp.zeros((POOL_PAD, LANE), F32)
    for half in range(POOL_WIDTH // LANE):
        u = u_ref[0, :, half * LANE:(half + 1) * LANE]
        ue = jnp.concatenate([zpad, u, zpad], axis=0)
        fwd = {1: ue}
        w = 1
        while w < POOL_WINDOWS[2 * half + 1]:
            fwd[2 * w] = fwd[w] + pltpu.roll(fwd[w], n - w, axis=0)
            w *= 2
        ds = []
        for win in POOL_WINDOWS[2 * half: 2 * half + 2]:
            hw = win // 2
            centred = pltpu.roll(fwd[win], hw, axis=0)[POOL_PAD:POOL_PAD + S]
            cnt = (jnp.minimum(t + hw, S) - jnp.maximum(t - hw, 0)).astype(F32)
            ds.append(centred / cnt - u)
        d = jnp.where(lane < POOL_GROUP, ds[0], ds[1]).astype(BF16)
        sl = slice(half * LANE, (half + 1) * LANE)
        y = _dot(d, bd_ref[sl, sl]) * ps_ref[:, sl]
        y_ref[0, :, sl] = y.astype(BF16)


def _pool_call(u, bd, ps):
    B, S, _ = u.shape
    return pl.pallas_call(
        _pool_kernel,
        grid=(B,),
        in_specs=[pl.BlockSpec((1, S, POOL_WIDTH), lambda b: (b, 0, 0)),
                  pl.BlockSpec((POOL_WIDTH, POOL_WIDTH), lambda b: (0, 0)),
                  pl.BlockSpec((1, POOL_WIDTH), lambda b: (0, 0))],
        out_specs=pl.BlockSpec((1, S, POOL_WIDTH), lambda b: (b, 0, 0)),
        out_shape=jax.ShapeDtypeStruct((B, S, POOL_WIDTH), BF16),
        name="pool",
    )(u, bd, ps)


ROW_REDUCE_WAYS = 16


def _reduce_rows(op, x):
    rows, lanes = x.shape
    if rows % (ROW_REDUCE_WAYS * SUBLANE) == 0:
        x = op(x.reshape(ROW_REDUCE_WAYS, rows // ROW_REDUCE_WAYS, lanes), axis=0)
    return op(x, axis=0, keepdims=True)


def _attn_kernel(q_ref, k_ref, vt_ref, o_ref, s_buf, m_buf):
    t = pl.program_id(0)
    tq = q_ref.shape[1]

    @pl.when(t == 0)
    def _():
        s_buf[1] = jnp.zeros_like(s_buf[1])
        m_buf[1] = jnp.zeros_like(m_buf[1])

    def step(slot):
        k = k_ref[0]
        vt = vt_ref[0, 0]
        outs = []
        for j in range(GQA_GROUP):
            s = lax.dot_general(k, q_ref[0, :, j * LANE:(j + 1) * LANE], _NT,
                                preferred_element_type=F32)
            s_buf[slot, j] = s
            m_buf[slot, j] = jnp.broadcast_to(_reduce_rows(jnp.max, s), (SUBLANE, tq))
            p = jnp.exp2(s_buf[1 - slot, j] - m_buf[1 - slot, j, 0:1, :])
            ol = _dot(vt, p.astype(BF16))
            outs.append(ol[:HEAD_DIM] / ol[HEAD_DIM:HEAD_DIM + 1])
        outs.append(jnp.zeros((2 * LANE - GQA_GROUP * HEAD_DIM, tq), F32))
        o_ref[0] = jnp.concatenate(outs, axis=0).T.astype(BF16)

    for slot in range(2):
        pl.when(t % 2 == slot)(functools.partial(step, slot))


def _attn_call(q, k_all, vt_all):
    B, S, _ = q.shape
    Lk = k_all.shape[1]
    tq = TQ_ATTN
    nq = S // tq
    n_items = B * N_KV_HEADS * nq

    def item(t):
        return t // (N_KV_HEADS * nq), (t // nq) % N_KV_HEADS, t % nq

    def cur(t):
        return item(jnp.minimum(t, n_items - 1))

    def prev(t):
        return item(jnp.maximum(t - 1, 0))

    return pl.pallas_call(
        _attn_kernel,
        grid=(n_items + 1,),
        in_specs=[pl.BlockSpec((1, tq, GQA_GROUP * LANE), lambda t: (cur(t)[0], cur(t)[2], cur(t)[1])),
                  pl.BlockSpec((1, Lk, LANE), lambda t: (cur(t)[0], 0, cur(t)[1])),
                  pl.BlockSpec((1, 1, VT_ROWS, Lk), lambda t: (prev(t)[0], prev(t)[1], 0, 0))],
        out_specs=pl.BlockSpec((1, tq, 2 * LANE), lambda t: (prev(t)[0], prev(t)[2], prev(t)[1])),
        out_shape=jax.ShapeDtypeStruct((B, S, ATTN_PAD), BF16),
        scratch_shapes=[pltpu.VMEM((2, GQA_GROUP, Lk, tq), F32),
                        pltpu.VMEM((2, GQA_GROUP, SUBLANE, tq), F32)],
        compiler_params=pltpu.CompilerParams(dimension_semantics=("arbitrary",)),
        name="attn",
    )(q, k_all, vt_all)


def _mix_kernel(x_ref, yp_ref, at_ref, wop_ref, woa_ref, mod_ref, g_ref, b_ref, wr_ref, rb_ref, tri_ref,
                xmid_ref, hp_ref, idx_ref, wts_ref, rank_ref, cnt_ref, carry):
    i = pl.program_id(0)
    tm = x_ref.shape[0]

    @pl.when(i == 0)
    def _():
        carry[...] = jnp.zeros_like(carry)

    mix = _dot(yp_ref[...], wop_ref[...]) + _dot(at_ref[...], woa_ref[...])
    g1 = mod_ref[0, 2:3, :]
    sh2 = mod_ref[0, 3:4, :]
    sc2 = mod_ref[0, 4:5, :]
    xmid = _layer_norm(ALPHA * x_ref[...] + g1 * mix, g_ref[...], b_ref[...])
    xmid_ref[...] = xmid
    h = xmid * (1.0 + sc2) + sh2
    _store_rows(hp_ref, _pack2(h[:, :HALF_D], h[:, HALF_D:]))

    logits = lax.dot_general(wr_ref[...], h, _NT, precision=lax.Precision.HIGHEST,
                             preferred_element_type=F32)
    score = jax.nn.sigmoid(logits)
    sel = score + rb_ref[...]
    neg = jnp.float32(-jnp.inf)
    sub = lax.broadcasted_iota(I32, (GROUP_SIZE, tm), 0)
    grp_sel, grp_score = [], []
    for g in range(N_GROUPS):
        sg = sel[g * GROUP_SIZE:(g + 1) * GROUP_SIZE, :]
        m1 = jnp.max(sg, axis=0, keepdims=True)
        i1 = jnp.min(jnp.where(sg == m1, sub, GROUP_SIZE), axis=0, keepdims=True)
        m2 = jnp.max(jnp.where(sub == i1, neg, sg), axis=0, keepdims=True)
        grp_sel.append(sg)
        grp_score.append(m1 + m2)
    masked = []
    for g in range(N_GROUPS):
        ahead = jnp.zeros((1, tm), I32)
        for o in range(N_GROUPS):
            if o == g:
                continue
            beats = (grp_score[o] > grp_score[g]) | ((grp_score[o] == grp_score[g]) & (o < g))
            ahead = ahead + beats.astype(I32)
        masked.append(jnp.where(ahead < TOPK_GROUPS, grp_sel[g], neg))
    cur = jnp.concatenate(masked, axis=0)
    row = lax.broadcasted_iota(I32, (N_EXPERTS, tm), 0)
    member = jnp.zeros((N_EXPERTS, tm), F32)
    picks, wsel = [], []
    for k in range(TOP_K):
        mx = jnp.max(cur, axis=0, keepdims=True)
        ei = jnp.min(jnp.where(cur == mx, row, N_EXPERTS), axis=0, keepdims=True)
        hit = row == ei
        picks.append(ei)
        wsel.append(jnp.sum(jnp.where(hit, score, 0.0), axis=0, keepdims=True))
        cur = jnp.where(hit, neg, cur)
        member = jnp.where(hit, 1.0, member)
    tot = wsel[0]
    for k in range(1, TOP_K):
        tot = tot + wsel[k]
    before = _dot(member.astype(BF16), tri_ref[...]) + carry[:, 0:1]
    for k in range(TOP_K):
        idx_ref[k:k + 1, :] = picks[k]
        rk = jnp.sum(jnp.where(row == picks[k], before, 0.0), axis=0, keepdims=True)
        rank_ref[k:k + 1, :] = rk.astype(I32)
    wrows = [wsel[k] / tot * ROUTED_SCALE for k in range(TOP_K)]
    wmat = jnp.concatenate(wrows + [jnp.zeros((LANE - TOP_K, tm), F32)], axis=0)
    wts_ref[...] = wmat.T
    carry[...] = carry[...] + jnp.sum(member, axis=1, keepdims=True)
    cnt_ref[...] = carry[...].astype(I32)


def _mix_call(x2, ypool, attn, wop, woa, mod3, g, b, wr_t, rb, tri, tiles_per_seq):
    T = x2.shape[0]
    tm = TM_MIX
    row = lambda i: (i, 0)
    fixed = lambda i: (0, 0)
    tok = lambda i: (0, i)
    return pl.pallas_call(
        _mix_kernel,
        grid=(T // tm,),
        in_specs=[pl.BlockSpec((tm, D_MODEL), row),
                  pl.BlockSpec((tm, POOL_WIDTH), row),
                  pl.BlockSpec((tm, ATTN_PAD), row),
                  pl.BlockSpec((POOL_WIDTH, D_MODEL), fixed),
                  pl.BlockSpec((ATTN_PAD, D_MODEL), fixed),
                  pl.BlockSpec((1, 6, D_MODEL), lambda i: (i // tiles_per_seq, 0, 0)),
                  pl.BlockSpec((1, D_MODEL), fixed),
                  pl.BlockSpec((1, D_MODEL), fixed),
                  pl.BlockSpec((N_EXPERTS, D_MODEL), fixed),
                  pl.BlockSpec((N_EXPERTS, 1), fixed),
                  pl.BlockSpec((tm, tm), fixed)],
        out_specs=[pl.BlockSpec((tm, D_MODEL), row),
                   pl.BlockSpec((tm * ROW_SUB, LANE), row),
                   pl.BlockSpec((TOP_K, tm), tok),
                   pl.BlockSpec((tm, LANE), row),
                   pl.BlockSpec((TOP_K, tm), tok),
                   pl.BlockSpec((N_EXPERTS, LANE), fixed)],
        out_shape=[jax.ShapeDtypeStruct((T, D_MODEL), F32),
                   jax.ShapeDtypeStruct((T * ROW_SUB, LANE), I32),
                   jax.ShapeDtypeStruct((TOP_K, T), I32),
                   jax.ShapeDtypeStruct((T, LANE), F32),
                   jax.ShapeDtypeStruct((TOP_K, T), I32),
                   jax.ShapeDtypeStruct((N_EXPERTS, LANE), I32)],
        scratch_shapes=[pltpu.VMEM((N_EXPERTS, LANE), F32)],
        compiler_params=pltpu.CompilerParams(dimension_semantics=("arbitrary",)),
        name="mix",
    )(x2, ypool, attn, wop, woa, mod3, g, b, wr_t, rb, tri)


def _slab(r):
    return pl.ds(pl.multiple_of(r * ROW_SUB, ROW_SUB), ROW_SUB)


def _dest_kernel(ps_ref, idx_ref, rank_ref, o_ref):
    idx = idx_ref[...]
    d = rank_ref[...]
    for e in range(N_EXPERTS):
        d = d + jnp.where(idx == e, ps_ref[e], 0)
    tm = o_ref.shape[1]
    for j in range(idx.shape[1] // tm):
        o_ref[j * TOP_K:(j + 1) * TOP_K, :] = d[:, j * tm:(j + 1) * tm]


def _dest_call(pad_start, idx_t, rank_t):
    T = idx_t.shape[1]
    tm = TM_ROUTE
    td = 2048 if T % 2048 == 0 else tm
    per = td // tm
    grid_spec = pltpu.PrefetchScalarGridSpec(
        num_scalar_prefetch=1,
        grid=(T // td,),
        in_specs=[pl.BlockSpec((TOP_K, td), lambda i, ps: (0, i)),
                  pl.BlockSpec((TOP_K, td), lambda i, ps: (0, i))],
        out_specs=pl.BlockSpec((per * TOP_K, tm), lambda i, ps: (i, 0)),
    )
    return pl.pallas_call(
        _dest_kernel,
        grid_spec=grid_spec,
        out_shape=jax.ShapeDtypeStruct((T // tm * TOP_K, tm), I32),
        name="dest",
    )(pad_start, idx_t, rank_t)


def _dispatch_kernel(pst_ref, plen_ref, nreal_ref, dest_hbm, hp_ref, xs_hbm, dst0, dst1, zbuf, isem, ssem, zsem):
    tm = hp_ref.shape[0] // ROW_SUB
    bm = zbuf.shape[0] // ROW_SUB
    n_idx = TOP_K * tm
    nb = xs_hbm.shape[0] // (bm * ROW_SUB)
    i = pl.program_id(0)
    nt = pl.num_programs(0)
    dst = (dst0, dst1)

    def idx_copy(tile, s):
        off = pl.multiple_of(tile * n_idx, n_idx)
        return pltpu.make_async_copy(dest_hbm.at[pl.ds(off, n_idx)], dst[s], isem.at[s])

    def pad_fill(e):
        n = plen_ref[e] * ROW_SUB
        start = pl.multiple_of(pst_ref[e] * ROW_SUB, ROW_SUB)
        return pltpu.make_async_copy(zbuf.at[pl.ds(0, n)], xs_hbm.at[pl.ds(start, n)], zsem)

    def dead_fill(j):
        return pltpu.make_async_copy(zbuf, xs_hbm.at[pl.ds(j * bm * ROW_SUB, bm * ROW_SUB)], zsem)

    def fills(act):
        for e in range(N_EXPERTS):
            pl.when(plen_ref[e] > 0)(functools.partial(lambda e: act(pad_fill(e)), e))
        for j in range(nb - N_EXPERTS, nb):
            pl.when(j >= nreal_ref[0])(functools.partial(lambda j: act(dead_fill(j)), j))

    @pl.when(i == 0)
    def _():
        zbuf[...] = jnp.zeros_like(zbuf)
        idx_copy(0, 0).start()
        idx_copy(0, 0).wait()
        fills(lambda cp: cp.start())

    def scatter(s):
        @pl.when(i + 1 < nt)
        def _():
            idx_copy(i + 1, 1 - s).start()

        def body(rb, c):
            for rr in range(SUBLANE):
                r = rb * SUBLANE + rr
                for k in range(TOP_K):
                    pltpu.make_async_copy(hp_ref.at[_slab(r)], xs_hbm.at[_slab(dst[s][k * tm + r])],
                                          ssem).start(priority=k % 2)
            return c
        lax.fori_loop(0, tm // SUBLANE, body, 0)
        for k in range(TOP_K):
            pltpu.make_async_copy(hp_ref, xs_hbm.at[pl.ds(0, tm * ROW_SUB)], ssem).wait()

        @pl.when(i + 1 < nt)
        def _():
            idx_copy(i + 1, 1 - s).wait()

    for s in range(2):
        pl.when(i % 2 == s)(functools.partial(scatter, s))

    @pl.when(i == 0)
    def _():
        fills(lambda cp: cp.wait())


def _dispatch_call(pad_row_start, pad_len, n_real, dest, hp, nb):
    tm = TM_ROUTE
    bm = BM_MOE
    T = hp.shape[0] // ROW_SUB
    grid_spec = pltpu.PrefetchScalarGridSpec(
        num_scalar_prefetch=3,
        grid=(T // tm,),
        in_specs=[pl.BlockSpec(memory_space=pl.ANY),
                  pl.BlockSpec((tm * ROW_SUB, LANE), lambda i, a, b, c: (i, 0))],
        out_specs=pl.BlockSpec(memory_space=pl.ANY),
        scratch_shapes=[pltpu.SMEM((TOP_K * tm,), I32),
                        pltpu.SMEM((TOP_K * tm,), I32),
                        pltpu.VMEM((bm * ROW_SUB, LANE), I32),
                        pltpu.SemaphoreType.DMA((2,)),
                        pltpu.SemaphoreType.DMA,
                        pltpu.SemaphoreType.DMA],
    )
    return pl.pallas_call(
        _dispatch_kernel,
        grid_spec=grid_spec,
        out_shape=jax.ShapeDtypeStruct((nb * bm * ROW_SUB, LANE), I32),
        compiler_params=pltpu.CompilerParams(dimension_semantics=("arbitrary",)),
        name="dispatch",
    )(pad_row_start, pad_len, n_real, dest, hp)


def _moe_kernel(be_ref, nreal_ref, x_ref, wg_ref, wu_ref, wd_ref, y_ref, wgb, wub, wdb):
    bm = x_ref.shape[0] // ROW_SUB
    i = pl.program_id(0)
    n = nreal_ref[0]

    @pl.when(i < n)
    def _():
        e_changed = (i == 0) | (be_ref[i] != be_ref[jnp.maximum(i - 1, 0)])

        @pl.when(e_changed)
        def _():
            wgb[...] = wg_ref[0].astype(BF16)
            wub[...] = wu_ref[0].astype(BF16)
            wdb[...] = wd_ref[0].astype(BF16)

        halves = [_unpack2(w) for w in _load_rows(x_ref, bm)]
        xb = jnp.concatenate([lo.astype(BF16) for lo, _ in halves] + [hi.astype(BF16) for _, hi in halves], axis=1)
        a = _silu(_dot(xb, wgb[...])) * _dot(xb, wub[...])
        y = _dot(a.astype(BF16), wdb[...])
        _store_rows(y_ref, _pack2(y[:, :HALF_D], y[:, HALF_D:]))

    @pl.when(i >= n)
    def _():
        y_ref[...] = jnp.zeros_like(y_ref)


def _moe_call(block_expert, n_real, xs, w_gate, w_up, w_down):
    bm = BM_MOE
    nb = block_expert.shape[0]
    wmap = lambda i, be, nr: (be[i], 0, 0)
    rows = pl.BlockSpec((bm * ROW_SUB, LANE), lambda i, be, nr: (i, 0))
    grid_spec = pltpu.PrefetchScalarGridSpec(
        num_scalar_prefetch=2,
        grid=(nb,),
        in_specs=[rows,
                  pl.BlockSpec((1, D_MODEL, EXPERT_FF), wmap),
                  pl.BlockSpec((1, D_MODEL, EXPERT_FF), wmap),
                  pl.BlockSpec((1, EXPERT_FF, D_MODEL), wmap)],
        out_specs=rows,
        scratch_shapes=[pltpu.VMEM((D_MODEL, EXPERT_FF), BF16),
                        pltpu.VMEM((D_MODEL, EXPERT_FF), BF16),
                        pltpu.VMEM((EXPERT_FF, D_MODEL), BF16)],
    )
    return pl.pallas_call(
        _moe_kernel,
        grid_spec=grid_spec,
        out_shape=jax.ShapeDtypeStruct(xs.shape, I32),
        compiler_params=pltpu.CompilerParams(dimension_semantics=("arbitrary",)),
        name="moe",
    )(block_expert, n_real, xs, w_gate, w_up, w_down)


def _combine_kernel(dest_hbm, ys_hbm, wc_ref, xmid_ref, mod_ref, wsg_ref, wsu_ref, wsd_ref, g_ref, b_ref,
                    o_ref, dst0, dst1, rbuf, isem, gsem):
    tm = xmid_ref.shape[0]
    n_idx = TOP_K * tm
    i = pl.program_id(0)
    nt = pl.num_programs(0)
    dst = (dst0, dst1)

    def idx_copy(tile, s):
        off = pl.multiple_of(tile * n_idx, n_idx)
        return pltpu.make_async_copy(dest_hbm.at[pl.ds(off, n_idx)], dst[s], isem.at[s])

    def start_gather(s):
        def body(rb, c):
            for rr in range(SUBLANE):
                r = rb * SUBLANE + rr
                for k in range(TOP_K):
                    pltpu.make_async_copy(ys_hbm.at[_slab(dst[s][k * tm + r])], rbuf.at[s, k, _slab(r)],
                                          gsem.at[s]).start(priority=k % 2)
            return c
        lax.fori_loop(0, tm // SUBLANE, body, 0)

    def wait_gather(s):
        for k in range(TOP_K):
            pltpu.make_async_copy(ys_hbm.at[pl.ds(0, tm * ROW_SUB)], rbuf.at[s, k], gsem.at[s]).wait()

    @pl.when(i == 0)
    def _():
        idx_copy(0, 0).start()
        idx_copy(0, 0).wait()
        start_gather(0)

        @pl.when(nt > 1)
        def _():
            idx_copy(1, 1).start()

    def prefetch(s):
        @pl.when(i + 2 < nt)
        def _():
            idx_copy(i + 2, s).start()

        @pl.when(i + 1 < nt)
        def _():
            idx_copy(i + 1, 1 - s).wait()
            start_gather(1 - s)

        wait_gather(s)

    for s in range(2):
        pl.when(i % 2 == s)(functools.partial(prefetch, s))

    wc = wc_ref[...]
    slot = i % 2
    acc = None
    for k in range(TOP_K):
        parts = [_unpack2(w) for w in _load_rows(rbuf.at[slot, k], tm)]
        parts = [lo for lo, _ in parts] + [hi for _, hi in parts]
        wk = wc[:, k:k + 1]
        acc = [wk * p for p in parts] if acc is None else [a + wk * p for a, p in zip(acc, parts)]
    routed = jnp.concatenate(acc, axis=1)
    xmid = xmid_ref[...]
    sh2 = mod_ref[0, 3:4, :]
    sc2 = mod_ref[0, 4:5, :]
    g2 = mod_ref[0, 5:6, :]
    hb = (xmid * (1.0 + sc2) + sh2).astype(BF16)
    a = _silu(_dot(hb, wsg_ref[...])) * _dot(hb, wsu_ref[...])
    shared = _dot(a.astype(BF16), wsd_ref[...])
    o_ref[...] = _layer_norm(ALPHA * xmid + g2 * (routed + shared), g_ref[...], b_ref[...])


def _combine_call(dest, ys, wcol, xmid, mod3, wsg, wsu, wsd, g, b, tiles_per_seq):
    T = xmid.shape[0]
    tm = TM_ROUTE
    fixed = lambda i: (0, 0)
    row = lambda i: (i, 0)
    return pl.pallas_call(
        _combine_kernel,
        grid=(T // tm,),
        in_specs=[pl.BlockSpec(memory_space=pl.ANY),
                  pl.BlockSpec(memory_space=pl.ANY),
                  pl.BlockSpec((tm, LANE), row),
                  pl.BlockSpec((tm, D_MODEL), row),
                  pl.BlockSpec((1, 6, D_MODEL), lambda i: (i // tiles_per_seq, 0, 0)),
                  pl.BlockSpec((D_MODEL, SHARED_FF), fixed),
                  pl.BlockSpec((D_MODEL, SHARED_FF), fixed),
                  pl.BlockSpec((SHARED_FF, D_MODEL), fixed),
                  pl.BlockSpec((1, D_MODEL), fixed),
                  pl.BlockSpec((1, D_MODEL), fixed)],
        out_specs=pl.BlockSpec((tm, D_MODEL), row),
        out_shape=jax.ShapeDtypeStruct((T, D_MODEL), F32),
        scratch_shapes=[pltpu.SMEM((TOP_K * tm,), I32),
                        pltpu.SMEM((TOP_K * tm,), I32),
                        pltpu.VMEM((2, TOP_K, tm * ROW_SUB, LANE), I32),
                        pltpu.SemaphoreType.DMA((2,)),
                        pltpu.SemaphoreType.DMA((2,))],
        compiler_params=pltpu.CompilerParams(dimension_semantics=("arbitrary",)),
        name="combine",
    )(dest, ys, wcol, xmid, mod3, wsg, wsu, wsd, g, b)


def _head_cols(w, n_heads):
    d = w.shape[0]
    w = w.reshape(d, n_heads, HALF, 2)
    z = jnp.zeros((d, n_heads, HALF), w.dtype)
    return jnp.concatenate([w[..., 0], z, w[..., 1], z], axis=-1).reshape(d, n_heads * LANE)


def _head_vec(g):
    g = g.reshape(HALF, 2)
    z = jnp.zeros((HALF,), g.dtype)
    return jnp.concatenate([g[:, 0], z, g[:, 1], z]).reshape(1, LANE)


def _rope_tables(seq):
    inv_freq = ROPE_THETA ** (-jnp.arange(0, HALF, 2, dtype=F32) / HALF)
    pos = jnp.arange(seq)
    rowp = (pos // GRID_W).astype(F32)
    colp = (pos % GRID_W).astype(F32)
    ang = jnp.concatenate([rowp[:, None] * inv_freq, colp[:, None] * inv_freq], axis=-1)
    cos, sin = jnp.cos(ang), jnp.sin(ang)
    z = jnp.zeros_like(cos)
    return jnp.concatenate([cos, z, cos, z], axis=-1), jnp.concatenate([-sin, z, sin, z], axis=-1)


def kernel(x, c, ctx, c_ctx, w_mod, b_mod, w_in, q_norm, k_norm, pool_w, pool_scale, w_out, ln1_g, ln1_b,
           w_router, router_bias, w_gate, w_up, w_down, ws_gate, ws_up, ws_down, ln2_g, ln2_b):
    B, S, D = x.shape
    C = ctx.shape[1]
    T = B * S
    assert D == D_MODEL and w_mod.shape[0] == DEPTH and B + 1 <= MOD_ROWS
    assert S % TM_INPROJ == 0 and S % TQ_ATTN == 0 and S % TM_MIX == 0 and S % TM_ROUTE == 0
    assert S % GRID_W == 0 and C % SUBLANE == 0

    cc = jnp.concatenate([c, c_ctx[None, :], jnp.zeros((MOD_ROWS - B - 1, D), F32)], axis=0)
    mod3 = _mod_call(cc, w_mod[0], b_mod[0][None, :]).reshape(MOD_ROWS, 6, D)

    w = w_in[0]
    o1, o2, o3 = POOL_WIDTH, POOL_WIDTH + ATTN_WIDTH, POOL_WIDTH + ATTN_WIDTH + KV_WIDTH
    wq = _head_cols(w[:, o1:o2], N_HEADS)
    wk = _head_cols(w[:, o2:o3], N_KV_HEADS)
    w_all = jnp.concatenate([w[:, :o1], wq, wk, w[:, o3:]], axis=1).astype(BF16)
    w_kv = jnp.concatenate([wk, w[:, o3:]], axis=1).astype(BF16)
    qg, kg = _head_vec(q_norm[0]), _head_vec(k_norm[0])
    cos_t, sin_t = _rope_tables(S)
    qscale = HEAD_DIM ** -0.5 * LOG2_E
    u, q, k_l, v_l = _inproj_call(x, mod3, w_all, qg, kg, cos_t * qscale, sin_t * qscale, cos_t, sin_t)
    k_c, v_c = _ctx_kv_call(ctx, mod3, w_kv, kg, B)

    k_all = jnp.concatenate([k_c, k_l], axis=1)
    v_all = jnp.concatenate([v_c, v_l], axis=1).reshape(B, C + S, N_KV_HEADS, HEAD_DIM)
    vt = jnp.transpose(v_all, (0, 2, 3, 1))
    ones = jnp.ones((B, N_KV_HEADS, 1, C + S), BF16)
    pad = jnp.zeros((B, N_KV_HEADS, VT_ROWS - HEAD_DIM - 1, C + S), BF16)
    attn = _attn_call(q, k_all, jnp.concatenate([vt, ones, pad], axis=2))

    bd = jax.scipy.linalg.block_diag(*[pool_w[0, g] for g in range(len(POOL_WINDOWS))]).astype(BF16)
    ypool = _pool_call(u, bd, pool_scale[0][None, :])

    wo = w_out[0]
    wop = wo[:POOL_WIDTH].astype(BF16)
    woa = wo[POOL_WIDTH:].reshape(N_KV_HEADS, GQA_GROUP * HEAD_DIM, D)
    woa = jnp.pad(woa, ((0, 0), (0, 2 * LANE - GQA_GROUP * HEAD_DIM), (0, 0))).reshape(ATTN_PAD, D).astype(BF16)
    tri = (jnp.arange(TM_MIX)[:, None] < jnp.arange(TM_MIX)[None, :]).astype(BF16)
    xmid, hp, idx_t, wcol, rank_t, counts = _mix_call(
        x.reshape(T, D), ypool.reshape(T, POOL_WIDTH), attn.reshape(T, ATTN_PAD), wop, woa, mod3,
        ln1_g[0][None, :], ln1_b[0][None, :], w_router[0].T, router_bias[0][:, None], tri, S // TM_MIX)

    bm = BM_MOE
    counts = counts[:, 0]
    padded = ((counts + bm - 1) // bm) * bm
    pad_end = jnp.cumsum(padded)
    pad_start = pad_end - padded
    nb = T * TOP_K // bm + N_EXPERTS
    n_real = (pad_end[-1] // bm).astype(I32).reshape(1)
    blk_row = jnp.arange(nb, dtype=I32) * bm
    block_expert = jnp.minimum(jnp.sum((pad_end[None, :] <= blk_row[:, None]).astype(I32), axis=1), N_EXPERTS - 1)

    dest = _dest_call(pad_start.astype(I32), idx_t, rank_t).reshape(-1)
    xs = _dispatch_call((pad_start + counts).astype(I32), (padded - counts).astype(I32), n_real, dest, hp, nb)
    ys = _moe_call(block_expert.astype(I32), n_real, xs, w_gate[0], w_up[0], w_down[0])
    out = _combine_call(dest, ys, wcol, xmid, mod3, ws_gate[0].astype(BF16), ws_up[0].astype(BF16),
                        ws_down[0].astype(BF16), ln2_g[0][None, :], ln2_b[0][None, :], S // TM_ROUTE)
    return out.reshape(B, S, D)
```

```python
import functools

import jax
import jax.numpy as jnp
from jax import lax
from jax.experimental import pallas as pl
from jax.experimental.pallas import tpu as pltpu

F32 = jnp.float32
BF16 = jnp.bfloat16
I32 = jnp.int32

LANE = 128
SUBLANE = 8

D_MODEL = 1024
GRID_W = 64
POOL_WIDTH = 256
POOL_WINDOWS = (2, 4, 8, 16)
POOL_GROUP = 64
HEAD_DIM = 64
HALF = HEAD_DIM // 2
N_HEADS = 12
N_KV_HEADS = 4
GQA_GROUP = N_HEADS // N_KV_HEADS
ATTN_WIDTH = N_HEADS * HEAD_DIM
KV_WIDTH = N_KV_HEADS * HEAD_DIM
ROPE_THETA = 10000.0
N_EXPERTS = 64
TOP_K = 8
N_GROUPS = 8
GROUP_SIZE = N_EXPERTS // N_GROUPS
TOPK_GROUPS = 4
EXPERT_FF = 256
SHARED_FF = 256
ROUTED_SCALE = 2.5
DEPTH = 1
ALPHA = (2.0 * DEPTH) ** 0.25
LN_EPS = 1e-5
RMS_EPS = 1e-6

MOD_ROWS = 24
HALF_D = D_MODEL // 2
ROW_SUB = HALF_D // LANE
Q_COLS = N_HEADS // 2 * LANE
K_COLS = N_KV_HEADS * LANE
ATTN_PAD = N_KV_HEADS * 2 * LANE
VT_ROWS = HEAD_DIM + 16
LOG2_E = 1.4426950408889634

TM_INPROJ = 512
TQ_ATTN = 256
TM_MIX = 512
MIX_SUB = 256
BM_MOE = 512
MOE_ROW_CHUNKS = 2
TM_ROUTE = 256
GATHER_UNROLL = 8

_NT = (((1,), (1,)), ((), ()))


def _dot(a, b):
    return jnp.dot(a, b, preferred_element_type=F32)


def _pack2(lo, hi):
    lo_bits = lax.bitcast_convert_type(lo.astype(BF16).astype(F32), I32)
    hi_bits = lax.bitcast_convert_type(hi.astype(BF16).astype(F32), I32)
    return lax.shift_right_logical(lo_bits, 16) | (hi_bits & jnp.int32(-65536))


def _unpack2(w):
    lo = lax.bitcast_convert_type(lax.shift_left(w, 16), F32)
    hi = lax.bitcast_convert_type(w & jnp.int32(-65536), F32)
    return lo, hi


def _store_rows(ref, packed, row0=0):
    n = packed.shape[0]
    for s in range(ROW_SUB):
        ref[pl.ds(row0 * ROW_SUB + s, n, stride=ROW_SUB), :] = packed[:, s * LANE:(s + 1) * LANE]


def _load_rows(ref, n, row0=0):
    return [ref[pl.ds(row0 * ROW_SUB + s, n, stride=ROW_SUB), :] for s in range(ROW_SUB)]


def _silu(x):
    return x * jax.nn.sigmoid(x)


def _layer_norm(r, g, b):
    mu = jnp.mean(r, axis=-1, keepdims=True)
    d = r - mu
    var = jnp.mean(d * d, axis=-1, keepdims=True)
    return d * lax.rsqrt(var + LN_EPS) * g + b


def _mod_kernel(c_ref, w_ref, b_ref, o_ref):
    a = _silu(c_ref[...])
    o_ref[...] = jnp.dot(a, w_ref[...], precision=lax.Precision.HIGHEST,
                         preferred_element_type=F32) + b_ref[...]


def _mod_call(cc, w_mod, b_mod):
    n = w_mod.shape[1]
    tn = 512
    return pl.pallas_call(
        _mod_kernel,
        grid=(n // tn,),
        in_specs=[pl.BlockSpec((MOD_ROWS, D_MODEL), lambda j: (0, 0)),
                  pl.BlockSpec((D_MODEL, tn), lambda j: (0, j)),
                  pl.BlockSpec((1, tn), lambda j: (0, j))],
        out_specs=pl.BlockSpec((MOD_ROWS, tn), lambda j: (0, j)),
        out_shape=jax.ShapeDtypeStruct((MOD_ROWS, n), F32),
        name="mod",
    )(cc, w_mod, b_mod)


def _norm_head(seg, g):
    ms = jnp.sum(seg * seg, axis=-1, keepdims=True) * (1.0 / HEAD_DIM)
    return seg * lax.rsqrt(ms + RMS_EPS) * g


def _norm_rope(seg, gg, g, c, s):
    sq = seg * seg
    hi = sq.astype(BF16)
    lo = (sq - hi.astype(F32)).astype(BF16)
    ms = _dot(jnp.concatenate([hi, lo], axis=1), gg)
    xn = seg * lax.rsqrt(ms + RMS_EPS) * g
    return xn * c + pltpu.roll(xn, LANE // 2, axis=1) * s


def _inproj_kernel(x_ref, mod_ref, w_ref, gg_ref, qg_ref, kg_ref, cq_ref, sq_ref, ck_ref, sk_ref,
                   u_ref, q_ref, k_ref, v_ref):
    sh = mod_ref[0, 0:1, :]
    sc = mod_ref[0, 1:2, :]
    xm = (x_ref[0] * (1.0 + sc) + sh).astype(BF16)
    gg, qg = gg_ref[...], qg_ref[...]
    cq, sq = cq_ref[...], sq_ref[...]
    n_qp = Q_COLS // (2 * LANE)
    n_kp = K_COLS // (2 * LANE)
    n_chunks = 1 + n_qp + n_kp + 1

    def chunk(c):
        return _dot(xm, w_ref[:, c * 2 * LANE:(c + 1) * 2 * LANE])

    nxt = chunk(0)
    for c in range(n_chunks):
        p = nxt
        if c + 1 < n_chunks:
            nxt = chunk(c + 1)
        if c == 0:
            u_ref[0] = p
        elif c <= n_qp:
            for j in range(2):
                grp = (c - 1) * 2 + j
                q = _norm_rope(p[:, j * LANE:(j + 1) * LANE], gg, qg, cq, sq)
                q_ref[0, :, grp * LANE:(grp + 1) * LANE] = q.astype(BF16)
        elif c <= n_qp + n_kp:
            for j in range(2):
                h = (c - 1 - n_qp) * 2 + j
                k = _norm_rope(p[:, j * LANE:(j + 1) * LANE], gg, kg_ref[:, h * LANE:(h + 1) * LANE],
                               ck_ref[:, j * LANE:(j + 1) * LANE], sk_ref[:, j * LANE:(j + 1) * LANE])
                k_ref[0, :, h * LANE:(h + 1) * LANE] = k.astype(BF16)
        else:
            v_ref[0] = p.astype(BF16)


def _inproj_call(x, mod3, w_all, gg, qg, kg, cq, sq, ck, sk):
    B, S, _ = x.shape
    tm = TM_INPROJ
    ncol = w_all.shape[1]
    tab = pl.BlockSpec((tm, LANE), lambda b, i: (i, 0))
    tab2 = pl.BlockSpec((tm, 2 * LANE), lambda b, i: (i, 0))
    vec = pl.BlockSpec((1, LANE), lambda b, i: (0, 0))
    return pl.pallas_call(
        _inproj_kernel,
        grid=(B, S // tm),
        in_specs=[pl.BlockSpec((1, tm, D_MODEL), lambda b, i: (b, i, 0)),
                  pl.BlockSpec((1, 6, D_MODEL), lambda b, i: (b, 0, 0)),
                  pl.BlockSpec((D_MODEL, ncol), lambda b, i: (0, 0)),
                  pl.BlockSpec((2 * LANE, LANE), lambda b, i: (0, 0)),
                  vec, pl.BlockSpec((1, K_COLS), lambda b, i: (0, 0)), tab, tab, tab2, tab2],
        out_specs=[pl.BlockSpec((1, tm, POOL_WIDTH), lambda b, i: (b, i, 0)),
                   pl.BlockSpec((1, tm, Q_COLS), lambda b, i: (b, i, 0)),
                   pl.BlockSpec((1, tm, K_COLS), lambda b, i: (b, i, 0)),
                   pl.BlockSpec((1, tm, KV_WIDTH), lambda b, i: (b, i, 0))],
        out_shape=[jax.ShapeDtypeStruct((B, S, POOL_WIDTH), F32),
                   jax.ShapeDtypeStruct((B, S, Q_COLS), BF16),
                   jax.ShapeDtypeStruct((B, S, K_COLS), BF16),
                   jax.ShapeDtypeStruct((B, S, KV_WIDTH), BF16)],
        name="inproj",
    )(x, mod3, w_all, gg, qg, kg, cq, sq, ck, sk)


def _ctx_kv_kernel(x_ref, mod_ref, w_ref, kg_ref, k_ref, v_ref):
    sh = mod_ref[0, 0:1, :]
    sc = mod_ref[0, 1:2, :]
    xm = (x_ref[0] * (1.0 + sc) + sh).astype(BF16)
    for pair in range(N_KV_HEADS // 2):
        p = _dot(xm, w_ref[:, pair * 2 * LANE:(pair + 1) * 2 * LANE])
        for j in range(2):
            h = pair * 2 + j
            kn = _norm_head(p[:, j * LANE:(j + 1) * LANE], kg_ref[:, h * LANE:(h + 1) * LANE])
            k_ref[0, :, h * LANE:(h + 1) * LANE] = kn.astype(BF16)
    v_ref[0] = _dot(xm, w_ref[:, K_COLS:K_COLS + KV_WIDTH]).astype(BF16)


def _ctx_kv_call(ctx, mod3, w_kv, kg, ctx_row):
    B, C, _ = ctx.shape
    return pl.pallas_call(
        _ctx_kv_kernel,
        grid=(B,),
        in_specs=[pl.BlockSpec((1, C, D_MODEL), lambda b: (b, 0, 0)),
                  pl.BlockSpec((1, 6, D_MODEL), lambda b: (ctx_row, 0, 0)),
                  pl.BlockSpec((D_MODEL, K_COLS + KV_WIDTH), lambda b: (0, 0)),
                  pl.BlockSpec((1, K_COLS), lambda b: (0, 0))],
        out_specs=[pl.BlockSpec((1, C, K_COLS), lambda b: (b, 0, 0)),
                   pl.BlockSpec((1, C, KV_WIDTH), lambda b: (b, 0, 0))],
        out_shape=[jax.ShapeDtypeStruct((B, C, K_COLS), BF16),
                   jax.ShapeDtypeStruct((B, C, KV_WIDTH), BF16)],
        name="ctx_kv",
    )(ctx, mod3, w_kv, kg)


POOL_PAD = 8


def _pool_kernel(u_ref, bd_ref, ps_ref, y_ref):
    S = u_ref.shape[1]
    n = S + 2 * POOL_PAD
    t = lax.broadcasted_iota(I32, (S, LANE), 0)
    lane = lax.broadcasted_iota(I32, (S, LANE), 1)
    zpad = jnp.zeros((POOL_PAD, LANE), F32)
    for half in range(POOL_WIDTH // LANE):
        u = u_ref[0, :, half * LANE:(half + 1) * LANE]
        ue = jnp.concatenate([zpad, u, zpad], axis=0)
        fwd = {1: ue}
        w = 1
        while w < POOL_WINDOWS[2 * half + 1]:
            fwd[2 * w] = fwd[w] + pltpu.roll(fwd[w], n - w, axis=0)
            w *= 2
        ds = []
        for win in POOL_WINDOWS[2 * half: 2 * half + 2]:
            hw = win // 2
            centred = pltpu.roll(fwd[win], hw, axis=0)[POOL_PAD:POOL_PAD + S]
            cnt = (jnp.minimum(t + hw, S) - jnp.maximum(t - hw, 0)).astype(F32)
            ds.append(centred / cnt - u)
        d = jnp.where(lane < POOL_GROUP, ds[0], ds[1]).astype(BF16)
        sl = slice(half * LANE, (half + 1) * LANE)
        y = _dot(d, bd_ref[sl, sl]) * ps_ref[:, sl]
        y_ref[0, :, sl] = y.astype(BF16)


def _pool_call(u, bd, ps):
    B, S, _ = u.shape
    return pl.pallas_call(
        _pool_kernel,
        grid=(B,),
        in_specs=[pl.BlockSpec((1, S, POOL_WIDTH), lambda b: (b, 0, 0)),
                  pl.BlockSpec((POOL_WIDTH, POOL_WIDTH), lambda b: (0, 0)),
                  pl.BlockSpec((1, POOL_WIDTH), lambda b: (0, 0))],
        out_specs=pl.BlockSpec((1, S, POOL_WIDTH), lambda b: (b, 0, 0)),
        out_shape=jax.ShapeDtypeStruct((B, S, POOL_WIDTH), BF16),
        name="pool",
    )(u, bd, ps)


ROW_REDUCE_WAYS = 16


def _reduce_rows(op, x):
    rows, lanes = x.shape
    if rows % (ROW_REDUCE_WAYS * SUBLANE) == 0:
        x = op(x.reshape(ROW_REDUCE_WAYS, rows // ROW_REDUCE_WAYS, lanes), axis=0)
    return op(x, axis=0, keepdims=True)


def _attn_kernel(q_ref, k_ref, vt_ref, o_ref, s_buf, m_buf):
    t = pl.program_id(0)
    tq = q_ref.shape[1]

    @pl.when(t == 0)
    def _():
        s_buf[1] = jnp.zeros_like(s_buf[1])
        m_buf[1] = jnp.zeros_like(m_buf[1])

    def step(slot):
        k = k_ref[0]
        vt = vt_ref[0, 0]
        outs = []
        for j in range(GQA_GROUP):
            s = lax.dot_general(k, q_ref[0, :, j * LANE:(j + 1) * LANE], _NT,
                                preferred_element_type=F32)
            s_buf[slot, j] = s
            m_buf[slot, j] = jnp.broadcast_to(_reduce_rows(jnp.max, s), (SUBLANE, tq))
            p = jnp.exp2(s_buf[1 - slot, j] - m_buf[1 - slot, j, 0:1, :])
            ol = _dot(vt, p.astype(BF16))
            outs.append(ol[:HEAD_DIM] / ol[HEAD_DIM:HEAD_DIM + 1])
        outs.append(jnp.zeros((2 * LANE - GQA_GROUP * HEAD_DIM, tq), F32))
        o_ref[0] = jnp.concatenate(outs, axis=0).T.astype(BF16)

    for slot in range(2):
        pl.when(t % 2 == slot)(functools.partial(step, slot))


def _attn_call(q, k_all, vt_all):
    B, S, _ = q.shape
    Lk = k_all.shape[1]
    tq = TQ_ATTN
    nq = S // tq
    n_items = B * N_KV_HEADS * nq

    def item(t):
        return t // (N_KV_HEADS * nq), (t // nq) % N_KV_HEADS, t % nq

    def cur(t):
        return item(jnp.minimum(t, n_items - 1))

    def prev(t):
        return item(jnp.maximum(t - 1, 0))

    return pl.pallas_call(
        _attn_kernel,
        grid=(n_items + 1,),
        in_specs=[pl.BlockSpec((1, tq, GQA_GROUP * LANE), lambda t: (cur(t)[0], cur(t)[2], cur(t)[1] // 2)),
                  pl.BlockSpec((1, Lk, LANE), lambda t: (cur(t)[0], 0, cur(t)[1])),
                  pl.BlockSpec((1, 1, VT_ROWS, Lk), lambda t: (prev(t)[0], prev(t)[1], 0, 0))],
        out_specs=pl.BlockSpec((1, tq, 2 * LANE), lambda t: (prev(t)[0], prev(t)[2], prev(t)[1])),
        out_shape=jax.ShapeDtypeStruct((B, S, ATTN_PAD), BF16),
        scratch_shapes=[pltpu.VMEM((2, GQA_GROUP, Lk, tq), F32),
                        pltpu.VMEM((2, GQA_GROUP, SUBLANE, tq), F32)],
        compiler_params=pltpu.CompilerParams(dimension_semantics=("arbitrary",)),
        name="attn",
    )(q, k_all, vt_all)


def _mix_kernel(x_ref, yp_ref, at_ref, wop_ref, woa_ref, mod_ref, g_ref, b_ref, wr_ref, rb_ref, tri_ref,
                xmid_ref, hp_ref, idx_ref, wts_ref, rank_ref, cnt_ref, carry):
    i = pl.program_id(0)
    tm = tri_ref.shape[0]
    n_sub = x_ref.shape[0] // tm

    @pl.when(i == 0)
    def _():
        carry[...] = jnp.zeros_like(carry)

    g1 = mod_ref[0, 2:3, :]
    sh2 = mod_ref[0, 3:4, :]
    sc2 = mod_ref[0, 4:5, :]

    def out_proj(s):
        rows = pl.ds(s * tm, tm)
        return _dot(yp_ref[rows, :], wop_ref[...]) + _dot(at_ref[rows, :], woa_ref[...])

    def norm_router(s, mix):
        rows = pl.ds(s * tm, tm)
        xmid = _layer_norm(ALPHA * x_ref[rows, :] + g1 * mix, g_ref[...], b_ref[...])
        xmid_ref[rows, :] = xmid
        h = xmid * (1.0 + sc2) + sh2
        _store_rows(hp_ref, _pack2(h[:, :HALF_D], h[:, HALF_D:]), s * tm)
        h_hi = h.astype(BF16)
        h_lo = (h - h_hi.astype(F32)).astype(BF16)
        a = lax.dot_general(wr_ref[...], h_hi, _NT, preferred_element_type=F32)
        b = lax.dot_general(wr_ref[0:N_EXPERTS, :], h_lo, _NT, preferred_element_type=F32)
        return a[:N_EXPERTS] + a[N_EXPERTS:] + b

    nxt = out_proj(0)
    logits = []
    for s in range(n_sub):
        mix = nxt
        if s + 1 < n_sub:
            nxt = out_proj(s + 1)
        logits.append(norm_router(s, mix))
    base = carry[:, 0:1]
    for s in range(n_sub):
        base = base + _route(logits[s], base, rb_ref, tri_ref, idx_ref, wts_ref, rank_ref, s * tm)
    carry[...] = jnp.broadcast_to(base, carry.shape)
    cnt_ref[...] = carry[...].astype(I32)


def _route(logits, base, rb_ref, tri_ref, idx_ref, wts_ref, rank_ref, tok0):
    tm = logits.shape[1]
    cols = pl.ds(tok0, tm)
    score = jax.nn.sigmoid(logits)
    sel = score + rb_ref[...]
    neg = jnp.float32(-jnp.inf)
    sub = lax.broadcasted_iota(I32, (GROUP_SIZE, tm), 0)
    grp_sel, grp_score = [], []
    for g in range(N_GROUPS):
        sg = sel[g * GROUP_SIZE:(g + 1) * GROUP_SIZE, :]
        m1 = jnp.max(sg, axis=0, keepdims=True)
        i1 = jnp.min(jnp.where(sg == m1, sub, GROUP_SIZE), axis=0, keepdims=True)
        m2 = jnp.max(jnp.where(sub == i1, neg, sg), axis=0, keepdims=True)
        grp_sel.append(sg)
        grp_score.append(m1 + m2)
    masked = []
    for g in range(N_GROUPS):
        ahead = jnp.zeros((1, tm), I32)
        for o in range(N_GROUPS):
            if o == g:
                continue
            beats = (grp_score[o] > grp_score[g]) | ((grp_score[o] == grp_score[g]) & (o < g))
            ahead = ahead + beats.astype(I32)
        masked.append(jnp.where(ahead < TOPK_GROUPS, grp_sel[g], neg))
    cur = jnp.concatenate(masked, axis=0)
    row = lax.broadcasted_iota(I32, (N_EXPERTS, tm), 0)
    member = jnp.zeros((N_EXPERTS, tm), F32)
    picks, wsel = [], []
    for k in range(TOP_K):
        mx = jnp.max(cur, axis=0, keepdims=True)
        ei = jnp.min(jnp.where(cur == mx, row, N_EXPERTS), axis=0, keepdims=True)
        hit = row == ei
        picks.append(ei)
        wsel.append(jnp.sum(jnp.where(hit, score, 0.0), axis=0, keepdims=True))
        cur = jnp.where(hit, neg, cur)
        member = jnp.where(hit, 1.0, member)
    tot = wsel[0]
    for k in range(1, TOP_K):
        tot = tot + wsel[k]
    before = _dot(member.astype(BF16), tri_ref[...]) + base
    for k in range(TOP_K):
        idx_ref[k:k + 1, cols] = picks[k]
        rk = jnp.sum(jnp.where(row == picks[k], before, 0.0), axis=0, keepdims=True)
        rank_ref[k:k + 1, cols] = rk.astype(I32)
    wrows = [wsel[k] / tot * ROUTED_SCALE for k in range(TOP_K)]
    wmat = jnp.concatenate(wrows + [jnp.zeros((LANE - TOP_K, tm), F32)], axis=0)
    wts_ref[cols, :] = wmat.T
    return jnp.sum(member, axis=1, keepdims=True)


def _mix_call(x2, ypool, attn, wop, woa, mod3, g, b, wr_t, rb, tri, tiles_per_seq):
    T = x2.shape[0]
    tm = TM_MIX
    row = lambda i: (i, 0)
    fixed = lambda i: (0, 0)
    tok = lambda i: (0, i)
    return pl.pallas_call(
        _mix_kernel,
        grid=(T // tm,),
        in_specs=[pl.BlockSpec((tm, D_MODEL), row),
                  pl.BlockSpec((tm, POOL_WIDTH), row),
                  pl.BlockSpec((tm, ATTN_PAD), row),
                  pl.BlockSpec((POOL_WIDTH, D_MODEL), fixed),
                  pl.BlockSpec((ATTN_PAD, D_MODEL), fixed),
                  pl.BlockSpec((1, 6, D_MODEL), lambda i: (i // tiles_per_seq, 0, 0)),
                  pl.BlockSpec((1, D_MODEL), fixed),
                  pl.BlockSpec((1, D_MODEL), fixed),
                  pl.BlockSpec((2 * N_EXPERTS, D_MODEL), fixed),
                  pl.BlockSpec((N_EXPERTS, 1), fixed),
                  pl.BlockSpec((MIX_SUB, MIX_SUB), fixed)],
        out_specs=[pl.BlockSpec((tm, D_MODEL), row),
                   pl.BlockSpec((tm * ROW_SUB, LANE), row),
                   pl.BlockSpec((TOP_K, tm), tok),
                   pl.BlockSpec((tm, LANE), row),
                   pl.BlockSpec((TOP_K, tm), tok),
                   pl.BlockSpec((N_EXPERTS, LANE), fixed)],
        out_shape=[jax.ShapeDtypeStruct((T, D_MODEL), F32),
                   jax.ShapeDtypeStruct((T * ROW_SUB, LANE), I32),
                   jax.ShapeDtypeStruct((TOP_K, T), I32),
                   jax.ShapeDtypeStruct((T, LANE), F32),
                   jax.ShapeDtypeStruct((TOP_K, T), I32),
                   jax.ShapeDtypeStruct((N_EXPERTS, LANE), I32)],
        scratch_shapes=[pltpu.VMEM((N_EXPERTS, LANE), F32)],
        compiler_params=pltpu.CompilerParams(dimension_semantics=("arbitrary",)),
        name="mix",
    )(x2, ypool, attn, wop, woa, mod3, g, b, wr_t, rb, tri)


def _slab(r):
    return pl.ds(pl.multiple_of(r * ROW_SUB, ROW_SUB), ROW_SUB)


def _dest_kernel(ps_ref, idx_ref, rank_ref, o_ref):
    idx = idx_ref[...]
    d = rank_ref[...]
    for e in range(N_EXPERTS):
        d = d + jnp.where(idx == e, ps_ref[e], 0)
    tm = o_ref.shape[1]
    for j in range(idx.shape[1] // tm):
        o_ref[j * TOP_K:(j + 1) * TOP_K, :] = d[:, j * tm:(j + 1) * tm]


def _dest_call(pad_start, idx_t, rank_t):
    T = idx_t.shape[1]
    tm = TM_ROUTE
    td = 2048 if T % 2048 == 0 else tm
    per = td // tm
    grid_spec = pltpu.PrefetchScalarGridSpec(
        num_scalar_prefetch=1,
        grid=(T // td,),
        in_specs=[pl.BlockSpec((TOP_K, td), lambda i, ps: (0, i)),
                  pl.BlockSpec((TOP_K, td), lambda i, ps: (0, i))],
        out_specs=pl.BlockSpec((per * TOP_K, tm), lambda i, ps: (i, 0)),
    )
    return pl.pallas_call(
        _dest_kernel,
        grid_spec=grid_spec,
        out_shape=jax.ShapeDtypeStruct((T // tm * TOP_K, tm), I32),
        name="dest",
    )(pad_start, idx_t, rank_t)


def _dispatch_kernel(pst_ref, plen_ref, nreal_ref, dest_hbm, hp_ref, xs_hbm, dst0, dst1, zbuf, isem, ssem, zsem):
    tm = hp_ref.shape[0] // ROW_SUB
    bm = zbuf.shape[0] // ROW_SUB
    n_idx = TOP_K * tm
    nb = xs_hbm.shape[0] // (bm * ROW_SUB)
    i = pl.program_id(0)
    nt = pl.num_programs(0)
    dst = (dst0, dst1)

    def idx_copy(tile, s):
        off = pl.multiple_of(tile * n_idx, n_idx)
        return pltpu.make_async_copy(dest_hbm.at[pl.ds(off, n_idx)], dst[s], isem.at[s])

    def pad_fill(e):
        n = plen_ref[e] * ROW_SUB
        start = pl.multiple_of(pst_ref[e] * ROW_SUB, ROW_SUB)
        return pltpu.make_async_copy(zbuf.at[pl.ds(0, n)], xs_hbm.at[pl.ds(start, n)], zsem)

    def dead_fill(j):
        return pltpu.make_async_copy(zbuf, xs_hbm.at[pl.ds(j * bm * ROW_SUB, bm * ROW_SUB)], zsem)

    def fills(act):
        for e in range(N_EXPERTS):
            pl.when(plen_ref[e] > 0)(functools.partial(lambda e: act(pad_fill(e)), e))
        for j in range(nb - N_EXPERTS, nb):
            pl.when(j >= nreal_ref[0])(functools.partial(lambda j: act(dead_fill(j)), j))

    @pl.when(i == 0)
    def _():
        zbuf[...] = jnp.zeros_like(zbuf)
        idx_copy(0, 0).start()
        idx_copy(0, 0).wait()
        fills(lambda cp: cp.start())

    def scatter(s):
        @pl.when(i + 1 < nt)
        def _():
            idx_copy(i + 1, 1 - s).start()

        def body(rb, c):
            for rr in range(SUBLANE):
                r = rb * SUBLANE + rr
                for k in range(TOP_K):
                    pltpu.make_async_copy(hp_ref.at[_slab(r)], xs_hbm.at[_slab(dst[s][k * tm + r])],
                                          ssem).start(priority=k % 2)
            return c
        lax.fori_loop(0, tm // SUBLANE, body, 0)
        for k in range(TOP_K):
            pltpu.make_async_copy(hp_ref, xs_hbm.at[pl.ds(0, tm * ROW_SUB)], ssem).wait()

        @pl.when(i + 1 < nt)
        def _():
            idx_copy(i + 1, 1 - s).wait()

    for s in range(2):
        pl.when(i % 2 == s)(functools.partial(scatter, s))

    @pl.when(i == 0)
    def _():
        fills(lambda cp: cp.wait())


def _dispatch_call(pad_row_start, pad_len, n_real, dest, hp, nb):
    tm = TM_ROUTE
    bm = BM_MOE
    T = hp.shape[0] // ROW_SUB
    grid_spec = pltpu.PrefetchScalarGridSpec(
        num_scalar_prefetch=3,
        grid=(T // tm,),
        in_specs=[pl.BlockSpec(memory_space=pl.ANY),
                  pl.BlockSpec((tm * ROW_SUB, LANE), lambda i, a, b, c: (i, 0))],
        out_specs=pl.BlockSpec(memory_space=pl.ANY),
        scratch_shapes=[pltpu.SMEM((TOP_K * tm,), I32),
                        pltpu.SMEM((TOP_K * tm,), I32),
                        pltpu.VMEM((bm * ROW_SUB, LANE), I32),
                        pltpu.SemaphoreType.DMA((2,)),
                        pltpu.SemaphoreType.DMA,
                        pltpu.SemaphoreType.DMA],
    )
    return pl.pallas_call(
        _dispatch_kernel,
        grid_spec=grid_spec,
        out_shape=jax.ShapeDtypeStruct((nb * bm * ROW_SUB, LANE), I32),
        compiler_params=pltpu.CompilerParams(dimension_semantics=("arbitrary",)),
        name="dispatch",
    )(pad_row_start, pad_len, n_real, dest, hp)


def _moe_kernel(be_ref, nreal_ref, x_ref, wg_ref, wu_ref, wd_ref, y_ref, wgb, wub, wdb):
    bm = x_ref.shape[0] // ROW_SUB
    i = pl.program_id(0)
    n = nreal_ref[0]

    @pl.when(i < n)
    def _():
        e_changed = (i == 0) | (be_ref[i] != be_ref[jnp.maximum(i - 1, 0)])

        @pl.when(e_changed)
        def _():
            wgb[...] = wg_ref[0].astype(BF16)
            wub[...] = wu_ref[0].astype(BF16)
            wdb[...] = wd_ref[0].astype(BF16)

        cm = bm // MOE_ROW_CHUNKS

        def up(c):
            halves = [_unpack2(w) for w in _load_rows(x_ref, cm, c * cm)]
            xb = jnp.concatenate([lo.astype(BF16) for lo, _ in halves] + [hi.astype(BF16) for _, hi in halves],
                                 axis=1)
            return (_silu(_dot(xb, wgb[...])) * _dot(xb, wub[...])).astype(BF16)

        nxt = up(0)
        for c in range(MOE_ROW_CHUNKS):
            a = nxt
            if c + 1 < MOE_ROW_CHUNKS:
                nxt = up(c + 1)
            y = _dot(a, wdb[...])
            _store_rows(y_ref, _pack2(y[:, :HALF_D], y[:, HALF_D:]), c * cm)

    @pl.when(i >= n)
    def _():
        y_ref[...] = jnp.zeros_like(y_ref)


def _moe_call(block_expert, n_real, xs, w_gate, w_up, w_down):
    bm = BM_MOE
    nb = block_expert.shape[0]
    wmap = lambda i, be, nr: (be[i], 0, 0)
    rows = pl.BlockSpec((bm * ROW_SUB, LANE), lambda i, be, nr: (i, 0))
    grid_spec = pltpu.PrefetchScalarGridSpec(
        num_scalar_prefetch=2,
        grid=(nb,),
        in_specs=[rows,
                  pl.BlockSpec((1, D_MODEL, EXPERT_FF), wmap),
                  pl.BlockSpec((1, D_MODEL, EXPERT_FF), wmap),
                  pl.BlockSpec((1, EXPERT_FF, D_MODEL), wmap)],
        out_specs=rows,
        scratch_shapes=[pltpu.VMEM((D_MODEL, EXPERT_FF), BF16),
                        pltpu.VMEM((D_MODEL, EXPERT_FF), BF16),
                        pltpu.VMEM((EXPERT_FF, D_MODEL), BF16)],
    )
    return pl.pallas_call(
        _moe_kernel,
        grid_spec=grid_spec,
        out_shape=jax.ShapeDtypeStruct(xs.shape, I32),
        compiler_params=pltpu.CompilerParams(dimension_semantics=("arbitrary",)),
        name="moe",
    )(block_expert, n_real, xs, w_gate, w_up, w_down)


def _combine_kernel(dest_hbm, ys_hbm, wc_ref, xmid_ref, mod_ref, wsg_ref, wsu_ref, wsd_ref, g_ref, b_ref,
                    o_ref, dst0, dst1, rbuf, isem, gsem):
    tm = xmid_ref.shape[0]
    n_idx = TOP_K * tm
    i = pl.program_id(0)
    nt = pl.num_programs(0)
    dst = (dst0, dst1)

    def idx_copy(tile, s):
        off = pl.multiple_of(tile * n_idx, n_idx)
        return pltpu.make_async_copy(dest_hbm.at[pl.ds(off, n_idx)], dst[s], isem.at[s])

    def start_gather(s):
        def body(rb, c):
            for rr in range(GATHER_UNROLL):
                r = rb * GATHER_UNROLL + rr
                for k in range(TOP_K):
                    pltpu.make_async_copy(ys_hbm.at[_slab(dst[s][k * tm + r])], rbuf.at[s, k, _slab(r)],
                                          gsem.at[s]).start(priority=k % 2)
            return c
        lax.fori_loop(0, tm // GATHER_UNROLL, body, 0)

    def wait_gather(s):
        for k in range(TOP_K):
            pltpu.make_async_copy(ys_hbm.at[pl.ds(0, tm * ROW_SUB)], rbuf.at[s, k], gsem.at[s]).wait()

    @pl.when(i == 0)
    def _():
        idx_copy(0, 0).start()
        idx_copy(0, 0).wait()
        start_gather(0)

        @pl.when(nt > 1)
        def _():
            idx_copy(1, 1).start()

    def prefetch(s):
        @pl.when(i + 2 < nt)
        def _():
            idx_copy(i + 2, s).start()

        @pl.when(i + 1 < nt)
        def _():
            idx_copy(i + 1, 1 - s).wait()
            start_gather(1 - s)

        wait_gather(s)

    for s in range(2):
        pl.when(i % 2 == s)(functools.partial(prefetch, s))

    wc = wc_ref[...]
    slot = i % 2
    acc = None
    for k in range(TOP_K):
        parts = [_unpack2(w) for w in _load_rows(rbuf.at[slot, k], tm)]
        parts = [lo for lo, _ in parts] + [hi for _, hi in parts]
        wk = wc[:, k:k + 1]
        acc = [wk * p for p in parts] if acc is None else [a + wk * p for a, p in zip(acc, parts)]
    routed = jnp.concatenate(acc, axis=1)
    xmid = xmid_ref[...]
    sh2 = mod_ref[0, 3:4, :]
    sc2 = mod_ref[0, 4:5, :]
    g2 = mod_ref[0, 5:6, :]
    hb = (xmid * (1.0 + sc2) + sh2).astype(BF16)
    a = _silu(_dot(hb, wsg_ref[...])) * _dot(hb, wsu_ref[...])
    shared = _dot(a.astype(BF16), wsd_ref[...])
    o_ref[...] = _layer_norm(ALPHA * xmid + g2 * (routed + shared), g_ref[...], b_ref[...])


def _combine_call(dest, ys, wcol, xmid, mod3, wsg, wsu, wsd, g, b, tiles_per_seq):
    T = xmid.shape[0]
    tm = TM_ROUTE
    fixed = lambda i: (0, 0)
    row = lambda i: (i, 0)
    return pl.pallas_call(
        _combine_kernel,
        grid=(T // tm,),
        in_specs=[pl.BlockSpec(memory_space=pl.ANY),
                  pl.BlockSpec(memory_space=pl.ANY),
                  pl.BlockSpec((tm, LANE), row),
                  pl.BlockSpec((tm, D_MODEL), row),
                  pl.BlockSpec((1, 6, D_MODEL), lambda i: (i // tiles_per_seq, 0, 0)),
                  pl.BlockSpec((D_MODEL, SHARED_FF), fixed),
                  pl.BlockSpec((D_MODEL, SHARED_FF), fixed),
                  pl.BlockSpec((SHARED_FF, D_MODEL), fixed),
                  pl.BlockSpec((1, D_MODEL), fixed),
                  pl.BlockSpec((1, D_MODEL), fixed)],
        out_specs=pl.BlockSpec((tm, D_MODEL), row),
        out_shape=jax.ShapeDtypeStruct((T, D_MODEL), F32),
        scratch_shapes=[pltpu.SMEM((TOP_K * tm,), I32),
                        pltpu.SMEM((TOP_K * tm,), I32),
                        pltpu.VMEM((2, TOP_K, tm * ROW_SUB, LANE), I32),
                        pltpu.SemaphoreType.DMA((2,)),
                        pltpu.SemaphoreType.DMA((2,))],
        compiler_params=pltpu.CompilerParams(dimension_semantics=("arbitrary",)),
        name="combine",
    )(dest, ys, wcol, xmid, mod3, wsg, wsu, wsd, g, b)


def _group(a, b):
    ref = a if a is not None else b
    z = jnp.zeros(ref.shape[:-1], ref.dtype)
    a0, a1 = (a[..., 0], a[..., 1]) if a is not None else (z, z)
    b0, b1 = (b[..., 0], b[..., 1]) if b is not None else (z, z)
    return jnp.concatenate([a0, b0, a1, b1], axis=-1)


def _q_groups(per_head):
    return jnp.concatenate([_group(per_head(3 * (2 * p) + j), per_head(3 * (2 * p + 1) + j))
                            for p in range(N_KV_HEADS // 2) for j in range(GQA_GROUP)], axis=-1)


def _k_groups(per_head):
    return jnp.concatenate([_group(per_head(h), None) if h % 2 == 0 else _group(None, per_head(h))
                            for h in range(N_KV_HEADS)], axis=-1)


def _rope_pairs(seq):
    inv_freq = ROPE_THETA ** (-jnp.arange(0, HALF, 2, dtype=F32) / HALF)
    pos = jnp.arange(seq)
    rowp = (pos // GRID_W).astype(F32)
    colp = (pos % GRID_W).astype(F32)
    ang = jnp.concatenate([rowp[:, None] * inv_freq, colp[:, None] * inv_freq], axis=-1)
    cos, sin = jnp.cos(ang), jnp.sin(ang)
    return jnp.stack([cos, cos], axis=-1), jnp.stack([-sin, sin], axis=-1)


def kernel(x, c, ctx, c_ctx, w_mod, b_mod, w_in, q_norm, k_norm, pool_w, pool_scale, w_out, ln1_g, ln1_b,
           w_router, router_bias, w_gate, w_up, w_down, ws_gate, ws_up, ws_down, ln2_g, ln2_b):
    B, S, D = x.shape
    C = ctx.shape[1]
    T = B * S
    assert D == D_MODEL and w_mod.shape[0] == DEPTH and B + 1 <= MOD_ROWS
    assert S % TM_INPROJ == 0 and S % TQ_ATTN == 0 and S % TM_MIX == 0 and S % TM_ROUTE == 0
    assert S % GRID_W == 0 and C % SUBLANE == 0

    cc = jnp.concatenate([c, c_ctx[None, :], jnp.zeros((MOD_ROWS - B - 1, D), F32)], axis=0)
    mod3 = _mod_call(cc, w_mod[0], b_mod[0][None, :]).reshape(MOD_ROWS, 6, D)

    w = w_in[0]
    o1, o2, o3 = POOL_WIDTH, POOL_WIDTH + ATTN_WIDTH, POOL_WIDTH + ATTN_WIDTH + KV_WIDTH
    wq4 = w[:, o1:o2].reshape(D, N_HEADS, HALF, 2)
    wk4 = w[:, o2:o3].reshape(D, N_KV_HEADS, HALF, 2)
    wq = _q_groups(lambda h: wq4[:, h])
    wk = _k_groups(lambda h: wk4[:, h])
    w_all = jnp.concatenate([w[:, :o1], wq, wk, w[:, o3:]], axis=1).astype(BF16)
    w_kv = jnp.concatenate([wk, w[:, o3:]], axis=1).astype(BF16)
    qg4, kg4 = q_norm[0].reshape(1, HALF, 2), k_norm[0].reshape(1, HALF, 2)
    qg = _group(qg4, qg4)
    kg = _k_groups(lambda h: kg4)
    cos_p, sin_p = _rope_pairs(S)
    qscale = HEAD_DIM ** -0.5 * LOG2_E
    cq, sq = _group(cos_p, cos_p) * qscale, _group(sin_p, sin_p) * qscale
    ck = jnp.concatenate([_group(cos_p, None), _group(None, cos_p)], axis=-1)
    sk = jnp.concatenate([_group(sin_p, None), _group(None, sin_p)], axis=-1)
    slot = (jnp.arange(LANE) // HALF) % 2
    same = (slot[:, None] == slot[None, :]).astype(F32) * (1.0 / HEAD_DIM)
    gg = jnp.concatenate([same, same], axis=0).astype(BF16)
    u, q, k_l, v_l = _inproj_call(x, mod3, w_all, gg, qg, kg, cq, sq, ck, sk)
    k_c, v_c = _ctx_kv_call(ctx, mod3, w_kv, kg, B)

    k_all = jnp.concatenate([k_c, k_l], axis=1)
    v_all = jnp.concatenate([v_c, v_l], axis=1).reshape(B, C + S, N_KV_HEADS, HEAD_DIM)
    vt = jnp.transpose(v_all, (0, 2, 3, 1))
    ones = jnp.ones((B, N_KV_HEADS, 1, C + S), BF16)
    pad = jnp.zeros((B, N_KV_HEADS, VT_ROWS - HEAD_DIM - 1, C + S), BF16)
    attn = _attn_call(q, k_all, jnp.concatenate([vt, ones, pad], axis=2))

    bd = jax.scipy.linalg.block_diag(*[pool_w[0, g] for g in range(len(POOL_WINDOWS))]).astype(BF16)
    ypool = _pool_call(u, bd, pool_scale[0][None, :])

    wo = w_out[0]
    wop = wo[:POOL_WIDTH].astype(BF16)
    woa = wo[POOL_WIDTH:].reshape(N_KV_HEADS, GQA_GROUP * HEAD_DIM, D)
    woa = jnp.pad(woa, ((0, 0), (0, 2 * LANE - GQA_GROUP * HEAD_DIM), (0, 0))).reshape(ATTN_PAD, D).astype(BF16)
    tri = (jnp.arange(MIX_SUB)[:, None] < jnp.arange(MIX_SUB)[None, :]).astype(BF16)
    wr = w_router[0].T
    wr_hi = wr.astype(BF16)
    wr_lo = (wr - wr_hi.astype(F32)).astype(BF16)
    xmid, hp, idx_t, wcol, rank_t, counts = _mix_call(
        x.reshape(T, D), ypool.reshape(T, POOL_WIDTH), attn.reshape(T, ATTN_PAD), wop, woa, mod3,
        ln1_g[0][None, :], ln1_b[0][None, :], jnp.concatenate([wr_hi, wr_lo], axis=0),
        router_bias[0][:, None], tri, S // TM_MIX)

    bm = BM_MOE
    counts = counts[:, 0]
    padded = ((counts + bm - 1) // bm) * bm
    pad_end = jnp.cumsum(padded)
    pad_start = pad_end - padded
    nb = T * TOP_K // bm + N_EXPERTS
    n_real = (pad_end[-1] // bm).astype(I32).reshape(1)
    blk_row = jnp.arange(nb, dtype=I32) * bm
    block_expert = jnp.minimum(jnp.sum((pad_end[None, :] <= blk_row[:, None]).astype(I32), axis=1), N_EXPERTS - 1)

    dest = _dest_call(pad_start.astype(I32), idx_t, rank_t).reshape(-1)
    xs = _dispatch_call((pad_start + counts).astype(I32), (padded - counts).astype(I32), n_real, dest, hp, nb)
    ys = _moe_call(block_expert.astype(I32), n_real, xs, w_gate[0], w_up[0], w_down[0])
    out = _combine_call(dest, ys, wcol, xmid, mod3, ws_gate[0].astype(BF16), ws_up[0].astype(BF16),
                        ws_down[0].astype(BF16), ln2_g[0][None, :], ln2_b[0][None, :], S // TM_ROUTE)
    return out.reshape(B, S, D)
```

```python
import functools

import jax
import jax.numpy as jnp
from jax import lax
from jax.experimental import pallas as pl
from jax.experimental.pallas import tpu as pltpu

F32 = jnp.float32
BF16 = jnp.bfloat16
I32 = jnp.int32

LANE = 128
SUBLANE = 8

D_MODEL = 1024
GRID_W = 64
POOL_WIDTH = 256
POOL_WINDOWS = (2, 4, 8, 16)
POOL_GROUP = 64
HEAD_DIM = 64
HALF = HEAD_DIM // 2
N_HEADS = 12
N_KV_HEADS = 4
GQA_GROUP = N_HEADS // N_KV_HEADS
ATTN_WIDTH = N_HEADS * HEAD_DIM
KV_WIDTH = N_KV_HEADS * HEAD_DIM
ROPE_THETA = 10000.0
N_EXPERTS = 64
TOP_K = 8
N_GROUPS = 8
GROUP_SIZE = N_EXPERTS // N_GROUPS
TOPK_GROUPS = 4
EXPERT_FF = 256
SHARED_FF = 256
ROUTED_SCALE = 2.5
DEPTH = 1
ALPHA = (2.0 * DEPTH) ** 0.25
LN_EPS = 1e-5
RMS_EPS = 1e-6

MOD_ROWS = 24
HALF_D = D_MODEL // 2
ROW_SUB = HALF_D // LANE
Q_COLS = N_HEADS // 2 * LANE
K_COLS = N_KV_HEADS * LANE
ATTN_PAD = N_KV_HEADS * 2 * LANE
VT_ROWS = HEAD_DIM + 16
LOG2_E = 1.4426950408889634

TM_INPROJ = 512
TQ_ATTN = 256
TM_MIX = 512
MIX_SUB = 256
BM_MOE = 1024
MOE_ROW_CHUNKS = 4
TM_ROUTE = 256

_NT = (((1,), (1,)), ((), ()))


def _dot(a, b):
    return jnp.dot(a, b, preferred_element_type=F32)


def _pack2(lo, hi):
    lo_bits = lax.bitcast_convert_type(lo.astype(BF16).astype(F32), I32)
    hi_bits = lax.bitcast_convert_type(hi.astype(BF16).astype(F32), I32)
    return lax.shift_right_logical(lo_bits, 16) | (hi_bits & jnp.int32(-65536))


def _unpack2(w):
    lo = lax.bitcast_convert_type(lax.shift_left(w, 16), F32)
    hi = lax.bitcast_convert_type(w & jnp.int32(-65536), F32)
    return lo, hi


def _store_rows(ref, packed, row0=0):
    n = packed.shape[0]
    for s in range(ROW_SUB):
        ref[pl.ds(row0 * ROW_SUB + s, n, stride=ROW_SUB), :] = packed[:, s * LANE:(s + 1) * LANE]


def _load_rows(ref, n, row0=0):
    return [ref[pl.ds(row0 * ROW_SUB + s, n, stride=ROW_SUB), :] for s in range(ROW_SUB)]


def _silu(x):
    return x * jax.nn.sigmoid(x)


def _layer_norm(r, g, b):
    mu = jnp.mean(r, axis=-1, keepdims=True)
    d = r - mu
    var = jnp.mean(d * d, axis=-1, keepdims=True)
    return d * lax.rsqrt(var + LN_EPS) * g + b


def _mod_kernel(c_ref, w_ref, b_ref, o_ref):
    a = _silu(c_ref[...])
    o_ref[...] = jnp.dot(a, w_ref[...], precision=lax.Precision.HIGHEST,
                         preferred_element_type=F32) + b_ref[...]


def _mod_call(cc, w_mod, b_mod):
    n = w_mod.shape[1]
    tn = 512
    return pl.pallas_call(
        _mod_kernel,
        grid=(n // tn,),
        in_specs=[pl.BlockSpec((MOD_ROWS, D_MODEL), lambda j: (0, 0)),
                  pl.BlockSpec((D_MODEL, tn), lambda j: (0, j)),
                  pl.BlockSpec((1, tn), lambda j: (0, j))],
        out_specs=pl.BlockSpec((MOD_ROWS, tn), lambda j: (0, j)),
        out_shape=jax.ShapeDtypeStruct((MOD_ROWS, n), F32),
        name="mod",
    )(cc, w_mod, b_mod)


def _norm_head(seg, g):
    ms = jnp.sum(seg * seg, axis=-1, keepdims=True) * (1.0 / HEAD_DIM)
    return seg * lax.rsqrt(ms + RMS_EPS) * g


def _norm_rope(seg, gg, g, c, s):
    sq = seg * seg
    hi = sq.astype(BF16)
    lo = (sq - hi.astype(F32)).astype(BF16)
    ms = _dot(jnp.concatenate([hi, lo], axis=1), gg)
    xn = seg * lax.rsqrt(ms + RMS_EPS) * g
    return xn * c + pltpu.roll(xn, LANE // 2, axis=1) * s


def _inproj_kernel(x_ref, mod_ref, w_ref, gg_ref, qg_ref, kg_ref, cq_ref, sq_ref, ck_ref, sk_ref,
                   u_ref, q_ref, k_ref, v_ref):
    sh = mod_ref[0, 0:1, :]
    sc = mod_ref[0, 1:2, :]
    xm = (x_ref[0] * (1.0 + sc) + sh).astype(BF16)
    gg, qg = gg_ref[...], qg_ref[...]
    cq, sq = cq_ref[...], sq_ref[...]
    n_qp = Q_COLS // (2 * LANE)
    n_kp = K_COLS // (2 * LANE)
    n_chunks = 1 + n_qp + n_kp + 1

    def chunk(c):
        return _dot(xm, w_ref[:, c * 2 * LANE:(c + 1) * 2 * LANE])

    nxt = chunk(0)
    for c in range(n_chunks):
        p = nxt
        if c + 1 < n_chunks:
            nxt = chunk(c + 1)
        if c == 0:
            u_ref[0] = p
        elif c <= n_qp:
            for j in range(2):
                grp = (c - 1) * 2 + j
                q = _norm_rope(p[:, j * LANE:(j + 1) * LANE], gg, qg, cq, sq)
                q_ref[0, :, grp * LANE:(grp + 1) * LANE] = q.astype(BF16)
        elif c <= n_qp + n_kp:
            for j in range(2):
                h = (c - 1 - n_qp) * 2 + j
                k = _norm_rope(p[:, j * LANE:(j + 1) * LANE], gg, kg_ref[:, h * LANE:(h + 1) * LANE],
                               ck_ref[:, j * LANE:(j + 1) * LANE], sk_ref[:, j * LANE:(j + 1) * LANE])
                k_ref[0, :, h * LANE:(h + 1) * LANE] = k.astype(BF16)
        else:
            v_ref[0] = p.astype(BF16)


def _inproj_call(x, mod3, w_all, gg, qg, kg, cq, sq, ck, sk):
    B, S, _ = x.shape
    tm = TM_INPROJ
    ncol = w_all.shape[1]
    tab = pl.BlockSpec((tm, LANE), lambda b, i: (i, 0))
    tab2 = pl.BlockSpec((tm, 2 * LANE), lambda b, i: (i, 0))
    vec = pl.BlockSpec((1, LANE), lambda b, i: (0, 0))
    return pl.pallas_call(
        _inproj_kernel,
        grid=(B, S // tm),
        in_specs=[pl.BlockSpec((1, tm, D_MODEL), lambda b, i: (b, i, 0)),
                  pl.BlockSpec((1, 6, D_MODEL), lambda b, i: (b, 0, 0)),
                  pl.BlockSpec((D_MODEL, ncol), lambda b, i: (0, 0)),
                  pl.BlockSpec((2 * LANE, LANE), lambda b, i: (0, 0)),
                  vec, pl.BlockSpec((1, K_COLS), lambda b, i: (0, 0)), tab, tab, tab2, tab2],
        out_specs=[pl.BlockSpec((1, tm, POOL_WIDTH), lambda b, i: (b, i, 0)),
                   pl.BlockSpec((1, tm, Q_COLS), lambda b, i: (b, i, 0)),
                   pl.BlockSpec((1, tm, K_COLS), lambda b, i: (b, i, 0)),
                   pl.BlockSpec((1, tm, KV_WIDTH), lambda b, i: (b, i, 0))],
        out_shape=[jax.ShapeDtypeStruct((B, S, POOL_WIDTH), F32),
                   jax.ShapeDtypeStruct((B, S, Q_COLS), BF16),
                   jax.ShapeDtypeStruct((B, S, K_COLS), BF16),
                   jax.ShapeDtypeStruct((B, S, KV_WIDTH), BF16)],
        name="inproj",
    )(x, mod3, w_all, gg, qg, kg, cq, sq, ck, sk)


def _ctx_kv_kernel(x_ref, mod_ref, w_ref, kg_ref, k_ref, v_ref):
    sh = mod_ref[0, 0:1, :]
    sc = mod_ref[0, 1:2, :]
    xm = (x_ref[0] * (1.0 + sc) + sh).astype(BF16)
    for pair in range(N_KV_HEADS // 2):
        p = _dot(xm, w_ref[:, pair * 2 * LANE:(pair + 1) * 2 * LANE])
        for j in range(2):
            h = pair * 2 + j
            kn = _norm_head(p[:, j * LANE:(j + 1) * LANE], kg_ref[:, h * LANE:(h + 1) * LANE])
            k_ref[0, :, h * LANE:(h + 1) * LANE] = kn.astype(BF16)
    v_ref[0] = _dot(xm, w_ref[:, K_COLS:K_COLS + KV_WIDTH]).astype(BF16)


def _ctx_kv_call(ctx, mod3, w_kv, kg, ctx_row):
    B, C, _ = ctx.shape
    return pl.pallas_call(
        _ctx_kv_kernel,
        grid=(B,),
        in_specs=[pl.BlockSpec((1, C, D_MODEL), lambda b: (b, 0, 0)),
                  pl.BlockSpec((1, 6, D_MODEL), lambda b: (ctx_row, 0, 0)),
                  pl.BlockSpec((D_MODEL, K_COLS + KV_WIDTH), lambda b: (0, 0)),
                  pl.BlockSpec((1, K_COLS), lambda b: (0, 0))],
        out_specs=[pl.BlockSpec((1, C, K_COLS), lambda b: (b, 0, 0)),
                   pl.BlockSpec((1, C, KV_WIDTH), lambda b: (b, 0, 0))],
        out_shape=[jax.ShapeDtypeStruct((B, C, K_COLS), BF16),
                   jax.ShapeDtypeStruct((B, C, KV_WIDTH), BF16)],
        name="ctx_kv",
    )(ctx, mod3, w_kv, kg)


POOL_PAD = 8


def _pool_kernel(u_ref, bd_ref, ps_ref, y_ref):
    S = u_ref.shape[1]
    n = S + 2 * POOL_PAD
    t = lax.broadcasted_iota(I32, (S, LANE), 0)
    lane = lax.broadcasted_iota(I32, (S, LANE), 1)
    zpad = jnp.zeros((POOL_PAD, LANE), F32)
    for half in range(POOL_WIDTH // LANE):
        u = u_ref[0, :, half * LANE:(half + 1) * LANE]
        ue = jnp.concatenate([zpad, u, zpad], axis=0)
        fwd = {1: ue}
        w = 1
        while w < POOL_WINDOWS[2 * half + 1]:
            fwd[2 * w] = fwd[w] + pltpu.roll(fwd[w], n - w, axis=0)
            w *= 2
        ds = []
        for win in POOL_WINDOWS[2 * half: 2 * half + 2]:
            hw = win // 2
            centred = pltpu.roll(fwd[win], hw, axis=0)[POOL_PAD:POOL_PAD + S]
            cnt = (jnp.minimum(t + hw, S) - jnp.maximum(t - hw, 0)).astype(F32)
            ds.append(centred / cnt - u)
        d = jnp.where(lane < POOL_GROUP, ds[0], ds[1]).astype(BF16)
        sl = slice(half * LANE, (half + 1) * LANE)
        y = _dot(d, bd_ref[sl, sl]) * ps_ref[:, sl]
        y_ref[0, :, sl] = y.astype(BF16)


def _pool_call(u, bd, ps):
    B, S, _ = u.shape
    return pl.pallas_call(
        _pool_kernel,
        grid=(B,),
        in_specs=[pl.BlockSpec((1, S, POOL_WIDTH), lambda b: (b, 0, 0)),
                  pl.BlockSpec((POOL_WIDTH, POOL_WIDTH), lambda b: (0, 0)),
                  pl.BlockSpec((1, POOL_WIDTH), lambda b: (0, 0))],
        out_specs=pl.BlockSpec((1, S, POOL_WIDTH), lambda b: (b, 0, 0)),
        out_shape=jax.ShapeDtypeStruct((B, S, POOL_WIDTH), BF16),
        name="pool",
    )(u, bd, ps)


ROW_REDUCE_WAYS = 16


def _reduce_rows(op, x):
    rows, lanes = x.shape
    if rows % (ROW_REDUCE_WAYS * SUBLANE) == 0:
        x = op(x.reshape(ROW_REDUCE_WAYS, rows // ROW_REDUCE_WAYS, lanes), axis=0)
    return op(x, axis=0, keepdims=True)


def _attn_kernel(q_ref, k_ref, vt_ref, o_ref, s_buf, m_buf):
    t = pl.program_id(0)
    tq = q_ref.shape[1]

    @pl.when(t == 0)
    def _():
        s_buf[1] = jnp.zeros_like(s_buf[1])
        m_buf[1] = jnp.zeros_like(m_buf[1])

    def step(slot):
        k = k_ref[0]
        vt = vt_ref[0, 0]
        outs = []
        for j in range(GQA_GROUP):
            s = lax.dot_general(k, q_ref[0, :, j * LANE:(j + 1) * LANE], _NT,
                                preferred_element_type=F32)
            s_buf[slot, j] = s
            m_buf[slot, j] = jnp.broadcast_to(_reduce_rows(jnp.max, s), (SUBLANE, tq))
            p = jnp.exp2(s_buf[1 - slot, j] - m_buf[1 - slot, j, 0:1, :])
            ol = _dot(vt, p.astype(BF16))
            outs.append(ol[:HEAD_DIM] / ol[HEAD_DIM:HEAD_DIM + 1])
        outs.append(jnp.zeros((2 * LANE - GQA_GROUP * HEAD_DIM, tq), F32))
        o_ref[0] = jnp.concatenate(outs, axis=0).T.astype(BF16)

    for slot in range(2):
        pl.when(t % 2 == slot)(functools.partial(step, slot))


def _attn_call(q, k_all, vt_all):
    B, S, _ = q.shape
    Lk = k_all.shape[1]
    tq = TQ_ATTN
    nq = S // tq
    n_items = B * N_KV_HEADS * nq

    def item(t):
        return t // (N_KV_HEADS * nq), (t // nq) % N_KV_HEADS, t % nq

    def cur(t):
        return item(jnp.minimum(t, n_items - 1))

    def prev(t):
        return item(jnp.maximum(t - 1, 0))

    return pl.pallas_call(
        _attn_kernel,
        grid=(n_items + 1,),
        in_specs=[pl.BlockSpec((1, tq, GQA_GROUP * LANE), lambda t: (cur(t)[0], cur(t)[2], cur(t)[1] // 2)),
                  pl.BlockSpec((1, Lk, LANE), lambda t: (cur(t)[0], 0, cur(t)[1])),
                  pl.BlockSpec((1, 1, VT_ROWS, Lk), lambda t: (prev(t)[0], prev(t)[1], 0, 0))],
        out_specs=pl.BlockSpec((1, tq, 2 * LANE), lambda t: (prev(t)[0], prev(t)[2], prev(t)[1])),
        out_shape=jax.ShapeDtypeStruct((B, S, ATTN_PAD), BF16),
        scratch_shapes=[pltpu.VMEM((2, GQA_GROUP, Lk, tq), F32),
                        pltpu.VMEM((2, GQA_GROUP, SUBLANE, tq), F32)],
        compiler_params=pltpu.CompilerParams(dimension_semantics=("arbitrary",)),
        name="attn",
    )(q, k_all, vt_all)


def _mix_kernel(x_ref, yp_ref, at_ref, wop_ref, woa_ref, mod_ref, g_ref, b_ref, wr_ref, rb_ref, tri_ref,
                xmid_ref, hp_ref, idx_ref, wts_ref, rank_ref, cnt_ref, carry):
    i = pl.program_id(0)
    tm = tri_ref.shape[0]
    n_sub = x_ref.shape[0] // tm

    @pl.when(i == 0)
    def _():
        carry[...] = jnp.zeros_like(carry)

    g1 = mod_ref[0, 2:3, :]
    sh2 = mod_ref[0, 3:4, :]
    sc2 = mod_ref[0, 4:5, :]

    def out_proj(s):
        rows = pl.ds(s * tm, tm)
        return _dot(yp_ref[rows, :], wop_ref[...]) + _dot(at_ref[rows, :], woa_ref[...])

    def norm_router(s, mix):
        rows = pl.ds(s * tm, tm)
        xmid = _layer_norm(ALPHA * x_ref[rows, :] + g1 * mix, g_ref[...], b_ref[...])
        xmid_ref[rows, :] = xmid
        h = xmid * (1.0 + sc2) + sh2
        _store_rows(hp_ref, _pack2(h[:, :HALF_D], h[:, HALF_D:]), s * tm)
        h_hi = h.astype(BF16)
        h_lo = (h - h_hi.astype(F32)).astype(BF16)
        a = lax.dot_general(wr_ref[...], h_hi, _NT, preferred_element_type=F32)
        b = lax.dot_general(wr_ref[0:N_EXPERTS, :], h_lo, _NT, preferred_element_type=F32)
        return a[:N_EXPERTS] + a[N_EXPERTS:] + b

    nxt = out_proj(0)
    logits = []
    for s in range(n_sub):
        mix = nxt
        if s + 1 < n_sub:
            nxt = out_proj(s + 1)
        logits.append(norm_router(s, mix))
    base = carry[:, 0:1]
    for s in range(n_sub):
        base = base + _route(logits[s], base, rb_ref, tri_ref, idx_ref, wts_ref, rank_ref, s * tm)
    carry[...] = jnp.broadcast_to(base, carry.shape)
    cnt_ref[...] = carry[...].astype(I32)


def _route(logits, base, rb_ref, tri_ref, idx_ref, wts_ref, rank_ref, tok0):
    tm = logits.shape[1]
    cols = pl.ds(tok0, tm)
    score = jax.nn.sigmoid(logits)
    sel = score + rb_ref[...]
    neg = jnp.float32(-jnp.inf)
    sub = lax.broadcasted_iota(I32, (GROUP_SIZE, tm), 0)
    grp_sel, grp_score = [], []
    for g in range(N_GROUPS):
        sg = sel[g * GROUP_SIZE:(g + 1) * GROUP_SIZE, :]
        m1 = jnp.max(sg, axis=0, keepdims=True)
        i1 = jnp.min(jnp.where(sg == m1, sub, GROUP_SIZE), axis=0, keepdims=True)
        m2 = jnp.max(jnp.where(sub == i1, neg, sg), axis=0, keepdims=True)
        grp_sel.append(sg)
        grp_score.append(m1 + m2)
    masked = []
    for g in range(N_GROUPS):
        ahead = jnp.zeros((1, tm), I32)
        for o in range(N_GROUPS):
            if o == g:
                continue
            beats = (grp_score[o] > grp_score[g]) | ((grp_score[o] == grp_score[g]) & (o < g))
            ahead = ahead + beats.astype(I32)
        masked.append(jnp.where(ahead < TOPK_GROUPS, grp_sel[g], neg))
    cur = jnp.concatenate(masked, axis=0)
    row = lax.broadcasted_iota(I32, (N_EXPERTS, tm), 0)
    member = jnp.zeros((N_EXPERTS, tm), F32)
    picks, wsel = [], []
    for k in range(TOP_K):
        mx = jnp.max(cur, axis=0, keepdims=True)
        ei = jnp.min(jnp.where(cur == mx, row, N_EXPERTS), axis=0, keepdims=True)
        hit = row == ei
        picks.append(ei)
        wsel.append(jnp.sum(jnp.where(hit, score, 0.0), axis=0, keepdims=True))
        cur = jnp.where(hit, neg, cur)
        member = jnp.where(hit, 1.0, member)
    tot = wsel[0]
    for k in range(1, TOP_K):
        tot = tot + wsel[k]
    before = _dot(member.astype(BF16), tri_ref[...]) + base
    for k in range(TOP_K):
        idx_ref[k:k + 1, cols] = picks[k]
        rk = jnp.sum(jnp.where(row == picks[k], before, 0.0), axis=0, keepdims=True)
        rank_ref[k:k + 1, cols] = rk.astype(I32)
    wrows = [wsel[k] / tot * ROUTED_SCALE for k in range(TOP_K)]
    wmat = jnp.concatenate(wrows + [jnp.zeros((LANE - TOP_K, tm), F32)], axis=0)
    wts_ref[cols, :] = wmat.T
    return jnp.sum(member, axis=1, keepdims=True)


def _mix_call(x2, ypool, attn, wop, woa, mod3, g, b, wr_t, rb, tri, tiles_per_seq):
    T = x2.shape[0]
    tm = TM_MIX
    row = lambda i: (i, 0)
    fixed = lambda i: (0, 0)
    tok = lambda i: (0, i)
    return pl.pallas_call(
        _mix_kernel,
        grid=(T // tm,),
        in_specs=[pl.BlockSpec((tm, D_MODEL), row),
                  pl.BlockSpec((tm, POOL_WIDTH), row),
                  pl.BlockSpec((tm, ATTN_PAD), row),
                  pl.BlockSpec((POOL_WIDTH, D_MODEL), fixed),
                  pl.BlockSpec((ATTN_PAD, D_MODEL), fixed),
                  pl.BlockSpec((1, 6, D_MODEL), lambda i: (i // tiles_per_seq, 0, 0)),
                  pl.BlockSpec((1, D_MODEL), fixed),
                  pl.BlockSpec((1, D_MODEL), fixed),
                  pl.BlockSpec((2 * N_EXPERTS, D_MODEL), fixed),
                  pl.BlockSpec((N_EXPERTS, 1), fixed),
                  pl.BlockSpec((MIX_SUB, MIX_SUB), fixed)],
        out_specs=[pl.BlockSpec((tm, D_MODEL), row),
                   pl.BlockSpec((tm * ROW_SUB, LANE), row),
                   pl.BlockSpec((TOP_K, tm), tok),
                   pl.BlockSpec((tm, LANE), row),
                   pl.BlockSpec((TOP_K, tm), tok),
                   pl.BlockSpec((N_EXPERTS, LANE), fixed)],
        out_shape=[jax.ShapeDtypeStruct((T, D_MODEL), F32),
                   jax.ShapeDtypeStruct((T * ROW_SUB, LANE), I32),
                   jax.ShapeDtypeStruct((TOP_K, T), I32),
                   jax.ShapeDtypeStruct((T, LANE), F32),
                   jax.ShapeDtypeStruct((TOP_K, T), I32),
                   jax.ShapeDtypeStruct((N_EXPERTS, LANE), I32)],
        scratch_shapes=[pltpu.VMEM((N_EXPERTS, LANE), F32)],
        compiler_params=pltpu.CompilerParams(dimension_semantics=("arbitrary",)),
        name="mix",
    )(x2, ypool, attn, wop, woa, mod3, g, b, wr_t, rb, tri)


def _slab(r):
    return pl.ds(pl.multiple_of(r * ROW_SUB, ROW_SUB), ROW_SUB)


def _dest_kernel(ps_ref, idx_ref, rank_ref, o_ref):
    idx = idx_ref[...]
    d = rank_ref[...]
    for e in range(N_EXPERTS):
        d = d + jnp.where(idx == e, ps_ref[e], 0)
    tm = o_ref.shape[1]
    for j in range(idx.shape[1] // tm):
        o_ref[j * TOP_K:(j + 1) * TOP_K, :] = d[:, j * tm:(j + 1) * tm]


def _dest_call(pad_start, idx_t, rank_t):
    T = idx_t.shape[1]
    tm = TM_ROUTE
    td = 2048 if T % 2048 == 0 else tm
    per = td // tm
    grid_spec = pltpu.PrefetchScalarGridSpec(
        num_scalar_prefetch=1,
        grid=(T // td,),
        in_specs=[pl.BlockSpec((TOP_K, td), lambda i, ps: (0, i)),
                  pl.BlockSpec((TOP_K, td), lambda i, ps: (0, i))],
        out_specs=pl.BlockSpec((per * TOP_K, tm), lambda i, ps: (i, 0)),
    )
    return pl.pallas_call(
        _dest_kernel,
        grid_spec=grid_spec,
        out_shape=jax.ShapeDtypeStruct((T // tm * TOP_K, tm), I32),
        name="dest",
    )(pad_start, idx_t, rank_t)


def _dispatch_kernel(pst_ref, plen_ref, nreal_ref, dest_hbm, hp_ref, xs_hbm, dst0, dst1, zbuf, isem, ssem, zsem):
    tm = hp_ref.shape[0] // ROW_SUB
    bm = zbuf.shape[0] // ROW_SUB
    n_idx = TOP_K * tm
    nb = xs_hbm.shape[0] // (bm * ROW_SUB)
    i = pl.program_id(0)
    nt = pl.num_programs(0)
    dst = (dst0, dst1)

    def idx_copy(tile, s):
        off = pl.multiple_of(tile * n_idx, n_idx)
        return pltpu.make_async_copy(dest_hbm.at[pl.ds(off, n_idx)], dst[s], isem.at[s])

    def pad_fill(e):
        n = plen_ref[e] * ROW_SUB
        start = pl.multiple_of(pst_ref[e] * ROW_SUB, ROW_SUB)
        return pltpu.make_async_copy(zbuf.at[pl.ds(0, n)], xs_hbm.at[pl.ds(start, n)], zsem)

    def dead_fill(j):
        return pltpu.make_async_copy(zbuf, xs_hbm.at[pl.ds(j * bm * ROW_SUB, bm * ROW_SUB)], zsem)

    def fills(act):
        for e in range(N_EXPERTS):
            pl.when(plen_ref[e] > 0)(functools.partial(lambda e: act(pad_fill(e)), e))
        for j in range(nb - N_EXPERTS, nb):
            pl.when(j >= nreal_ref[0])(functools.partial(lambda j: act(dead_fill(j)), j))

    @pl.when(i == 0)
    def _():
        zbuf[...] = jnp.zeros_like(zbuf)
        idx_copy(0, 0).start()
        idx_copy(0, 0).wait()
        fills(lambda cp: cp.start())

    def scatter(s):
        @pl.when(i + 1 < nt)
        def _():
            idx_copy(i + 1, 1 - s).start()

        def body(rb, c):
            for rr in range(SUBLANE):
                r = rb * SUBLANE + rr
                for k in range(TOP_K):
                    pltpu.make_async_copy(hp_ref.at[_slab(r)], xs_hbm.at[_slab(dst[s][k * tm + r])],
                                          ssem).start(priority=k % 2)
            return c
        lax.fori_loop(0, tm // SUBLANE, body, 0)
        for k in range(TOP_K):
            pltpu.make_async_copy(hp_ref, xs_hbm.at[pl.ds(0, tm * ROW_SUB)], ssem).wait()

        @pl.when(i + 1 < nt)
        def _():
            idx_copy(i + 1, 1 - s).wait()

    for s in range(2):
        pl.when(i % 2 == s)(functools.partial(scatter, s))

    @pl.when(i == 0)
    def _():
        fills(lambda cp: cp.wait())


def _dispatch_call(pad_row_start, pad_len, n_real, dest, hp, nb):
    tm = TM_ROUTE
    bm = BM_MOE
    T = hp.shape[0] // ROW_SUB
    grid_spec = pltpu.PrefetchScalarGridSpec(
        num_scalar_prefetch=3,
        grid=(T // tm,),
        in_specs=[pl.BlockSpec(memory_space=pl.ANY),
                  pl.BlockSpec((tm * ROW_SUB, LANE), lambda i, a, b, c: (i, 0))],
        out_specs=pl.BlockSpec(memory_space=pl.ANY),
        scratch_shapes=[pltpu.SMEM((TOP_K * tm,), I32),
                        pltpu.SMEM((TOP_K * tm,), I32),
                        pltpu.VMEM((bm * ROW_SUB, LANE), I32),
                        pltpu.SemaphoreType.DMA((2,)),
                        pltpu.SemaphoreType.DMA,
                        pltpu.SemaphoreType.DMA],
    )
    return pl.pallas_call(
        _dispatch_kernel,
        grid_spec=grid_spec,
        out_shape=jax.ShapeDtypeStruct((nb * bm * ROW_SUB, LANE), I32),
        compiler_params=pltpu.CompilerParams(dimension_semantics=("arbitrary",)),
        name="dispatch",
    )(pad_row_start, pad_len, n_real, dest, hp)


def _moe_kernel(be_ref, nreal_ref, x_ref, wg_ref, wu_ref, wd_ref, y_ref, wgb, wub, wdb):
    bm = x_ref.shape[0] // ROW_SUB
    i = pl.program_id(0)
    n = nreal_ref[0]

    @pl.when(i < n)
    def _():
        e_changed = (i == 0) | (be_ref[i] != be_ref[jnp.maximum(i - 1, 0)])

        @pl.when(e_changed)
        def _():
            wgb[...] = wg_ref[0].astype(BF16)
            wub[...] = wu_ref[0].astype(BF16)
            wdb[...] = wd_ref[0].astype(BF16)

        cm = bm // MOE_ROW_CHUNKS

        def up(c):
            halves = [_unpack2(w) for w in _load_rows(x_ref, cm, c * cm)]
            xb = jnp.concatenate([lo.astype(BF16) for lo, _ in halves] + [hi.astype(BF16) for _, hi in halves],
                                 axis=1)
            return (_silu(_dot(xb, wgb[...])) * _dot(xb, wub[...])).astype(BF16)

        nxt = up(0)
        for c in range(MOE_ROW_CHUNKS):
            a = nxt
            if c + 1 < MOE_ROW_CHUNKS:
                nxt = up(c + 1)
            y = _dot(a, wdb[...])
            _store_rows(y_ref, _pack2(y[:, :HALF_D], y[:, HALF_D:]), c * cm)

    @pl.when(i >= n)
    def _():
        y_ref[...] = jnp.zeros_like(y_ref)


def _moe_call(block_expert, n_real, xs, w_gate, w_up, w_down):
    bm = BM_MOE
    nb = block_expert.shape[0]
    wmap = lambda i, be, nr: (be[i], 0, 0)
    rows = pl.BlockSpec((bm * ROW_SUB, LANE), lambda i, be, nr: (i, 0))
    grid_spec = pltpu.PrefetchScalarGridSpec(
        num_scalar_prefetch=2,
        grid=(nb,),
        in_specs=[rows,
                  pl.BlockSpec((1, D_MODEL, EXPERT_FF), wmap),
                  pl.BlockSpec((1, D_MODEL, EXPERT_FF), wmap),
                  pl.BlockSpec((1, EXPERT_FF, D_MODEL), wmap)],
        out_specs=rows,
        scratch_shapes=[pltpu.VMEM((D_MODEL, EXPERT_FF), BF16),
                        pltpu.VMEM((D_MODEL, EXPERT_FF), BF16),
                        pltpu.VMEM((EXPERT_FF, D_MODEL), BF16)],
    )
    return pl.pallas_call(
        _moe_kernel,
        grid_spec=grid_spec,
        out_shape=jax.ShapeDtypeStruct(xs.shape, I32),
        compiler_params=pltpu.CompilerParams(dimension_semantics=("arbitrary",)),
        name="moe",
    )(block_expert, n_real, xs, w_gate, w_up, w_down)


def _combine_kernel(dest_hbm, ys_hbm, wc_ref, xmid_ref, mod_ref, wsg_ref, wsu_ref, wsd_ref, g_ref, b_ref,
                    o_ref, dst0, dst1, rbuf, isem, gsem):
    tm = xmid_ref.shape[0]
    n_idx = TOP_K * tm
    i = pl.program_id(0)
    nt = pl.num_programs(0)
    dst = (dst0, dst1)

    def idx_copy(tile, s):
        off = pl.multiple_of(tile * n_idx, n_idx)
        return pltpu.make_async_copy(dest_hbm.at[pl.ds(off, n_idx)], dst[s], isem.at[s])

    def gather_copy(s, k, r, src_row):
        return pltpu.make_async_copy(ys_hbm.at[_slab(src_row)], rbuf.at[s, k, pl.ds(r * ROW_SUB, ROW_SUB)],
                                     gsem.at[s])

    def start_gather(s):
        for r in range(tm):
            for k in range(TOP_K):
                gather_copy(s, k, r, dst[s][k * tm + r]).start(priority=k % 2)

    def wait_gather(s):
        for k in range(TOP_K):
            pltpu.make_async_copy(ys_hbm.at[pl.ds(0, tm * ROW_SUB)], rbuf.at[s, k], gsem.at[s]).wait()

    @pl.when(i == 0)
    def _():
        for s in range(2):
            idx_copy(jnp.minimum(s, nt - 1), s).start()
        for s in range(2):
            idx_copy(jnp.minimum(s, nt - 1), s).wait()

        def body(rb, c):
            for rr in range(SUBLANE):
                r = rb * SUBLANE + rr
                for k in range(TOP_K):
                    pltpu.make_async_copy(ys_hbm.at[_slab(dst0[k * tm + r])], rbuf.at[0, k, _slab(r)],
                                          gsem.at[0]).start(priority=k % 2)
            return c
        lax.fori_loop(0, tm // SUBLANE, body, 0)

    def finish(s):
        wc = wc_ref[...]
        acc = None
        for k in range(TOP_K):
            parts = [_unpack2(w) for w in _load_rows(rbuf.at[s, k], tm)]
            parts = [lo for lo, _ in parts] + [hi for _, hi in parts]
            wk = wc[:, k:k + 1]
            acc = [wk * p for p in parts] if acc is None else [a + wk * p for a, p in zip(acc, parts)]
        routed = jnp.concatenate(acc, axis=1)
        xmid = xmid_ref[...]
        sh2 = mod_ref[0, 3:4, :]
        sc2 = mod_ref[0, 4:5, :]
        g2 = mod_ref[0, 5:6, :]
        hb = (xmid * (1.0 + sc2) + sh2).astype(BF16)
        a = _silu(_dot(hb, wsg_ref[...])) * _dot(hb, wsu_ref[...])
        shared = _dot(a.astype(BF16), wsd_ref[...])
        o_ref[...] = _layer_norm(ALPHA * xmid + g2 * (routed + shared), g_ref[...], b_ref[...])

    def step(s):
        @pl.when((i >= 1) & (i + 1 < nt))
        def _():
            idx_copy(i + 1, 1 - s).wait()

        wait_gather(s)
        start_gather(1 - s)
        finish(s)

        @pl.when(i + 2 < nt)
        def _():
            idx_copy(i + 2, s).start()

        @pl.when(i == nt - 1)
        def _():
            wait_gather(1 - s)

    for s in range(2):
        pl.when(i % 2 == s)(functools.partial(step, s))


def _combine_call(dest, ys, wcol, xmid, mod3, wsg, wsu, wsd, g, b, tiles_per_seq):
    T = xmid.shape[0]
    tm = TM_ROUTE
    fixed = lambda i: (0, 0)
    row = lambda i: (i, 0)
    return pl.pallas_call(
        _combine_kernel,
        grid=(T // tm,),
        in_specs=[pl.BlockSpec(memory_space=pl.ANY),
                  pl.BlockSpec(memory_space=pl.ANY),
                  pl.BlockSpec((tm, LANE), row),
                  pl.BlockSpec((tm, D_MODEL), row),
                  pl.BlockSpec((1, 6, D_MODEL), lambda i: (i // tiles_per_seq, 0, 0)),
                  pl.BlockSpec((D_MODEL, SHARED_FF), fixed),
                  pl.BlockSpec((D_MODEL, SHARED_FF), fixed),
                  pl.BlockSpec((SHARED_FF, D_MODEL), fixed),
                  pl.BlockSpec((1, D_MODEL), fixed),
                  pl.BlockSpec((1, D_MODEL), fixed)],
        out_specs=pl.BlockSpec((tm, D_MODEL), row),
        out_shape=jax.ShapeDtypeStruct((T, D_MODEL), F32),
        scratch_shapes=[pltpu.SMEM((TOP_K * tm,), I32),
                        pltpu.SMEM((TOP_K * tm,), I32),
                        pltpu.VMEM((2, TOP_K, tm * ROW_SUB, LANE), I32),
                        pltpu.SemaphoreType.DMA((2,)),
                        pltpu.SemaphoreType.DMA((2,))],
        compiler_params=pltpu.CompilerParams(dimension_semantics=("arbitrary",)),
        name="combine",
    )(dest, ys, wcol, xmid, mod3, wsg, wsu, wsd, g, b)


def _group(a, b):
    ref = a if a is not None else b
    z = jnp.zeros(ref.shape[:-1], ref.dtype)
    a0, a1 = (a[..., 0], a[..., 1]) if a is not None else (z, z)
    b0, b1 = (b[..., 0], b[..., 1]) if b is not None else (z, z)
    return jnp.concatenate([a0, b0, a1, b1], axis=-1)


def _q_groups(per_head):
    return jnp.concatenate([_group(per_head(3 * (2 * p) + j), per_head(3 * (2 * p + 1) + j))
                            for p in range(N_KV_HEADS // 2) for j in range(GQA_GROUP)], axis=-1)


def _k_groups(per_head):
    return jnp.concatenate([_group(per_head(h), None) if h % 2 == 0 else _group(None, per_head(h))
                            for h in range(N_KV_HEADS)], axis=-1)


def _rope_pairs(seq):
    inv_freq = ROPE_THETA ** (-jnp.arange(0, HALF, 2, dtype=F32) / HALF)
    pos = jnp.arange(seq)
    rowp = (pos // GRID_W).astype(F32)
    colp = (pos % GRID_W).astype(F32)
    ang = jnp.concatenate([rowp[:, None] * inv_freq, colp[:, None] * inv_freq], axis=-1)
    cos, sin = jnp.cos(ang), jnp.sin(ang)
    return jnp.stack([cos, cos], axis=-1), jnp.stack([-sin, sin], axis=-1)


def kernel(x, c, ctx, c_ctx, w_mod, b_mod, w_in, q_norm, k_norm, pool_w, pool_scale, w_out, ln1_g, ln1_b,
           w_router, router_bias, w_gate, w_up, w_down, ws_gate, ws_up, ws_down, ln2_g, ln2_b):
    B, S, D = x.shape
    C = ctx.shape[1]
    T = B * S
    assert D == D_MODEL and w_mod.shape[0] == DEPTH and B + 1 <= MOD_ROWS
    assert S % TM_INPROJ == 0 and S % TQ_ATTN == 0 and S % TM_MIX == 0 and S % TM_ROUTE == 0
    assert S % GRID_W == 0 and C % SUBLANE == 0

    cc = jnp.concatenate([c, c_ctx[None, :], jnp.zeros((MOD_ROWS - B - 1, D), F32)], axis=0)
    mod3 = _mod_call(cc, w_mod[0], b_mod[0][None, :]).reshape(MOD_ROWS, 6, D)

    w = w_in[0]
    o1, o2, o3 = POOL_WIDTH, POOL_WIDTH + ATTN_WIDTH, POOL_WIDTH + ATTN_WIDTH + KV_WIDTH
    wq4 = w[:, o1:o2].reshape(D, N_HEADS, HALF, 2)
    wk4 = w[:, o2:o3].reshape(D, N_KV_HEADS, HALF, 2)
    wq = _q_groups(lambda h: wq4[:, h])
    wk = _k_groups(lambda h: wk4[:, h])
    w_all = jnp.concatenate([w[:, :o1], wq, wk, w[:, o3:]], axis=1).astype(BF16)
    w_kv = jnp.concatenate([wk, w[:, o3:]], axis=1).astype(BF16)
    qg4, kg4 = q_norm[0].reshape(1, HALF, 2), k_norm[0].reshape(1, HALF, 2)
    qg = _group(qg4, qg4)
    kg = _k_groups(lambda h: kg4)
    cos_p, sin_p = _rope_pairs(S)
    qscale = HEAD_DIM ** -0.5 * LOG2_E
    cq, sq = _group(cos_p, cos_p) * qscale, _group(sin_p, sin_p) * qscale
    ck = jnp.concatenate([_group(cos_p, None), _group(None, cos_p)], axis=-1)
    sk = jnp.concatenate([_group(sin_p, None), _group(None, sin_p)], axis=-1)
    slot = (jnp.arange(LANE) // HALF) % 2
    same = (slot[:, None] == slot[None, :]).astype(F32) * (1.0 / HEAD_DIM)
    gg = jnp.concatenate([same, same], axis=0).astype(BF16)
    u, q, k_l, v_l = _inproj_call(x, mod3, w_all, gg, qg, kg, cq, sq, ck, sk)
    k_c, v_c = _ctx_kv_call(ctx, mod3, w_kv, kg, B)

    k_all = jnp.concatenate([k_c, k_l], axis=1)
    v_all = jnp.concatenate([v_c, v_l], axis=1).reshape(B, C + S, N_KV_HEADS, HEAD_DIM)
    vt = jnp.transpose(v_all, (0, 2, 3, 1))
    ones = jnp.ones((B, N_KV_HEADS, 1, C + S), BF16)
    pad = jnp.zeros((B, N_KV_HEADS, VT_ROWS - HEAD_DIM - 1, C + S), BF16)
    attn = _attn_call(q, k_all, jnp.concatenate([vt, ones, pad], axis=2))

    bd = jax.scipy.linalg.block_diag(*[pool_w[0, g] for g in range(len(POOL_WINDOWS))]).astype(BF16)
    ypool = _pool_call(u, bd, pool_scale[0][None, :])

    wo = w_out[0]
    wop = wo[:POOL_WIDTH].astype(BF16)
    woa = wo[POOL_WIDTH:].reshape(N_KV_HEADS, GQA_GROUP * HEAD_DIM, D)
    woa = jnp.pad(woa, ((0, 0), (0, 2 * LANE - GQA_GROUP * HEAD_DIM), (0, 0))).reshape(ATTN_PAD, D).astype(BF16)
    tri = (jnp.arange(MIX_SUB)[:, None] < jnp.arange(MIX_SUB)[None, :]).astype(BF16)
    wr = w_router[0].T
    wr_hi = wr.astype(BF16)
    wr_lo = (wr - wr_hi.astype(F32)).astype(BF16)
    xmid, hp, idx_t, wcol, rank_t, counts = _mix_call(
        x.reshape(T, D), ypool.reshape(T, POOL_WIDTH), attn.reshape(T, ATTN_PAD), wop, woa, mod3,
        ln1_g[0][None, :], ln1_b[0][None, :], jnp.concatenate([wr_hi, wr_lo], axis=0),
        router_bias[0][:, None], tri, S // TM_MIX)

    bm = BM_MOE
    counts = counts[:, 0]
    padded = ((counts + bm - 1) // bm) * bm
    pad_end = jnp.cumsum(padded)
    pad_start = pad_end - padded
    nb = T * TOP_K // bm + N_EXPERTS
    n_real = (pad_end[-1] // bm).astype(I32).reshape(1)
    blk_row = jnp.arange(nb, dtype=I32) * bm
    block_expert = jnp.minimum(jnp.sum((pad_end[None, :] <= blk_row[:, None]).astype(I32), axis=1), N_EXPERTS - 1)

    dest = _dest_call(pad_start.astype(I32), idx_t, rank_t).reshape(-1)
    xs = _dispatch_call((pad_start + counts).astype(I32), (padded - counts).astype(I32), n_real, dest, hp, nb)
    ys = _moe_call(block_expert.astype(I32), n_real, xs, w_gate[0], w_up[0], w_down[0])
    out = _combine_call(dest, ys, wcol, xmid, mod3, ws_gate[0].astype(BF16), ws_up[0].astype(BF16),
                        ws_down[0].astype(BF16), ln2_g[0][None, :], ln2_b[0][None, :], S // TM_ROUTE)
    return out.reshape(B, S, D)
```

```python
import functools

import jax
import jax.numpy as jnp
from jax import lax
from jax.experimental import pallas as pl
from jax.experimental.pallas import tpu as pltpu

F32 = jnp.float32
BF16 = jnp.bfloat16
I32 = jnp.int32

LANE = 128
SUBLANE = 8

D_MODEL = 1024
GRID_W = 64
POOL_WIDTH = 256
POOL_WINDOWS = (2, 4, 8, 16)
POOL_GROUP = 64
HEAD_DIM = 64
HALF = HEAD_DIM // 2
N_HEADS = 12
N_KV_HEADS = 4
GQA_GROUP = N_HEADS // N_KV_HEADS
ATTN_WIDTH = N_HEADS * HEAD_DIM
KV_WIDTH = N_KV_HEADS * HEAD_DIM
ROPE_THETA = 10000.0
N_EXPERTS = 64
TOP_K = 8
N_GROUPS = 8
GROUP_SIZE = N_EXPERTS // N_GROUPS
TOPK_GROUPS = 4
EXPERT_FF = 256
SHARED_FF = 256
ROUTED_SCALE = 2.5
DEPTH = 1
ALPHA = (2.0 * DEPTH) ** 0.25
LN_EPS = 1e-5
RMS_EPS = 1e-6

MOD_ROWS = 24
HALF_D = D_MODEL // 2
ROW_SUB = HALF_D // LANE
Q_COLS = N_HEADS // 2 * LANE
K_COLS = N_KV_HEADS * LANE
ATTN_PAD = N_KV_HEADS * 2 * LANE
VT_ROWS = HEAD_DIM + 16
LOG2_E = 1.4426950408889634

TM_INPROJ = 512
TQ_ATTN = 512
ATTN_SUB = 256
TM_MIX = 512
MIX_SUB = 256
BM_MOE = 1024
MOE_ROW_CHUNKS = 4
TM_ROUTE = 256

_NT = (((1,), (1,)), ((), ()))


def _dot(a, b):
    return jnp.dot(a, b, preferred_element_type=F32)


def _pack2(lo, hi):
    lo_bits = lax.bitcast_convert_type(lo.astype(BF16).astype(F32), I32)
    hi_bits = lax.bitcast_convert_type(hi.astype(BF16).astype(F32), I32)
    return lax.shift_right_logical(lo_bits, 16) | (hi_bits & jnp.int32(-65536))


def _unpack2(w):
    lo = lax.bitcast_convert_type(lax.shift_left(w, 16), F32)
    hi = lax.bitcast_convert_type(w & jnp.int32(-65536), F32)
    return lo, hi


def _store_rows(ref, packed, row0=0):
    n = packed.shape[0]
    for s in range(ROW_SUB):
        ref[pl.ds(row0 * ROW_SUB + s, n, stride=ROW_SUB), :] = packed[:, s * LANE:(s + 1) * LANE]


def _load_rows(ref, n, row0=0):
    return [ref[pl.ds(row0 * ROW_SUB + s, n, stride=ROW_SUB), :] for s in range(ROW_SUB)]


def _silu(x):
    return x * jax.nn.sigmoid(x)


def _layer_norm(r, g, b):
    mu = jnp.mean(r, axis=-1, keepdims=True)
    d = r - mu
    var = jnp.mean(d * d, axis=-1, keepdims=True)
    return d * lax.rsqrt(var + LN_EPS) * g + b


def _mod_kernel(c_ref, w_ref, b_ref, o_ref):
    a = _silu(c_ref[...])
    o_ref[...] = jnp.dot(a, w_ref[...], precision=lax.Precision.HIGHEST,
                         preferred_element_type=F32) + b_ref[...]


def _mod_call(cc, w_mod, b_mod):
    n = w_mod.shape[1]
    tn = 512
    return pl.pallas_call(
        _mod_kernel,
        grid=(n // tn,),
        in_specs=[pl.BlockSpec((MOD_ROWS, D_MODEL), lambda j: (0, 0)),
                  pl.BlockSpec((D_MODEL, tn), lambda j: (0, j)),
                  pl.BlockSpec((1, tn), lambda j: (0, j))],
        out_specs=pl.BlockSpec((MOD_ROWS, tn), lambda j: (0, j)),
        out_shape=jax.ShapeDtypeStruct((MOD_ROWS, n), F32),
        name="mod",
    )(cc, w_mod, b_mod)


def _norm_head(seg, g):
    ms = jnp.sum(seg * seg, axis=-1, keepdims=True) * (1.0 / HEAD_DIM)
    return seg * lax.rsqrt(ms + RMS_EPS) * g


def _norm_rope(seg, gg, g, c, s):
    sq = seg * seg
    hi = sq.astype(BF16)
    lo = (sq - hi.astype(F32)).astype(BF16)
    ms = _dot(jnp.concatenate([hi, lo], axis=1), gg)
    xn = seg * lax.rsqrt(ms + RMS_EPS) * g
    return xn * c + pltpu.roll(xn, LANE // 2, axis=1) * s


def _inproj_kernel(x_ref, mod_ref, w_ref, gg_ref, qg_ref, kg_ref, cq_ref, sq_ref, ck_ref, sk_ref,
                   u_ref, q_ref, k_ref, v_ref):
    sh = mod_ref[0, 0:1, :]
    sc = mod_ref[0, 1:2, :]
    xm = (x_ref[0] * (1.0 + sc) + sh).astype(BF16)
    gg, qg = gg_ref[...], qg_ref[...]
    cq, sq = cq_ref[...], sq_ref[...]
    n_qp = Q_COLS // (2 * LANE)
    n_kp = K_COLS // (2 * LANE)
    n_chunks = 1 + n_qp + n_kp + 1

    def chunk(c):
        return _dot(xm, w_ref[:, c * 2 * LANE:(c + 1) * 2 * LANE])

    nxt = chunk(0)
    for c in range(n_chunks):
        p = nxt
        if c + 1 < n_chunks:
            nxt = chunk(c + 1)
        if c == 0:
            u_ref[0] = p
        elif c <= n_qp:
            for j in range(2):
                grp = (c - 1) * 2 + j
                q = _norm_rope(p[:, j * LANE:(j + 1) * LANE], gg, qg, cq, sq)
                q_ref[0, :, grp * LANE:(grp + 1) * LANE] = q.astype(BF16)
        elif c <= n_qp + n_kp:
            for j in range(2):
                h = (c - 1 - n_qp) * 2 + j
                k = _norm_rope(p[:, j * LANE:(j + 1) * LANE], gg, kg_ref[:, h * LANE:(h + 1) * LANE],
                               ck_ref[:, j * LANE:(j + 1) * LANE], sk_ref[:, j * LANE:(j + 1) * LANE])
                k_ref[0, :, h * LANE:(h + 1) * LANE] = k.astype(BF16)
        else:
            v_ref[0] = p.astype(BF16)


def _inproj_call(x, mod3, w_all, gg, qg, kg, cq, sq, ck, sk):
    B, S, _ = x.shape
    tm = TM_INPROJ
    ncol = w_all.shape[1]
    tab = pl.BlockSpec((tm, LANE), lambda b, i: (i, 0))
    tab2 = pl.BlockSpec((tm, 2 * LANE), lambda b, i: (i, 0))
    vec = pl.BlockSpec((1, LANE), lambda b, i: (0, 0))
    return pl.pallas_call(
        _inproj_kernel,
        grid=(B, S // tm),
        in_specs=[pl.BlockSpec((1, tm, D_MODEL), lambda b, i: (b, i, 0)),
                  pl.BlockSpec((1, 6, D_MODEL), lambda b, i: (b, 0, 0)),
                  pl.BlockSpec((D_MODEL, ncol), lambda b, i: (0, 0)),
                  pl.BlockSpec((2 * LANE, LANE), lambda b, i: (0, 0)),
                  vec, pl.BlockSpec((1, K_COLS), lambda b, i: (0, 0)), tab, tab, tab2, tab2],
        out_specs=[pl.BlockSpec((1, tm, POOL_WIDTH), lambda b, i: (b, i, 0)),
                   pl.BlockSpec((1, tm, Q_COLS), lambda b, i: (b, i, 0)),
                   pl.BlockSpec((1, tm, K_COLS), lambda b, i: (b, i, 0)),
                   pl.BlockSpec((1, tm, KV_WIDTH), lambda b, i: (b, i, 0))],
        out_shape=[jax.ShapeDtypeStruct((B, S, POOL_WIDTH), F32),
                   jax.ShapeDtypeStruct((B, S, Q_COLS), BF16),
                   jax.ShapeDtypeStruct((B, S, K_COLS), BF16),
                   jax.ShapeDtypeStruct((B, S, KV_WIDTH), BF16)],
        name="inproj",
    )(x, mod3, w_all, gg, qg, kg, cq, sq, ck, sk)


def _ctx_kv_kernel(x_ref, mod_ref, w_ref, kg_ref, k_ref, v_ref):
    sh = mod_ref[0, 0:1, :]
    sc = mod_ref[0, 1:2, :]
    xm = (x_ref[0] * (1.0 + sc) + sh).astype(BF16)
    for pair in range(N_KV_HEADS // 2):
        p = _dot(xm, w_ref[:, pair * 2 * LANE:(pair + 1) * 2 * LANE])
        for j in range(2):
            h = pair * 2 + j
            kn = _norm_head(p[:, j * LANE:(j + 1) * LANE], kg_ref[:, h * LANE:(h + 1) * LANE])
            k_ref[0, :, h * LANE:(h + 1) * LANE] = kn.astype(BF16)
    v_ref[0] = _dot(xm, w_ref[:, K_COLS:K_COLS + KV_WIDTH]).astype(BF16)


def _ctx_kv_call(ctx, mod3, w_kv, kg, ctx_row):
    B, C, _ = ctx.shape
    return pl.pallas_call(
        _ctx_kv_kernel,
        grid=(B,),
        in_specs=[pl.BlockSpec((1, C, D_MODEL), lambda b: (b, 0, 0)),
                  pl.BlockSpec((1, 6, D_MODEL), lambda b: (ctx_row, 0, 0)),
                  pl.BlockSpec((D_MODEL, K_COLS + KV_WIDTH), lambda b: (0, 0)),
                  pl.BlockSpec((1, K_COLS), lambda b: (0, 0))],
        out_specs=[pl.BlockSpec((1, C, K_COLS), lambda b: (b, 0, 0)),
                   pl.BlockSpec((1, C, KV_WIDTH), lambda b: (b, 0, 0))],
        out_shape=[jax.ShapeDtypeStruct((B, C, K_COLS), BF16),
                   jax.ShapeDtypeStruct((B, C, KV_WIDTH), BF16)],
        name="ctx_kv",
    )(ctx, mod3, w_kv, kg)


POOL_PAD = 8


def _pool_kernel(u_ref, bd_ref, ps_ref, y_ref):
    S = u_ref.shape[1]
    n = S + 2 * POOL_PAD
    t = lax.broadcasted_iota(I32, (S, LANE), 0)
    lane = lax.broadcasted_iota(I32, (S, LANE), 1)
    zpad = jnp.zeros((POOL_PAD, LANE), F32)
    for half in range(POOL_WIDTH // LANE):
        u = u_ref[0, :, half * LANE:(half + 1) * LANE]
        ue = jnp.concatenate([zpad, u, zpad], axis=0)
        fwd = {1: ue}
        w = 1
        while w < POOL_WINDOWS[2 * half + 1]:
            fwd[2 * w] = fwd[w] + pltpu.roll(fwd[w], n - w, axis=0)
            w *= 2
        ds = []
        for win in POOL_WINDOWS[2 * half: 2 * half + 2]:
            hw = win // 2
            centred = pltpu.roll(fwd[win], hw, axis=0)[POOL_PAD:POOL_PAD + S]
            cnt = (jnp.minimum(t + hw, S) - jnp.maximum(t - hw, 0)).astype(F32)
            ds.append(centred / cnt - u)
        d = jnp.where(lane < POOL_GROUP, ds[0], ds[1]).astype(BF16)
        sl = slice(half * LANE, (half + 1) * LANE)
        y = _dot(d, bd_ref[sl, sl]) * ps_ref[:, sl]
        y_ref[0, :, sl] = y.astype(BF16)


def _pool_call(u, bd, ps):
    B, S, _ = u.shape
    return pl.pallas_call(
        _pool_kernel,
        grid=(B,),
        in_specs=[pl.BlockSpec((1, S, POOL_WIDTH), lambda b: (b, 0, 0)),
                  pl.BlockSpec((POOL_WIDTH, POOL_WIDTH), lambda b: (0, 0)),
                  pl.BlockSpec((1, POOL_WIDTH), lambda b: (0, 0))],
        out_specs=pl.BlockSpec((1, S, POOL_WIDTH), lambda b: (b, 0, 0)),
        out_shape=jax.ShapeDtypeStruct((B, S, POOL_WIDTH), BF16),
        name="pool",
    )(u, bd, ps)


ROW_REDUCE_WAYS = 16


def _reduce_rows(op, x):
    rows, lanes = x.shape
    if rows % (ROW_REDUCE_WAYS * SUBLANE) == 0:
        x = op(x.reshape(ROW_REDUCE_WAYS, rows // ROW_REDUCE_WAYS, lanes), axis=0)
    return op(x, axis=0, keepdims=True)


def _attn_kernel(q_ref, k_ref, vt_ref, o_ref, s_buf, m_buf):
    t = pl.program_id(0)
    ts = s_buf.shape[-1]
    n_sub = q_ref.shape[1] // ts

    @pl.when(t == 0)
    def _():
        s_buf[1] = jnp.zeros_like(s_buf[1])
        m_buf[1] = jnp.zeros_like(m_buf[1])

    def step(slot):
        k = k_ref[0]
        vt = vt_ref[0, 0]
        for u in range(n_sub):
            cols = pl.ds(u * ts, ts)
            outs = []
            for j in range(GQA_GROUP):
                s = lax.dot_general(k, q_ref[0, cols, j * LANE:(j + 1) * LANE], _NT, preferred_element_type=F32)
                s_buf[slot, u, j] = s
                m_buf[slot, u, j] = jnp.broadcast_to(_reduce_rows(jnp.max, s), (SUBLANE, ts))
                p = jnp.exp2(s_buf[1 - slot, u, j] - m_buf[1 - slot, u, j, 0:1, :])
                ol = _dot(vt, p.astype(BF16))
                outs.append(ol[:HEAD_DIM] / ol[HEAD_DIM:HEAD_DIM + 1])
            outs.append(jnp.zeros((2 * LANE - GQA_GROUP * HEAD_DIM, ts), F32))
            o_ref[0, cols, :] = jnp.concatenate(outs, axis=0).T.astype(BF16)

    for slot in range(2):
        pl.when(t % 2 == slot)(functools.partial(step, slot))


def _attn_call(q, k_all, vt_all):
    B, S, _ = q.shape
    Lk = k_all.shape[1]
    tq = TQ_ATTN
    nq = S // tq
    n_items = B * N_KV_HEADS * nq

    def item(t):
        return t // (N_KV_HEADS * nq), (t // nq) % N_KV_HEADS, t % nq

    def cur(t):
        return item(jnp.minimum(t, n_items - 1))

    def prev(t):
        return item(jnp.maximum(t - 1, 0))

    return pl.pallas_call(
        _attn_kernel,
        grid=(n_items + 1,),
        in_specs=[pl.BlockSpec((1, tq, GQA_GROUP * LANE), lambda t: (cur(t)[0], cur(t)[2], cur(t)[1] // 2)),
                  pl.BlockSpec((1, Lk, LANE), lambda t: (cur(t)[0], 0, cur(t)[1])),
                  pl.BlockSpec((1, 1, VT_ROWS, Lk), lambda t: (prev(t)[0], prev(t)[1], 0, 0))],
        out_specs=pl.BlockSpec((1, tq, 2 * LANE), lambda t: (prev(t)[0], prev(t)[2], prev(t)[1])),
        out_shape=jax.ShapeDtypeStruct((B, S, ATTN_PAD), BF16),
        scratch_shapes=[pltpu.VMEM((2, tq // ATTN_SUB, GQA_GROUP, Lk, ATTN_SUB), F32),
                        pltpu.VMEM((2, tq // ATTN_SUB, GQA_GROUP, SUBLANE, ATTN_SUB), F32)],
        compiler_params=pltpu.CompilerParams(dimension_semantics=("arbitrary",)),
        name="attn",
    )(q, k_all, vt_all)


def _mix_kernel(x_ref, yp_ref, at_ref, wop_ref, woa_ref, mod_ref, g_ref, b_ref, wr_ref, rb_ref, tri_ref,
                xmid_ref, hp_ref, idx_ref, wts_ref, rank_ref, cnt_ref, carry):
    i = pl.program_id(0)
    tm = tri_ref.shape[0]
    n_sub = x_ref.shape[0] // tm

    @pl.when(i == 0)
    def _():
        carry[...] = jnp.zeros_like(carry)

    g1 = mod_ref[0, 2:3, :]
    sh2 = mod_ref[0, 3:4, :]
    sc2 = mod_ref[0, 4:5, :]

    def out_proj(s):
        rows = pl.ds(s * tm, tm)
        return _dot(yp_ref[rows, :], wop_ref[...]) + _dot(at_ref[rows, :], woa_ref[...])

    def norm_router(s, mix):
        rows = pl.ds(s * tm, tm)
        xmid = _layer_norm(ALPHA * x_ref[rows, :] + g1 * mix, g_ref[...], b_ref[...])
        xmid_ref[rows, :] = xmid
        h = xmid * (1.0 + sc2) + sh2
        _store_rows(hp_ref, _pack2(h[:, :HALF_D], h[:, HALF_D:]), s * tm)
        h_hi = h.astype(BF16)
        h_lo = (h - h_hi.astype(F32)).astype(BF16)
        a = lax.dot_general(wr_ref[...], h_hi, _NT, preferred_element_type=F32)
        b = lax.dot_general(wr_ref[0:N_EXPERTS, :], h_lo, _NT, preferred_element_type=F32)
        return a[:N_EXPERTS] + a[N_EXPERTS:] + b

    nxt = out_proj(0)
    logits = []
    for s in range(n_sub):
        mix = nxt
        if s + 1 < n_sub:
            nxt = out_proj(s + 1)
        logits.append(norm_router(s, mix))
    base = carry[:, 0:1]
    for s in range(n_sub):
        base = base + _route(logits[s], base, rb_ref, tri_ref, idx_ref, wts_ref, rank_ref, s * tm)
    carry[...] = jnp.broadcast_to(base, carry.shape)
    cnt_ref[...] = carry[...].astype(I32)


def _route(logits, base, rb_ref, tri_ref, idx_ref, wts_ref, rank_ref, tok0):
    tm = logits.shape[1]
    cols = pl.ds(tok0, tm)
    score = jax.nn.sigmoid(logits)
    sel = score + rb_ref[...]
    neg = jnp.float32(-jnp.inf)
    sub = lax.broadcasted_iota(I32, (GROUP_SIZE, tm), 0)
    grp_sel, grp_score = [], []
    for g in range(N_GROUPS):
        sg = sel[g * GROUP_SIZE:(g + 1) * GROUP_SIZE, :]
        m1 = jnp.max(sg, axis=0, keepdims=True)
        i1 = jnp.min(jnp.where(sg == m1, sub, GROUP_SIZE), axis=0, keepdims=True)
        m2 = jnp.max(jnp.where(sub == i1, neg, sg), axis=0, keepdims=True)
        grp_sel.append(sg)
        grp_score.append(m1 + m2)
    masked = []
    for g in range(N_GROUPS):
        ahead = jnp.zeros((1, tm), I32)
        for o in range(N_GROUPS):
            if o == g:
                continue
            beats = (grp_score[o] > grp_score[g]) | ((grp_score[o] == grp_score[g]) & (o < g))
            ahead = ahead + beats.astype(I32)
        masked.append(jnp.where(ahead < TOPK_GROUPS, grp_sel[g], neg))
    cur = jnp.concatenate(masked, axis=0)
    row = lax.broadcasted_iota(I32, (N_EXPERTS, tm), 0)
    member = jnp.zeros((N_EXPERTS, tm), F32)
    picks, wsel = [], []
    for k in range(TOP_K):
        mx = jnp.max(cur, axis=0, keepdims=True)
        ei = jnp.min(jnp.where(cur == mx, row, N_EXPERTS), axis=0, keepdims=True)
        hit = row == ei
        picks.append(ei)
        wsel.append(jnp.sum(jnp.where(hit, score, 0.0), axis=0, keepdims=True))
        cur = jnp.where(hit, neg, cur)
        member = jnp.where(hit, 1.0, member)
    tot = wsel[0]
    for k in range(1, TOP_K):
        tot = tot + wsel[k]
    before = _dot(member.astype(BF16), tri_ref[...]) + base
    for k in range(TOP_K):
        idx_ref[k:k + 1, cols] = picks[k]
        rk = jnp.sum(jnp.where(row == picks[k], before, 0.0), axis=0, keepdims=True)
        rank_ref[k:k + 1, cols] = rk.astype(I32)
    wrows = [wsel[k] / tot * ROUTED_SCALE for k in range(TOP_K)]
    wmat = jnp.concatenate(wrows + [jnp.zeros((LANE - TOP_K, tm), F32)], axis=0)
    wts_ref[cols, :] = wmat.T
    return jnp.sum(member, axis=1, keepdims=True)


def _mix_call(x2, ypool, attn, wop, woa, mod3, g, b, wr_t, rb, tri, tiles_per_seq):
    T = x2.shape[0]
    tm = TM_MIX
    row = lambda i: (i, 0)
    fixed = lambda i: (0, 0)
    tok = lambda i: (0, i)
    return pl.pallas_call(
        _mix_kernel,
        grid=(T // tm,),
        in_specs=[pl.BlockSpec((tm, D_MODEL), row),
                  pl.BlockSpec((tm, POOL_WIDTH), row),
                  pl.BlockSpec((tm, ATTN_PAD), row),
                  pl.BlockSpec((POOL_WIDTH, D_MODEL), fixed),
                  pl.BlockSpec((ATTN_PAD, D_MODEL), fixed),
                  pl.BlockSpec((1, 6, D_MODEL), lambda i: (i // tiles_per_seq, 0, 0)),
                  pl.BlockSpec((1, D_MODEL), fixed),
                  pl.BlockSpec((1, D_MODEL), fixed),
                  pl.BlockSpec((2 * N_EXPERTS, D_MODEL), fixed),
                  pl.BlockSpec((N_EXPERTS, 1), fixed),
                  pl.BlockSpec((MIX_SUB, MIX_SUB), fixed)],
        out_specs=[pl.BlockSpec((tm, D_MODEL), row),
                   pl.BlockSpec((tm * ROW_SUB, LANE), row),
                   pl.BlockSpec((TOP_K, tm), tok),
                   pl.BlockSpec((tm, LANE), row),
                   pl.BlockSpec((TOP_K, tm), tok),
                   pl.BlockSpec((N_EXPERTS, LANE), fixed)],
        out_shape=[jax.ShapeDtypeStruct((T, D_MODEL), F32),
                   jax.ShapeDtypeStruct((T * ROW_SUB, LANE), I32),
                   jax.ShapeDtypeStruct((TOP_K, T), I32),
                   jax.ShapeDtypeStruct((T, LANE), F32),
                   jax.ShapeDtypeStruct((TOP_K, T), I32),
                   jax.ShapeDtypeStruct((N_EXPERTS, LANE), I32)],
        scratch_shapes=[pltpu.VMEM((N_EXPERTS, LANE), F32)],
        compiler_params=pltpu.CompilerParams(dimension_semantics=("arbitrary",)),
        name="mix",
    )(x2, ypool, attn, wop, woa, mod3, g, b, wr_t, rb, tri)


def _slab(r):
    return pl.ds(pl.multiple_of(r * ROW_SUB, ROW_SUB), ROW_SUB)


def _dest_kernel(ps_ref, idx_ref, rank_ref, o_ref):
    idx = idx_ref[...]
    d = rank_ref[...]
    for e in range(N_EXPERTS):
        d = d + jnp.where(idx == e, ps_ref[e], 0)
    tm = o_ref.shape[1]
    for j in range(idx.shape[1] // tm):
        o_ref[j * TOP_K:(j + 1) * TOP_K, :] = d[:, j * tm:(j + 1) * tm]


def _dest_call(pad_start, idx_t, rank_t):
    T = idx_t.shape[1]
    tm = TM_ROUTE
    td = 2048 if T % 2048 == 0 else tm
    per = td // tm
    grid_spec = pltpu.PrefetchScalarGridSpec(
        num_scalar_prefetch=1,
        grid=(T // td,),
        in_specs=[pl.BlockSpec((TOP_K, td), lambda i, ps: (0, i)),
                  pl.BlockSpec((TOP_K, td), lambda i, ps: (0, i))],
        out_specs=pl.BlockSpec((per * TOP_K, tm), lambda i, ps: (i, 0)),
    )
    return pl.pallas_call(
        _dest_kernel,
        grid_spec=grid_spec,
        out_shape=jax.ShapeDtypeStruct((T // tm * TOP_K, tm), I32),
        name="dest",
    )(pad_start, idx_t, rank_t)


def _dispatch_kernel(pst_ref, plen_ref, nreal_ref, dest_hbm, hp_ref, xs_hbm, dst0, dst1, zbuf, isem, ssem, zsem):
    tm = hp_ref.shape[0] // ROW_SUB
    bm = zbuf.shape[0] // ROW_SUB
    n_idx = TOP_K * tm
    nb = xs_hbm.shape[0] // (bm * ROW_SUB)
    i = pl.program_id(0)
    nt = pl.num_programs(0)
    dst = (dst0, dst1)

    def idx_copy(tile, s):
        off = pl.multiple_of(tile * n_idx, n_idx)
        return pltpu.make_async_copy(dest_hbm.at[pl.ds(off, n_idx)], dst[s], isem.at[s])

    def pad_fill(e):
        n = plen_ref[e] * ROW_SUB
        start = pl.multiple_of(pst_ref[e] * ROW_SUB, ROW_SUB)
        return pltpu.make_async_copy(zbuf.at[pl.ds(0, n)], xs_hbm.at[pl.ds(start, n)], zsem)

    def dead_fill(j):
        return pltpu.make_async_copy(zbuf, xs_hbm.at[pl.ds(j * bm * ROW_SUB, bm * ROW_SUB)], zsem)

    def fills(act):
        for e in range(N_EXPERTS):
            pl.when(plen_ref[e] > 0)(functools.partial(lambda e: act(pad_fill(e)), e))
        for j in range(nb - N_EXPERTS, nb):
            pl.when(j >= nreal_ref[0])(functools.partial(lambda j: act(dead_fill(j)), j))

    @pl.when(i == 0)
    def _():
        zbuf[...] = jnp.zeros_like(zbuf)
        idx_copy(0, 0).start()
        idx_copy(0, 0).wait()
        fills(lambda cp: cp.start())

    def scatter(s):
        @pl.when(i + 1 < nt)
        def _():
            idx_copy(i + 1, 1 - s).start()

        def body(rb, c):
            for rr in range(SUBLANE):
                r = rb * SUBLANE + rr
                for k in range(TOP_K):
                    pltpu.make_async_copy(hp_ref.at[_slab(r)], xs_hbm.at[_slab(dst[s][k * tm + r])],
                                          ssem).start(priority=k % 2)
            return c
        lax.fori_loop(0, tm // SUBLANE, body, 0)
        for k in range(TOP_K):
            pltpu.make_async_copy(hp_ref, xs_hbm.at[pl.ds(0, tm * ROW_SUB)], ssem).wait()

        @pl.when(i + 1 < nt)
        def _():
            idx_copy(i + 1, 1 - s).wait()

    for s in range(2):
        pl.when(i % 2 == s)(functools.partial(scatter, s))

    @pl.when(i == 0)
    def _():
        fills(lambda cp: cp.wait())


def _dispatch_call(pad_row_start, pad_len, n_real, dest, hp, nb):
    tm = TM_ROUTE
    bm = BM_MOE
    T = hp.shape[0] // ROW_SUB
    grid_spec = pltpu.PrefetchScalarGridSpec(
        num_scalar_prefetch=3,
        grid=(T // tm,),
        in_specs=[pl.BlockSpec(memory_space=pl.ANY),
                  pl.BlockSpec((tm * ROW_SUB, LANE), lambda i, a, b, c: (i, 0))],
        out_specs=pl.BlockSpec(memory_space=pl.ANY),
        scratch_shapes=[pltpu.SMEM((TOP_K * tm,), I32),
                        pltpu.SMEM((TOP_K * tm,), I32),
                        pltpu.VMEM((bm * ROW_SUB, LANE), I32),
                        pltpu.SemaphoreType.DMA((2,)),
                        pltpu.SemaphoreType.DMA,
                        pltpu.SemaphoreType.DMA],
    )
    return pl.pallas_call(
        _dispatch_kernel,
        grid_spec=grid_spec,
        out_shape=jax.ShapeDtypeStruct((nb * bm * ROW_SUB, LANE), I32),
        compiler_params=pltpu.CompilerParams(dimension_semantics=("arbitrary",)),
        name="dispatch",
    )(pad_row_start, pad_len, n_real, dest, hp)


def _moe_kernel(be_ref, nreal_ref, x_ref, wg_ref, wu_ref, wd_ref, y_ref, wgb, wub, wdb):
    bm = x_ref.shape[0] // ROW_SUB
    i = pl.program_id(0)
    n = nreal_ref[0]

    @pl.when(i < n)
    def _():
        e_changed = (i == 0) | (be_ref[i] != be_ref[jnp.maximum(i - 1, 0)])

        @pl.when(e_changed)
        def _():
            wgb[...] = wg_ref[0].astype(BF16)
            wub[...] = wu_ref[0].astype(BF16)
            wdb[...] = wd_ref[0].astype(BF16)

        cm = bm // MOE_ROW_CHUNKS

        def up(c):
            halves = [_unpack2(w) for w in _load_rows(x_ref, cm, c * cm)]
            xb = jnp.concatenate([lo.astype(BF16) for lo, _ in halves] + [hi.astype(BF16) for _, hi in halves],
                                 axis=1)
            return (_silu(_dot(xb, wgb[...])) * _dot(xb, wub[...])).astype(BF16)

        nxt = up(0)
        for c in range(MOE_ROW_CHUNKS):
            a = nxt
            if c + 1 < MOE_ROW_CHUNKS:
                nxt = up(c + 1)
            y = _dot(a, wdb[...])
            _store_rows(y_ref, _pack2(y[:, :HALF_D], y[:, HALF_D:]), c * cm)

    @pl.when(i >= n)
    def _():
        y_ref[...] = jnp.zeros_like(y_ref)


def _moe_call(block_expert, n_real, xs, w_gate, w_up, w_down):
    bm = BM_MOE
    nb = block_expert.shape[0]
    wmap = lambda i, be, nr: (be[i], 0, 0)
    rows = pl.BlockSpec((bm * ROW_SUB, LANE), lambda i, be, nr: (i, 0))
    grid_spec = pltpu.PrefetchScalarGridSpec(
        num_scalar_prefetch=2,
        grid=(nb,),
        in_specs=[rows,
                  pl.BlockSpec((1, D_MODEL, EXPERT_FF), wmap),
                  pl.BlockSpec((1, D_MODEL, EXPERT_FF), wmap),
                  pl.BlockSpec((1, EXPERT_FF, D_MODEL), wmap)],
        out_specs=rows,
        scratch_shapes=[pltpu.VMEM((D_MODEL, EXPERT_FF), BF16),
                        pltpu.VMEM((D_MODEL, EXPERT_FF), BF16),
                        pltpu.VMEM((EXPERT_FF, D_MODEL), BF16)],
    )
    return pl.pallas_call(
        _moe_kernel,
        grid_spec=grid_spec,
        out_shape=jax.ShapeDtypeStruct(xs.shape, I32),
        compiler_params=pltpu.CompilerParams(dimension_semantics=("arbitrary",)),
        name="moe",
    )(block_expert, n_real, xs, w_gate, w_up, w_down)


def _combine_kernel(dest_hbm, ys_hbm, wc_ref, xmid_ref, mod_ref, wsg_ref, wsu_ref, wsd_ref, g_ref, b_ref,
                    o_ref, dst0, dst1, rbuf, isem, gsem):
    tm = xmid_ref.shape[0]
    n_idx = TOP_K * tm
    i = pl.program_id(0)
    nt = pl.num_programs(0)
    dst = (dst0, dst1)

    def idx_copy(tile, s):
        off = pl.multiple_of(tile * n_idx, n_idx)
        return pltpu.make_async_copy(dest_hbm.at[pl.ds(off, n_idx)], dst[s], isem.at[s])

    def gather_copy(s, k, r, src_row):
        return pltpu.make_async_copy(ys_hbm.at[_slab(src_row)], rbuf.at[s, k, pl.ds(r * ROW_SUB, ROW_SUB)],
                                     gsem.at[s])

    def start_gather(s):
        for r in range(tm):
            for k in range(TOP_K):
                gather_copy(s, k, r, dst[s][k * tm + r]).start(priority=k % 2)

    def wait_gather(s):
        for k in range(TOP_K):
            pltpu.make_async_copy(ys_hbm.at[pl.ds(0, tm * ROW_SUB)], rbuf.at[s, k], gsem.at[s]).wait()

    @pl.when(i == 0)
    def _():
        for s in range(2):
            idx_copy(jnp.minimum(s, nt - 1), s).start()
        for s in range(2):
            idx_copy(jnp.minimum(s, nt - 1), s).wait()

        def body(rb, c):
            for rr in range(SUBLANE):
                r = rb * SUBLANE + rr
                for k in range(TOP_K):
                    pltpu.make_async_copy(ys_hbm.at[_slab(dst0[k * tm + r])], rbuf.at[0, k, _slab(r)],
                                          gsem.at[0]).start(priority=k % 2)
            return c
        lax.fori_loop(0, tm // SUBLANE, body, 0)

    def finish(s):
        wc = wc_ref[...]
        acc = None
        for k in range(TOP_K):
            parts = [_unpack2(w) for w in _load_rows(rbuf.at[s, k], tm)]
            parts = [lo for lo, _ in parts] + [hi for _, hi in parts]
            wk = wc[:, k:k + 1]
            acc = [wk * p for p in parts] if acc is None else [a + wk * p for a, p in zip(acc, parts)]
        routed = jnp.concatenate(acc, axis=1)
        xmid = xmid_ref[...]
        sh2 = mod_ref[0, 3:4, :]
        sc2 = mod_ref[0, 4:5, :]
        g2 = mod_ref[0, 5:6, :]
        hb = (xmid * (1.0 + sc2) + sh2).astype(BF16)
        a = _silu(_dot(hb, wsg_ref[...])) * _dot(hb, wsu_ref[...])
        shared = _dot(a.astype(BF16), wsd_ref[...])
        o_ref[...] = _layer_norm(ALPHA * xmid + g2 * (routed + shared), g_ref[...], b_ref[...])

    def step(s):
        @pl.when((i >= 1) & (i + 1 < nt))
        def _():
            idx_copy(i + 1, 1 - s).wait()

        wait_gather(s)
        start_gather(1 - s)
        finish(s)

        @pl.when(i + 2 < nt)
        def _():
            idx_copy(i + 2, s).start()

        @pl.when(i == nt - 1)
        def _():
            wait_gather(1 - s)

    for s in range(2):
        pl.when(i % 2 == s)(functools.partial(step, s))


def _combine_call(dest, ys, wcol, xmid, mod3, wsg, wsu, wsd, g, b, tiles_per_seq):
    T = xmid.shape[0]
    tm = TM_ROUTE
    fixed = lambda i: (0, 0)
    row = lambda i: (i, 0)
    return pl.pallas_call(
        _combine_kernel,
        grid=(T // tm,),
        in_specs=[pl.BlockSpec(memory_space=pl.ANY),
                  pl.BlockSpec(memory_space=pl.ANY),
                  pl.BlockSpec((tm, LANE), row),
                  pl.BlockSpec((tm, D_MODEL), row),
                  pl.BlockSpec((1, 6, D_MODEL), lambda i: (i // tiles_per_seq, 0, 0)),
                  pl.BlockSpec((D_MODEL, SHARED_FF), fixed),
                  pl.BlockSpec((D_MODEL, SHARED_FF), fixed),
                  pl.BlockSpec((SHARED_FF, D_MODEL), fixed),
                  pl.BlockSpec((1, D_MODEL), fixed),
                  pl.BlockSpec((1, D_MODEL), fixed)],
        out_specs=pl.BlockSpec((tm, D_MODEL), row),
        out_shape=jax.ShapeDtypeStruct((T, D_MODEL), F32),
        scratch_shapes=[pltpu.SMEM((TOP_K * tm,), I32),
                        pltpu.SMEM((TOP_K * tm,), I32),
                        pltpu.VMEM((2, TOP_K, tm * ROW_SUB, LANE), I32),
                        pltpu.SemaphoreType.DMA((2,)),
                        pltpu.SemaphoreType.DMA((2,))],
        compiler_params=pltpu.CompilerParams(dimension_semantics=("arbitrary",)),
        name="combine",
    )(dest, ys, wcol, xmid, mod3, wsg, wsu, wsd, g, b)


def _group(a, b):
    ref = a if a is not None else b
    z = jnp.zeros(ref.shape[:-1], ref.dtype)
    a0, a1 = (a[..., 0], a[..., 1]) if a is not None else (z, z)
    b0, b1 = (b[..., 0], b[..., 1]) if b is not None else (z, z)
    return jnp.concatenate([a0, b0, a1, b1], axis=-1)


def _q_groups(per_head):
    return jnp.concatenate([_group(per_head(3 * (2 * p) + j), per_head(3 * (2 * p + 1) + j))
                            for p in range(N_KV_HEADS // 2) for j in range(GQA_GROUP)], axis=-1)


def _k_groups(per_head):
    return jnp.concatenate([_group(per_head(h), None) if h % 2 == 0 else _group(None, per_head(h))
                            for h in range(N_KV_HEADS)], axis=-1)


def _rope_pairs(seq):
    inv_freq = ROPE_THETA ** (-jnp.arange(0, HALF, 2, dtype=F32) / HALF)
    pos = jnp.arange(seq)
    rowp = (pos // GRID_W).astype(F32)
    colp = (pos % GRID_W).astype(F32)
    ang = jnp.concatenate([rowp[:, None] * inv_freq, colp[:, None] * inv_freq], axis=-1)
    cos, sin = jnp.cos(ang), jnp.sin(ang)
    return jnp.stack([cos, cos], axis=-1), jnp.stack([-sin, sin], axis=-1)


def kernel(x, c, ctx, c_ctx, w_mod, b_mod, w_in, q_norm, k_norm, pool_w, pool_scale, w_out, ln1_g, ln1_b,
           w_router, router_bias, w_gate, w_up, w_down, ws_gate, ws_up, ws_down, ln2_g, ln2_b):
    B, S, D = x.shape
    C = ctx.shape[1]
    T = B * S
    assert D == D_MODEL and w_mod.shape[0] == DEPTH and B + 1 <= MOD_ROWS
    assert S % TM_INPROJ == 0 and S % TQ_ATTN == 0 and S % TM_MIX == 0 and S % TM_ROUTE == 0
    assert S % GRID_W == 0 and C % SUBLANE == 0

    cc = jnp.concatenate([c, c_ctx[None, :], jnp.zeros((MOD_ROWS - B - 1, D), F32)], axis=0)
    mod3 = _mod_call(cc, w_mod[0], b_mod[0][None, :]).reshape(MOD_ROWS, 6, D)

    w = w_in[0]
    o1, o2, o3 = POOL_WIDTH, POOL_WIDTH + ATTN_WIDTH, POOL_WIDTH + ATTN_WIDTH + KV_WIDTH
    wq4 = w[:, o1:o2].reshape(D, N_HEADS, HALF, 2)
    wk4 = w[:, o2:o3].reshape(D, N_KV_HEADS, HALF, 2)
    wq = _q_groups(lambda h: wq4[:, h])
    wk = _k_groups(lambda h: wk4[:, h])
    w_all = jnp.concatenate([w[:, :o1], wq, wk, w[:, o3:]], axis=1).astype(BF16)
    w_kv = jnp.concatenate([wk, w[:, o3:]], axis=1).astype(BF16)
    qg4, kg4 = q_norm[0].reshape(1, HALF, 2), k_norm[0].reshape(1, HALF, 2)
    qg = _group(qg4, qg4)
    kg = _k_groups(lambda h: kg4)
    cos_p, sin_p = _rope_pairs(S)
    qscale = HEAD_DIM ** -0.5 * LOG2_E
    cq, sq = _group(cos_p, cos_p) * qscale, _group(sin_p, sin_p) * qscale
    ck = jnp.concatenate([_group(cos_p, None), _group(None, cos_p)], axis=-1)
    sk = jnp.concatenate([_group(sin_p, None), _group(None, sin_p)], axis=-1)
    slot = (jnp.arange(LANE) // HALF) % 2
    same = (slot[:, None] == slot[None, :]).astype(F32) * (1.0 / HEAD_DIM)
    gg = jnp.concatenate([same, same], axis=0).astype(BF16)
    u, q, k_l, v_l = _inproj_call(x, mod3, w_all, gg, qg, kg, cq, sq, ck, sk)
    k_c, v_c = _ctx_kv_call(ctx, mod3, w_kv, kg, B)

    k_all = jnp.concatenate([k_c, k_l], axis=1)
    v_all = jnp.concatenate([v_c, v_l], axis=1).reshape(B, C + S, N_KV_HEADS, HEAD_DIM)
    vt = jnp.transpose(v_all, (0, 2, 3, 1))
    ones = jnp.ones((B, N_KV_HEADS, 1, C + S), BF16)
    pad = jnp.zeros((B, N_KV_HEADS, VT_ROWS - HEAD_DIM - 1, C + S), BF16)
    attn = _attn_call(q, k_all, jnp.concatenate([vt, ones, pad], axis=2))

    bd = jax.scipy.linalg.block_diag(*[pool_w[0, g] for g in range(len(POOL_WINDOWS))]).astype(BF16)
    ypool = _pool_call(u, bd, pool_scale[0][None, :])

    wo = w_out[0]
    wop = wo[:POOL_WIDTH].astype(BF16)
    woa = wo[POOL_WIDTH:].reshape(N_KV_HEADS, GQA_GROUP * HEAD_DIM, D)
    woa = jnp.pad(woa, ((0, 0), (0, 2 * LANE - GQA_GROUP * HEAD_DIM), (0, 0))).reshape(ATTN_PAD, D).astype(BF16)
    tri = (jnp.arange(MIX_SUB)[:, None] < jnp.arange(MIX_SUB)[None, :]).astype(BF16)
    wr = w_router[0].T
    wr_hi = wr.astype(BF16)
    wr_lo = (wr - wr_hi.astype(F32)).astype(BF16)
    xmid, hp, idx_t, wcol, rank_t, counts = _mix_call(
        x.reshape(T, D), ypool.reshape(T, POOL_WIDTH), attn.reshape(T, ATTN_PAD), wop, woa, mod3,
        ln1_g[0][None, :], ln1_b[0][None, :], jnp.concatenate([wr_hi, wr_lo], axis=0),
        router_bias[0][:, None], tri, S // TM_MIX)

    bm = BM_MOE
    counts = counts[:, 0]
    padded = ((counts + bm - 1) // bm) * bm
    pad_end = jnp.cumsum(padded)
    pad_start = pad_end - padded
    nb = T * TOP_K // bm + N_EXPERTS
    n_real = (pad_end[-1] // bm).astype(I32).reshape(1)
    blk_row = jnp.arange(nb, dtype=I32) * bm
    block_expert = jnp.minimum(jnp.sum((pad_end[None, :] <= blk_row[:, None]).astype(I32), axis=1), N_EXPERTS - 1)

    dest = _dest_call(pad_start.astype(I32), idx_t, rank_t).reshape(-1)
    xs = _dispatch_call((pad_start + counts).astype(I32), (padded - counts).astype(I32), n_real, dest, hp, nb)
    ys = _moe_call(block_expert.astype(I32), n_real, xs, w_gate[0], w_up[0], w_down[0])
    out = _combine_call(dest, ys, wcol, xmid, mod3, ws_gate[0].astype(BF16), ws_up[0].astype(BF16),
                        ws_down[0].astype(BF16), ln2_g[0][None, :], ln2_b[0][None, :], S // TM_ROUTE)
    return out.reshape(B, S, D)
```

```python
import functools

import jax
import jax.numpy as jnp
from jax import lax
from jax.experimental import pallas as pl
from jax.experimental.pallas import tpu as pltpu

F32 = jnp.float32
BF16 = jnp.bfloat16
I32 = jnp.int32

LANE = 128
SUBLANE = 8

D_MODEL = 1024
GRID_W = 64
POOL_WIDTH = 256
POOL_WINDOWS = (2, 4, 8, 16)
POOL_GROUP = 64
HEAD_DIM = 64
HALF = HEAD_DIM // 2
N_HEADS = 12
N_KV_HEADS = 4
GQA_GROUP = N_HEADS // N_KV_HEADS
ATTN_WIDTH = N_HEADS * HEAD_DIM
KV_WIDTH = N_KV_HEADS * HEAD_DIM
ROPE_THETA = 10000.0
N_EXPERTS = 64
TOP_K = 8
N_GROUPS = 8
GROUP_SIZE = N_EXPERTS // N_GROUPS
TOPK_GROUPS = 4
EXPERT_FF = 256
SHARED_FF = 256
ROUTED_SCALE = 2.5
DEPTH = 1
ALPHA = (2.0 * DEPTH) ** 0.25
LN_EPS = 1e-5
RMS_EPS = 1e-6

MOD_ROWS = 24
HALF_D = D_MODEL // 2
ROW_SUB = HALF_D // LANE
Q_COLS = N_HEADS // 2 * LANE
K_COLS = N_KV_HEADS * LANE
ATTN_PAD = N_KV_HEADS * 2 * LANE
VT_ROWS = HEAD_DIM + 16
LOG2_E = 1.4426950408889634

TM_INPROJ = 512
TQ_ATTN = 512
ATTN_SUB = 256
TM_MIX = 512
MIX_SUB = 256
BM_MOE = 1024
MOE_ROW_CHUNKS = 4
TM_ROUTE = 256

_NT = (((1,), (1,)), ((), ()))


def _dot(a, b):
    return jnp.dot(a, b, preferred_element_type=F32)


def _pack2(lo, hi):
    lo_bits = lax.bitcast_convert_type(lo.astype(BF16).astype(F32), I32)
    hi_bits = lax.bitcast_convert_type(hi.astype(BF16).astype(F32), I32)
    return lax.shift_right_logical(lo_bits, 16) | (hi_bits & jnp.int32(-65536))


def _unpack2(w):
    lo = lax.bitcast_convert_type(lax.shift_left(w, 16), F32)
    hi = lax.bitcast_convert_type(w & jnp.int32(-65536), F32)
    return lo, hi


def _store_rows(ref, packed, row0=0):
    n = packed.shape[0]
    for s in range(ROW_SUB):
        ref[pl.ds(row0 * ROW_SUB + s, n, stride=ROW_SUB), :] = packed[:, s * LANE:(s + 1) * LANE]


def _load_rows(ref, n, row0=0):
    return [ref[pl.ds(row0 * ROW_SUB + s, n, stride=ROW_SUB), :] for s in range(ROW_SUB)]


def _silu(x):
    return x * jax.nn.sigmoid(x)


def _layer_norm(r, g, b):
    mu = jnp.mean(r, axis=-1, keepdims=True)
    d = r - mu
    var = jnp.mean(d * d, axis=-1, keepdims=True)
    return d * lax.rsqrt(var + LN_EPS) * g + b


def _mod_kernel(c_ref, w_ref, b_ref, o_ref):
    a = _silu(c_ref[...])
    o_ref[...] = jnp.dot(a, w_ref[...], precision=lax.Precision.HIGHEST,
                         preferred_element_type=F32) + b_ref[...]


def _mod_call(cc, w_mod, b_mod):
    n = w_mod.shape[1]
    tn = 512
    return pl.pallas_call(
        _mod_kernel,
        grid=(n // tn,),
        in_specs=[pl.BlockSpec((MOD_ROWS, D_MODEL), lambda j: (0, 0)),
                  pl.BlockSpec((D_MODEL, tn), lambda j: (0, j)),
                  pl.BlockSpec((1, tn), lambda j: (0, j))],
        out_specs=pl.BlockSpec((MOD_ROWS, tn), lambda j: (0, j)),
        out_shape=jax.ShapeDtypeStruct((MOD_ROWS, n), F32),
        name="mod",
    )(cc, w_mod, b_mod)


def _norm_head(seg, g):
    ms = jnp.sum(seg * seg, axis=-1, keepdims=True) * (1.0 / HEAD_DIM)
    return seg * lax.rsqrt(ms + RMS_EPS) * g


def _norm_rope(seg, gg, g, c, s):
    sq = seg * seg
    hi = sq.astype(BF16)
    lo = (sq - hi.astype(F32)).astype(BF16)
    ms = _dot(jnp.concatenate([hi, lo], axis=1), gg)
    xn = seg * lax.rsqrt(ms + RMS_EPS) * g
    return xn * c + pltpu.roll(xn, LANE // 2, axis=1) * s


def _inproj_kernel(x_ref, mod_ref, w_ref, gg_ref, qg_ref, kg_ref, cq_ref, sq_ref, ck_ref, sk_ref,
                   u_ref, q_ref, k_ref, v_ref):
    sh = mod_ref[0, 0:1, :]
    sc = mod_ref[0, 1:2, :]
    xm = (x_ref[0] * (1.0 + sc) + sh).astype(BF16)
    gg, qg = gg_ref[...], qg_ref[...]
    cq, sq = cq_ref[...], sq_ref[...]
    n_qp = Q_COLS // (2 * LANE)
    n_kp = K_COLS // (2 * LANE)
    n_chunks = 1 + n_qp + n_kp + 1

    def chunk(c):
        return _dot(xm, w_ref[:, c * 2 * LANE:(c + 1) * 2 * LANE])

    nxt = chunk(0)
    for c in range(n_chunks):
        p = nxt
        if c + 1 < n_chunks:
            nxt = chunk(c + 1)
        if c == 0:
            u_ref[0] = p
        elif c <= n_qp:
            for j in range(2):
                grp = (c - 1) * 2 + j
                q = _norm_rope(p[:, j * LANE:(j + 1) * LANE], gg, qg, cq, sq)
                q_ref[0, :, grp * LANE:(grp + 1) * LANE] = q.astype(BF16)
        elif c <= n_qp + n_kp:
            for j in range(2):
                h = (c - 1 - n_qp) * 2 + j
                k = _norm_rope(p[:, j * LANE:(j + 1) * LANE], gg, kg_ref[:, h * LANE:(h + 1) * LANE],
                               ck_ref[:, j * LANE:(j + 1) * LANE], sk_ref[:, j * LANE:(j + 1) * LANE])
                k_ref[0, :, h * LANE:(h + 1) * LANE] = k.astype(BF16)
        else:
            v_ref[0] = p.astype(BF16)


def _inproj_call(x, mod3, w_all, gg, qg, kg, cq, sq, ck, sk):
    B, S, _ = x.shape
    tm = TM_INPROJ
    ncol = w_all.shape[1]
    tab = pl.BlockSpec((tm, LANE), lambda b, i: (i, 0))
    tab2 = pl.BlockSpec((tm, 2 * LANE), lambda b, i: (i, 0))
    vec = pl.BlockSpec((1, LANE), lambda b, i: (0, 0))
    return pl.pallas_call(
        _inproj_kernel,
        grid=(B, S // tm),
        in_specs=[pl.BlockSpec((1, tm, D_MODEL), lambda b, i: (b, i, 0)),
                  pl.BlockSpec((1, 6, D_MODEL), lambda b, i: (b, 0, 0)),
                  pl.BlockSpec((D_MODEL, ncol), lambda b, i: (0, 0)),
                  pl.BlockSpec((2 * LANE, LANE), lambda b, i: (0, 0)),
                  vec, pl.BlockSpec((1, K_COLS), lambda b, i: (0, 0)), tab, tab, tab2, tab2],
        out_specs=[pl.BlockSpec((1, tm, POOL_WIDTH), lambda b, i: (b, i, 0)),
                   pl.BlockSpec((1, tm, Q_COLS), lambda b, i: (b, i, 0)),
                   pl.BlockSpec((1, tm, K_COLS), lambda b, i: (b, i, 0)),
                   pl.BlockSpec((1, tm, KV_WIDTH), lambda b, i: (b, i, 0))],
        out_shape=[jax.ShapeDtypeStruct((B, S, POOL_WIDTH), F32),
                   jax.ShapeDtypeStruct((B, S, Q_COLS), BF16),
                   jax.ShapeDtypeStruct((B, S, K_COLS), BF16),
                   jax.ShapeDtypeStruct((B, S, KV_WIDTH), BF16)],
        name="inproj",
    )(x, mod3, w_all, gg, qg, kg, cq, sq, ck, sk)


def _ctx_kv_kernel(x_ref, mod_ref, w_ref, kg_ref, k_ref, v_ref):
    sh = mod_ref[0, 0:1, :]
    sc = mod_ref[0, 1:2, :]
    xm = (x_ref[0] * (1.0 + sc) + sh).astype(BF16)
    for pair in range(N_KV_HEADS // 2):
        p = _dot(xm, w_ref[:, pair * 2 * LANE:(pair + 1) * 2 * LANE])
        for j in range(2):
            h = pair * 2 + j
            kn = _norm_head(p[:, j * LANE:(j + 1) * LANE], kg_ref[:, h * LANE:(h + 1) * LANE])
            k_ref[0, :, h * LANE:(h + 1) * LANE] = kn.astype(BF16)
    v_ref[0] = _dot(xm, w_ref[:, K_COLS:K_COLS + KV_WIDTH]).astype(BF16)


def _ctx_kv_call(ctx, mod3, w_kv, kg, ctx_row):
    B, C, _ = ctx.shape
    return pl.pallas_call(
        _ctx_kv_kernel,
        grid=(B,),
        in_specs=[pl.BlockSpec((1, C, D_MODEL), lambda b: (b, 0, 0)),
                  pl.BlockSpec((1, 6, D_MODEL), lambda b: (ctx_row, 0, 0)),
                  pl.BlockSpec((D_MODEL, K_COLS + KV_WIDTH), lambda b: (0, 0)),
                  pl.BlockSpec((1, K_COLS), lambda b: (0, 0))],
        out_specs=[pl.BlockSpec((1, C, K_COLS), lambda b: (b, 0, 0)),
                   pl.BlockSpec((1, C, KV_WIDTH), lambda b: (b, 0, 0))],
        out_shape=[jax.ShapeDtypeStruct((B, C, K_COLS), BF16),
                   jax.ShapeDtypeStruct((B, C, KV_WIDTH), BF16)],
        name="ctx_kv",
    )(ctx, mod3, w_kv, kg)


POOL_PAD = 8


def _pool_kernel(u_ref, bd_ref, ps_ref, y_ref):
    S = u_ref.shape[1]
    n = S + 2 * POOL_PAD
    t = lax.broadcasted_iota(I32, (S, LANE), 0)
    lane = lax.broadcasted_iota(I32, (S, LANE), 1)
    zpad = jnp.zeros((POOL_PAD, LANE), F32)
    for half in range(POOL_WIDTH // LANE):
        u = u_ref[0, :, half * LANE:(half + 1) * LANE]
        ue = jnp.concatenate([zpad, u, zpad], axis=0)
        fwd = {1: ue}
        w = 1
        while w < POOL_WINDOWS[2 * half + 1]:
            fwd[2 * w] = fwd[w] + pltpu.roll(fwd[w], n - w, axis=0)
            w *= 2
        ds = []
        for win in POOL_WINDOWS[2 * half: 2 * half + 2]:
            hw = win // 2
            centred = pltpu.roll(fwd[win], hw, axis=0)[POOL_PAD:POOL_PAD + S]
            cnt = (jnp.minimum(t + hw, S) - jnp.maximum(t - hw, 0)).astype(F32)
            ds.append(centred / cnt - u)
        d = jnp.where(lane < POOL_GROUP, ds[0], ds[1]).astype(BF16)
        sl = slice(half * LANE, (half + 1) * LANE)
        y = _dot(d, bd_ref[sl, sl]) * ps_ref[:, sl]
        y_ref[0, :, sl] = y.astype(BF16)


def _pool_call(u, bd, ps):
    B, S, _ = u.shape
    return pl.pallas_call(
        _pool_kernel,
        grid=(B,),
        in_specs=[pl.BlockSpec((1, S, POOL_WIDTH), lambda b: (b, 0, 0)),
                  pl.BlockSpec((POOL_WIDTH, POOL_WIDTH), lambda b: (0, 0)),
                  pl.BlockSpec((1, POOL_WIDTH), lambda b: (0, 0))],
        out_specs=pl.BlockSpec((1, S, POOL_WIDTH), lambda b: (b, 0, 0)),
        out_shape=jax.ShapeDtypeStruct((B, S, POOL_WIDTH), BF16),
        name="pool",
    )(u, bd, ps)


ROW_REDUCE_WAYS = 16


def _reduce_rows(op, x):
    rows, lanes = x.shape
    if rows % (ROW_REDUCE_WAYS * SUBLANE) == 0:
        x = op(x.reshape(ROW_REDUCE_WAYS, rows // ROW_REDUCE_WAYS, lanes), axis=0)
    return op(x, axis=0, keepdims=True)


def _attn_kernel(q_ref, k_ref, vt_ref, o_ref, s_buf, m_buf):
    t = pl.program_id(0)
    ts = s_buf.shape[-1]
    n_sub = q_ref.shape[1] // ts

    @pl.when(t == 0)
    def _():
        s_buf[1] = jnp.zeros_like(s_buf[1])
        m_buf[1] = jnp.zeros_like(m_buf[1])

    def step(slot):
        k = k_ref[0]
        vt = vt_ref[0, 0]
        for u in range(n_sub):
            cols = pl.ds(u * ts, ts)
            outs = []
            for j in range(GQA_GROUP):
                s = lax.dot_general(k, q_ref[0, cols, j * LANE:(j + 1) * LANE], _NT, preferred_element_type=F32)
                s_buf[slot, u, j] = s
                m_buf[slot, u, j] = jnp.broadcast_to(_reduce_rows(jnp.max, s), (SUBLANE, ts))
                p = jnp.exp2(s_buf[1 - slot, u, j] - m_buf[1 - slot, u, j, 0:1, :])
                ol = _dot(vt, p.astype(BF16))
                outs.append(ol[:HEAD_DIM] / ol[HEAD_DIM:HEAD_DIM + 1])
            outs.append(jnp.zeros((2 * LANE - GQA_GROUP * HEAD_DIM, ts), F32))
            o_ref[0, cols, :] = jnp.concatenate(outs, axis=0).T.astype(BF16)

    for slot in range(2):
        pl.when(t % 2 == slot)(functools.partial(step, slot))


def _attn_call(q, k_all, vt_all):
    B, S, _ = q.shape
    Lk = k_all.shape[1]
    tq = TQ_ATTN
    nq = S // tq
    n_items = B * N_KV_HEADS * nq

    def item(t):
        return t // (N_KV_HEADS * nq), (t // nq) % N_KV_HEADS, t % nq

    def cur(t):
        return item(jnp.minimum(t, n_items - 1))

    def prev(t):
        return item(jnp.maximum(t - 1, 0))

    return pl.pallas_call(
        _attn_kernel,
        grid=(n_items + 1,),
        in_specs=[pl.BlockSpec((1, tq, GQA_GROUP * LANE), lambda t: (cur(t)[0], cur(t)[2], cur(t)[1] // 2)),
                  pl.BlockSpec((1, Lk, LANE), lambda t: (cur(t)[0], 0, cur(t)[1])),
                  pl.BlockSpec((1, 1, VT_ROWS, Lk), lambda t: (prev(t)[0], prev(t)[1], 0, 0))],
        out_specs=pl.BlockSpec((1, tq, 2 * LANE), lambda t: (prev(t)[0], prev(t)[2], prev(t)[1])),
        out_shape=jax.ShapeDtypeStruct((B, S, ATTN_PAD), BF16),
        scratch_shapes=[pltpu.VMEM((2, tq // ATTN_SUB, GQA_GROUP, Lk, ATTN_SUB), F32),
                        pltpu.VMEM((2, tq // ATTN_SUB, GQA_GROUP, SUBLANE, ATTN_SUB), F32)],
        compiler_params=pltpu.CompilerParams(dimension_semantics=("arbitrary",)),
        name="attn",
    )(q, k_all, vt_all)


def _mix_kernel(x_ref, yp_ref, at_ref, wop_ref, woa_ref, mod_ref, g_ref, b_ref, wr_ref, rb_ref, tri_ref,
                xmid_ref, hp_ref, idx_ref, wts_ref, rank_ref, cnt_ref, carry):
    i = pl.program_id(0)
    tm = tri_ref.shape[0]
    n_sub = x_ref.shape[0] // tm

    @pl.when(i == 0)
    def _():
        carry[...] = jnp.zeros_like(carry)

    g1 = mod_ref[0, 2:3, :]
    sh2 = mod_ref[0, 3:4, :]
    sc2 = mod_ref[0, 4:5, :]

    def out_proj(s):
        rows = pl.ds(s * tm, tm)
        return _dot(yp_ref[rows, :], wop_ref[...]) + _dot(at_ref[rows, :], woa_ref[...])

    def norm_router(s, mix):
        rows = pl.ds(s * tm, tm)
        xmid = _layer_norm(ALPHA * x_ref[rows, :] + g1 * mix, g_ref[...], b_ref[...])
        xmid_ref[rows, :] = xmid
        h = xmid * (1.0 + sc2) + sh2
        _store_rows(hp_ref, _pack2(h[:, :HALF_D], h[:, HALF_D:]), s * tm)
        h_hi = h.astype(BF16)
        h_lo = (h - h_hi.astype(F32)).astype(BF16)
        a = lax.dot_general(wr_ref[...], h_hi, _NT, preferred_element_type=F32)
        b = lax.dot_general(wr_ref[0:N_EXPERTS, :], h_lo, _NT, preferred_element_type=F32)
        return a[:N_EXPERTS] + a[N_EXPERTS:] + b

    nxt = out_proj(0)
    logits = []
    for s in range(n_sub):
        mix = nxt
        if s + 1 < n_sub:
            nxt = out_proj(s + 1)
        logits.append(norm_router(s, mix))
    base = carry[:, 0:1]
    for s in range(n_sub):
        base = base + _route(logits[s], base, rb_ref, tri_ref, idx_ref, wts_ref, rank_ref, s * tm)
    carry[...] = jnp.broadcast_to(base, carry.shape)
    cnt_ref[...] = carry[...].astype(I32)


def _route(logits, base, rb_ref, tri_ref, idx_ref, wts_ref, rank_ref, tok0):
    tm = logits.shape[1]
    cols = pl.ds(tok0, tm)
    score = jax.nn.sigmoid(logits)
    sel = score + rb_ref[...]
    neg = jnp.float32(-jnp.inf)
    sub = lax.broadcasted_iota(I32, (GROUP_SIZE, tm), 0)
    grp_sel, grp_score = [], []
    for g in range(N_GROUPS):
        sg = sel[g * GROUP_SIZE:(g + 1) * GROUP_SIZE, :]
        m1 = jnp.max(sg, axis=0, keepdims=True)
        i1 = jnp.min(jnp.where(sg == m1, sub, GROUP_SIZE), axis=0, keepdims=True)
        m2 = jnp.max(jnp.where(sub == i1, neg, sg), axis=0, keepdims=True)
        grp_sel.append(sg)
        grp_score.append(m1 + m2)
    masked = []
    for g in range(N_GROUPS):
        ahead = jnp.zeros((1, tm), I32)
        for o in range(N_GROUPS):
            if o == g:
                continue
            beats = (grp_score[o] > grp_score[g]) | ((grp_score[o] == grp_score[g]) & (o < g))
            ahead = ahead + beats.astype(I32)
        masked.append(jnp.where(ahead < TOPK_GROUPS, grp_sel[g], neg))
    cur = jnp.concatenate(masked, axis=0)
    row = lax.broadcasted_iota(I32, (N_EXPERTS, tm), 0)
    member = jnp.zeros((N_EXPERTS, tm), F32)
    picks, wsel = [], []
    for k in range(TOP_K):
        mx = jnp.max(cur, axis=0, keepdims=True)
        ei = jnp.min(jnp.where(cur == mx, row, N_EXPERTS), axis=0, keepdims=True)
        hit = row == ei
        picks.append(ei)
        wsel.append(jnp.sum(jnp.where(hit, score, 0.0), axis=0, keepdims=True))
        cur = jnp.where(hit, neg, cur)
        member = jnp.where(hit, 1.0, member)
    tot = wsel[0]
    for k in range(1, TOP_K):
        tot = tot + wsel[k]
    before = _dot(member.astype(BF16), tri_ref[...]) + base
    for k in range(TOP_K):
        idx_ref[k:k + 1, cols] = picks[k]
        rk = jnp.sum(jnp.where(row == picks[k], before, 0.0), axis=0, keepdims=True)
        rank_ref[k:k + 1, cols] = rk.astype(I32)
    wrows = [wsel[k] / tot * ROUTED_SCALE for k in range(TOP_K)]
    wmat = jnp.concatenate(wrows + [jnp.zeros((LANE - TOP_K, tm), F32)], axis=0)
    wts_ref[cols, :] = wmat.T
    return jnp.sum(member, axis=1, keepdims=True)


def _mix_call(x2, ypool, attn, wop, woa, mod3, g, b, wr_t, rb, tri, tiles_per_seq):
    T = x2.shape[0]
    tm = TM_MIX
    row = lambda i: (i, 0)
    fixed = lambda i: (0, 0)
    tok = lambda i: (0, i)
    return pl.pallas_call(
        _mix_kernel,
        grid=(T // tm,),
        in_specs=[pl.BlockSpec((tm, D_MODEL), row),
                  pl.BlockSpec((tm, POOL_WIDTH), row),
                  pl.BlockSpec((tm, ATTN_PAD), row),
                  pl.BlockSpec((POOL_WIDTH, D_MODEL), fixed),
                  pl.BlockSpec((ATTN_PAD, D_MODEL), fixed),
                  pl.BlockSpec((1, 6, D_MODEL), lambda i: (i // tiles_per_seq, 0, 0)),
                  pl.BlockSpec((1, D_MODEL), fixed),
                  pl.BlockSpec((1, D_MODEL), fixed),
                  pl.BlockSpec((2 * N_EXPERTS, D_MODEL), fixed),
                  pl.BlockSpec((N_EXPERTS, 1), fixed),
                  pl.BlockSpec((MIX_SUB, MIX_SUB), fixed)],
        out_specs=[pl.BlockSpec((tm, D_MODEL), row),
                   pl.BlockSpec((tm * ROW_SUB, LANE), row),
                   pl.BlockSpec((TOP_K, tm), tok),
                   pl.BlockSpec((tm, LANE), row),
                   pl.BlockSpec((TOP_K, tm), tok),
                   pl.BlockSpec((N_EXPERTS, LANE), fixed)],
        out_shape=[jax.ShapeDtypeStruct((T, D_MODEL), F32),
                   jax.ShapeDtypeStruct((T * ROW_SUB, LANE), I32),
                   jax.ShapeDtypeStruct((TOP_K, T), I32),
                   jax.ShapeDtypeStruct((T, LANE), F32),
                   jax.ShapeDtypeStruct((TOP_K, T), I32),
                   jax.ShapeDtypeStruct((N_EXPERTS, LANE), I32)],
        scratch_shapes=[pltpu.VMEM((N_EXPERTS, LANE), F32)],
        compiler_params=pltpu.CompilerParams(dimension_semantics=("arbitrary",)),
        name="mix",
    )(x2, ypool, attn, wop, woa, mod3, g, b, wr_t, rb, tri)


def _slab(r):
    return pl.ds(pl.multiple_of(r * ROW_SUB, ROW_SUB), ROW_SUB)


def _dest_kernel(ps_ref, idx_ref, rank_ref, o_ref):
    idx = idx_ref[...]
    d = rank_ref[...]
    for e in range(N_EXPERTS):
        d = d + jnp.where(idx == e, ps_ref[e], 0)
    tm = o_ref.shape[1]
    for j in range(idx.shape[1] // tm):
        o_ref[j * TOP_K:(j + 1) * TOP_K, :] = d[:, j * tm:(j + 1) * tm]


def _dest_call(pad_start, idx_t, rank_t):
    T = idx_t.shape[1]
    tm = TM_ROUTE
    td = 2048 if T % 2048 == 0 else tm
    per = td // tm
    grid_spec = pltpu.PrefetchScalarGridSpec(
        num_scalar_prefetch=1,
        grid=(T // td,),
        in_specs=[pl.BlockSpec((TOP_K, td), lambda i, ps: (0, i)),
                  pl.BlockSpec((TOP_K, td), lambda i, ps: (0, i))],
        out_specs=pl.BlockSpec((per * TOP_K, tm), lambda i, ps: (i, 0)),
    )
    return pl.pallas_call(
        _dest_kernel,
        grid_spec=grid_spec,
        out_shape=jax.ShapeDtypeStruct((T // tm * TOP_K, tm), I32),
        name="dest",
    )(pad_start, idx_t, rank_t)


def _dispatch_kernel(pst_ref, plen_ref, nreal_ref, dest_hbm, hp_ref, hp_hbm, xs_hbm, dst0, dst1, zbuf, isem, ssem,
                     zsem):
    tm = hp_ref.shape[0] // ROW_SUB
    bm = zbuf.shape[0] // ROW_SUB
    n_idx = TOP_K * tm
    nb = xs_hbm.shape[0] // (bm * ROW_SUB)
    i = pl.program_id(0)
    nt = pl.num_programs(0)
    dst = (dst0, dst1)

    def idx_copy(tile, s):
        off = pl.multiple_of(tile * n_idx, n_idx)
        return pltpu.make_async_copy(dest_hbm.at[pl.ds(off, n_idx)], dst[s], isem.at[s])

    def pad_fill(e):
        n = plen_ref[e] * ROW_SUB
        start = pl.multiple_of(pst_ref[e] * ROW_SUB, ROW_SUB)
        return pltpu.make_async_copy(zbuf.at[pl.ds(0, n)], xs_hbm.at[pl.ds(start, n)], zsem)

    def dead_fill(j):
        return pltpu.make_async_copy(zbuf, xs_hbm.at[pl.ds(j * bm * ROW_SUB, bm * ROW_SUB)], zsem)

    def fills(act):
        for e in range(N_EXPERTS):
            pl.when(plen_ref[e] > 0)(functools.partial(lambda e: act(pad_fill(e)), e))
        for j in range(nb - N_EXPERTS, nb):
            pl.when(j >= nreal_ref[0])(functools.partial(lambda j: act(dead_fill(j)), j))

    @pl.when(i == 0)
    def _():
        zbuf[...] = jnp.zeros_like(zbuf)
        idx_copy(0, 0).start()
        idx_copy(0, 0).wait()
        fills(lambda cp: cp.start())

    def scatter(s):
        @pl.when(i + 1 < nt)
        def _():
            idx_copy(i + 1, 1 - s).start()

        def body(rb, c):
            for rr in range(SUBLANE):
                r = rb * SUBLANE + rr
                for k in range(TOP_K):
                    src = hp_ref.at[_slab(r)] if k < TOP_K // 2 else hp_hbm.at[_slab(i * tm + r)]
                    pltpu.make_async_copy(src, xs_hbm.at[_slab(dst[s][k * tm + r])], ssem).start(priority=k % 2)
            return c
        lax.fori_loop(0, tm // SUBLANE, body, 0)
        for k in range(TOP_K):
            pltpu.make_async_copy(hp_ref, xs_hbm.at[pl.ds(0, tm * ROW_SUB)], ssem).wait()

        @pl.when(i + 1 < nt)
        def _():
            idx_copy(i + 1, 1 - s).wait()

    for s in range(2):
        pl.when(i % 2 == s)(functools.partial(scatter, s))

    @pl.when(i == 0)
    def _():
        fills(lambda cp: cp.wait())


def _dispatch_call(pad_row_start, pad_len, n_real, dest, hp, nb):
    tm = TM_ROUTE
    bm = BM_MOE
    T = hp.shape[0] // ROW_SUB
    grid_spec = pltpu.PrefetchScalarGridSpec(
        num_scalar_prefetch=3,
        grid=(T // tm,),
        in_specs=[pl.BlockSpec(memory_space=pl.ANY),
                  pl.BlockSpec((tm * ROW_SUB, LANE), lambda i, a, b, c: (i, 0)),
                  pl.BlockSpec(memory_space=pl.ANY)],
        out_specs=pl.BlockSpec(memory_space=pl.ANY),
        scratch_shapes=[pltpu.SMEM((TOP_K * tm,), I32),
                        pltpu.SMEM((TOP_K * tm,), I32),
                        pltpu.VMEM((bm * ROW_SUB, LANE), I32),
                        pltpu.SemaphoreType.DMA((2,)),
                        pltpu.SemaphoreType.DMA,
                        pltpu.SemaphoreType.DMA],
    )
    return pl.pallas_call(
        _dispatch_kernel,
        grid_spec=grid_spec,
        out_shape=jax.ShapeDtypeStruct((nb * bm * ROW_SUB, LANE), I32),
        compiler_params=pltpu.CompilerParams(dimension_semantics=("arbitrary",)),
        name="dispatch",
    )(pad_row_start, pad_len, n_real, dest, hp, hp)


def _moe_kernel(be_ref, nreal_ref, x_ref, wg_ref, wu_ref, wd_ref, y_ref, wgb, wub, wdb):
    bm = x_ref.shape[0] // ROW_SUB
    i = pl.program_id(0)
    n = nreal_ref[0]

    @pl.when(i < n)
    def _():
        e_changed = (i == 0) | (be_ref[i] != be_ref[jnp.maximum(i - 1, 0)])

        @pl.when(e_changed)
        def _():
            wgb[...] = wg_ref[0].astype(BF16)
            wub[...] = wu_ref[0].astype(BF16)
            wdb[...] = wd_ref[0].astype(BF16)

        cm = bm // MOE_ROW_CHUNKS

        def up(c):
            halves = [_unpack2(w) for w in _load_rows(x_ref, cm, c * cm)]
            xb = jnp.concatenate([lo.astype(BF16) for lo, _ in halves] + [hi.astype(BF16) for _, hi in halves],
                                 axis=1)
            return (_silu(_dot(xb, wgb[...])) * _dot(xb, wub[...])).astype(BF16)

        nxt = up(0)
        for c in range(MOE_ROW_CHUNKS):
            a = nxt
            if c + 1 < MOE_ROW_CHUNKS:
                nxt = up(c + 1)
            y = _dot(a, wdb[...])
            _store_rows(y_ref, _pack2(y[:, :HALF_D], y[:, HALF_D:]), c * cm)

    @pl.when(i >= n)
    def _():
        y_ref[...] = jnp.zeros_like(y_ref)


def _moe_call(block_expert, n_real, xs, w_gate, w_up, w_down):
    bm = BM_MOE
    nb = block_expert.shape[0]
    wmap = lambda i, be, nr: (be[i], 0, 0)
    rows = pl.BlockSpec((bm * ROW_SUB, LANE), lambda i, be, nr: (i, 0))
    grid_spec = pltpu.PrefetchScalarGridSpec(
        num_scalar_prefetch=2,
        grid=(nb,),
        in_specs=[rows,
                  pl.BlockSpec((1, D_MODEL, EXPERT_FF), wmap),
                  pl.BlockSpec((1, D_MODEL, EXPERT_FF), wmap),
                  pl.BlockSpec((1, EXPERT_FF, D_MODEL), wmap)],
        out_specs=rows,
        scratch_shapes=[pltpu.VMEM((D_MODEL, EXPERT_FF), BF16),
                        pltpu.VMEM((D_MODEL, EXPERT_FF), BF16),
                        pltpu.VMEM((EXPERT_FF, D_MODEL), BF16)],
    )
    return pl.pallas_call(
        _moe_kernel,
        grid_spec=grid_spec,
        out_shape=jax.ShapeDtypeStruct(xs.shape, I32),
        compiler_params=pltpu.CompilerParams(dimension_semantics=("arbitrary",)),
        name="moe",
    )(block_expert, n_real, xs, w_gate, w_up, w_down)


def _combine_kernel(dest_hbm, ys_hbm, wc_ref, xmid_ref, mod_ref, wsg_ref, wsu_ref, wsd_ref, g_ref, b_ref,
                    o_ref, dst0, dst1, rbuf, isem, gsem):
    tm = xmid_ref.shape[0]
    n_idx = TOP_K * tm
    i = pl.program_id(0)
    nt = pl.num_programs(0)
    dst = (dst0, dst1)

    def idx_copy(tile, s):
        off = pl.multiple_of(tile * n_idx, n_idx)
        return pltpu.make_async_copy(dest_hbm.at[pl.ds(off, n_idx)], dst[s], isem.at[s])

    def gather_copy(s, k, r, src_row):
        return pltpu.make_async_copy(ys_hbm.at[_slab(src_row)], rbuf.at[s, k, pl.ds(r * ROW_SUB, ROW_SUB)],
                                     gsem.at[s])

    def start_gather(s):
        for r in range(tm):
            for k in range(TOP_K):
                gather_copy(s, k, r, dst[s][k * tm + r]).start(priority=k % 2)

    def wait_gather(s):
        for k in range(TOP_K):
            pltpu.make_async_copy(ys_hbm.at[pl.ds(0, tm * ROW_SUB)], rbuf.at[s, k], gsem.at[s]).wait()

    @pl.when(i == 0)
    def _():
        for s in range(2):
            idx_copy(jnp.minimum(s, nt - 1), s).start()
        for s in range(2):
            idx_copy(jnp.minimum(s, nt - 1), s).wait()

        def body(rb, c):
            for rr in range(SUBLANE):
                r = rb * SUBLANE + rr
                for k in range(TOP_K):
                    pltpu.make_async_copy(ys_hbm.at[_slab(dst0[k * tm + r])], rbuf.at[0, k, _slab(r)],
                                          gsem.at[0]).start(priority=k % 2)
            return c
        lax.fori_loop(0, tm // SUBLANE, body, 0)

    def finish(s):
        wc = wc_ref[...]
        acc = None
        for k in range(TOP_K):
            parts = [_unpack2(w) for w in _load_rows(rbuf.at[s, k], tm)]
            parts = [lo for lo, _ in parts] + [hi for _, hi in parts]
            wk = wc[:, k:k + 1]
            acc = [wk * p for p in parts] if acc is None else [a + wk * p for a, p in zip(acc, parts)]
        routed = jnp.concatenate(acc, axis=1)
        xmid = xmid_ref[...]
        sh2 = mod_ref[0, 3:4, :]
        sc2 = mod_ref[0, 4:5, :]
        g2 = mod_ref[0, 5:6, :]
        hb = (xmid * (1.0 + sc2) + sh2).astype(BF16)
        a = _silu(_dot(hb, wsg_ref[...])) * _dot(hb, wsu_ref[...])
        shared = _dot(a.astype(BF16), wsd_ref[...])
        o_ref[...] = _layer_norm(ALPHA * xmid + g2 * (routed + shared), g_ref[...], b_ref[...])

    def step(s):
        @pl.when((i >= 1) & (i + 1 < nt))
        def _():
            idx_copy(i + 1, 1 - s).wait()

        wait_gather(s)
        start_gather(1 - s)
        finish(s)

        @pl.when(i + 2 < nt)
        def _():
            idx_copy(i + 2, s).start()

        @pl.when(i == nt - 1)
        def _():
            wait_gather(1 - s)

    for s in range(2):
        pl.when(i % 2 == s)(functools.partial(step, s))


def _combine_call(dest, ys, wcol, xmid, mod3, wsg, wsu, wsd, g, b, tiles_per_seq):
    T = xmid.shape[0]
    tm = TM_ROUTE
    fixed = lambda i: (0, 0)
    row = lambda i: (i, 0)
    return pl.pallas_call(
        _combine_kernel,
        grid=(T // tm,),
        in_specs=[pl.BlockSpec(memory_space=pl.ANY),
                  pl.BlockSpec(memory_space=pl.ANY),
                  pl.BlockSpec((tm, LANE), row),
                  pl.BlockSpec((tm, D_MODEL), row),
                  pl.BlockSpec((1, 6, D_MODEL), lambda i: (i // tiles_per_seq, 0, 0)),
                  pl.BlockSpec((D_MODEL, SHARED_FF), fixed),
                  pl.BlockSpec((D_MODEL, SHARED_FF), fixed),
                  pl.BlockSpec((SHARED_FF, D_MODEL), fixed),
                  pl.BlockSpec((1, D_MODEL), fixed),
                  pl.BlockSpec((1, D_MODEL), fixed)],
        out_specs=pl.BlockSpec((tm, D_MODEL), row),
        out_shape=jax.ShapeDtypeStruct((T, D_MODEL), F32),
        scratch_shapes=[pltpu.SMEM((TOP_K * tm,), I32),
                        pltpu.SMEM((TOP_K * tm,), I32),
                        pltpu.VMEM((2, TOP_K, tm * ROW_SUB, LANE), I32),
                        pltpu.SemaphoreType.DMA((2,)),
                        pltpu.SemaphoreType.DMA((2,))],
        compiler_params=pltpu.CompilerParams(dimension_semantics=("arbitrary",)),
        name="combine",
    )(dest, ys, wcol, xmid, mod3, wsg, wsu, wsd, g, b)


def _group(a, b):
    ref = a if a is not None else b
    z = jnp.zeros(ref.shape[:-1], ref.dtype)
    a0, a1 = (a[..., 0], a[..., 1]) if a is not None else (z, z)
    b0, b1 = (b[..., 0], b[..., 1]) if b is not None else (z, z)
    return jnp.concatenate([a0, b0, a1, b1], axis=-1)


def _q_groups(per_head):
    return jnp.concatenate([_group(per_head(3 * (2 * p) + j), per_head(3 * (2 * p + 1) + j))
                            for p in range(N_KV_HEADS // 2) for j in range(GQA_GROUP)], axis=-1)


def _k_groups(per_head):
    return jnp.concatenate([_group(per_head(h), None) if h % 2 == 0 else _group(None, per_head(h))
                            for h in range(N_KV_HEADS)], axis=-1)


def _rope_pairs(seq):
    inv_freq = ROPE_THETA ** (-jnp.arange(0, HALF, 2, dtype=F32) / HALF)
    pos = jnp.arange(seq)
    rowp = (pos // GRID_W).astype(F32)
    colp = (pos % GRID_W).astype(F32)
    ang = jnp.concatenate([rowp[:, None] * inv_freq, colp[:, None] * inv_freq], axis=-1)
    cos, sin = jnp.cos(ang), jnp.sin(ang)
    return jnp.stack([cos, cos], axis=-1), jnp.stack([-sin, sin], axis=-1)


def kernel(x, c, ctx, c_ctx, w_mod, b_mod, w_in, q_norm, k_norm, pool_w, pool_scale, w_out, ln1_g, ln1_b,
           w_router, router_bias, w_gate, w_up, w_down, ws_gate, ws_up, ws_down, ln2_g, ln2_b):
    B, S, D = x.shape
    C = ctx.shape[1]
    T = B * S
    assert D == D_MODEL and w_mod.shape[0] == DEPTH and B + 1 <= MOD_ROWS
    assert S % TM_INPROJ == 0 and S % TQ_ATTN == 0 and S % TM_MIX == 0 and S % TM_ROUTE == 0
    assert S % GRID_W == 0 and C % SUBLANE == 0

    cc = jnp.concatenate([c, c_ctx[None, :], jnp.zeros((MOD_ROWS - B - 1, D), F32)], axis=0)
    mod3 = _mod_call(cc, w_mod[0], b_mod[0][None, :]).reshape(MOD_ROWS, 6, D)

    w = w_in[0]
    o1, o2, o3 = POOL_WIDTH, POOL_WIDTH + ATTN_WIDTH, POOL_WIDTH + ATTN_WIDTH + KV_WIDTH
    wq4 = w[:, o1:o2].reshape(D, N_HEADS, HALF, 2)
    wk4 = w[:, o2:o3].reshape(D, N_KV_HEADS, HALF, 2)
    wq = _q_groups(lambda h: wq4[:, h])
    wk = _k_groups(lambda h: wk4[:, h])
    w_all = jnp.concatenate([w[:, :o1], wq, wk, w[:, o3:]], axis=1).astype(BF16)
    w_kv = jnp.concatenate([wk, w[:, o3:]], axis=1).astype(BF16)
    qg4, kg4 = q_norm[0].reshape(1, HALF, 2), k_norm[0].reshape(1, HALF, 2)
    qg = _group(qg4, qg4)
    kg = _k_groups(lambda h: kg4)
    cos_p, sin_p = _rope_pairs(S)
    qscale = HEAD_DIM ** -0.5 * LOG2_E
    cq, sq = _group(cos_p, cos_p) * qscale, _group(sin_p, sin_p) * qscale
    ck = jnp.concatenate([_group(cos_p, None), _group(None, cos_p)], axis=-1)
    sk = jnp.concatenate([_group(sin_p, None), _group(None, sin_p)], axis=-1)
    slot = (jnp.arange(LANE) // HALF) % 2
    same = (slot[:, None] == slot[None, :]).astype(F32) * (1.0 / HEAD_DIM)
    gg = jnp.concatenate([same, same], axis=0).astype(BF16)
    u, q, k_l, v_l = _inproj_call(x, mod3, w_all, gg, qg, kg, cq, sq, ck, sk)
    k_c, v_c = _ctx_kv_call(ctx, mod3, w_kv, kg, B)

    k_all = jnp.concatenate([k_c, k_l], axis=1)
    v_all = jnp.concatenate([v_c, v_l], axis=1).reshape(B, C + S, N_KV_HEADS, HEAD_DIM)
    vt = jnp.transpose(v_all, (0, 2, 3, 1))
    ones = jnp.ones((B, N_KV_HEADS, 1, C + S), BF16)
    pad = jnp.zeros((B, N_KV_HEADS, VT_ROWS - HEAD_DIM - 1, C + S), BF16)
    attn = _attn_call(q, k_all, jnp.concatenate([vt, ones, pad], axis=2))

    bd = jax.scipy.linalg.block_diag(*[pool_w[0, g] for g in range(len(POOL_WINDOWS))]).astype(BF16)
    ypool = _pool_call(u, bd, pool_scale[0][None, :])

    wo = w_out[0]
    wop = wo[:POOL_WIDTH].astype(BF16)
    woa = wo[POOL_WIDTH:].reshape(N_KV_HEADS, GQA_GROUP * HEAD_DIM, D)
    woa = jnp.pad(woa, ((0, 0), (0, 2 * LANE - GQA_GROUP * HEAD_DIM), (0, 0))).reshape(ATTN_PAD, D).astype(BF16)
    tri = (jnp.arange(MIX_SUB)[:, None] < jnp.arange(MIX_SUB)[None, :]).astype(BF16)
    wr = w_router[0].T
    wr_hi = wr.astype(BF16)
    wr_lo = (wr - wr_hi.astype(F32)).astype(BF16)
    xmid, hp, idx_t, wcol, rank_t, counts = _mix_call(
        x.reshape(T, D), ypool.reshape(T, POOL_WIDTH), attn.reshape(T, ATTN_PAD), wop, woa, mod3,
        ln1_g[0][None, :], ln1_b[0][None, :], jnp.concatenate([wr_hi, wr_lo], axis=0),
        router_bias[0][:, None], tri, S // TM_MIX)

    bm = BM_MOE
    counts = counts[:, 0]
    padded = ((counts + bm - 1) // bm) * bm
    pad_end = jnp.cumsum(padded)
    pad_start = pad_end - padded
    nb = T * TOP_K // bm + N_EXPERTS
    n_real = (pad_end[-1] // bm).astype(I32).reshape(1)
    blk_row = jnp.arange(nb, dtype=I32) * bm
    block_expert = jnp.minimum(jnp.sum((pad_end[None, :] <= blk_row[:, None]).astype(I32), axis=1), N_EXPERTS - 1)

    dest = _dest_call(pad_start.astype(I32), idx_t, rank_t).reshape(-1)
    xs = _dispatch_call((pad_start + counts).astype(I32), (padded - counts).astype(I32), n_real, dest, hp, nb)
    ys = _moe_call(block_expert.astype(I32), n_real, xs, w_gate[0], w_up[0], w_down[0])
    out = _combine_call(dest, ys, wcol, xmid, mod3, ws_gate[0].astype(BF16), ws_up[0].astype(BF16),
                        ws_down[0].astype(BF16), ln2_g[0][None, :], ln2_b[0][None, :], S // TM_ROUTE)
    return out.reshape(B, S, D)
```

```python
import functools

import jax
import jax.numpy as jnp
from jax import lax
from jax.experimental import pallas as pl
from jax.experimental.pallas import tpu as pltpu

F32 = jnp.float32
BF16 = jnp.bfloat16
I32 = jnp.int32

LANE = 128
SUBLANE = 8

D_MODEL = 1024
GRID_W = 64
POOL_WIDTH = 256
POOL_WINDOWS = (2, 4, 8, 16)
POOL_GROUP = 64
HEAD_DIM = 64
HALF = HEAD_DIM // 2
N_HEADS = 12
N_KV_HEADS = 4
GQA_GROUP = N_HEADS // N_KV_HEADS
ATTN_WIDTH = N_HEADS * HEAD_DIM
KV_WIDTH = N_KV_HEADS * HEAD_DIM
ROPE_THETA = 10000.0
N_EXPERTS = 64
TOP_K = 8
N_GROUPS = 8
GROUP_SIZE = N_EXPERTS // N_GROUPS
TOPK_GROUPS = 4
EXPERT_FF = 256
SHARED_FF = 256
ROUTED_SCALE = 2.5
DEPTH = 1
ALPHA = (2.0 * DEPTH) ** 0.25
LN_EPS = 1e-5
RMS_EPS = 1e-6

MOD_ROWS = 24
HALF_D = D_MODEL // 2
ROW_SUB = HALF_D // LANE
Q_COLS = N_HEADS // 2 * LANE
K_COLS = N_KV_HEADS * LANE
ATTN_PAD = N_KV_HEADS * 2 * LANE
VT_ROWS = HEAD_DIM + 16
LOG2_E = 1.4426950408889634

TN_MOD = 512
TD_DEST = 2048
TM_INPROJ = 512
TQ_ATTN = 512
ATTN_SUB = 256
TM_MIX = 512
MIX_SUB = 256
BM_MOE = 1024
MOE_ROW_CHUNKS = 4
TM_ROUTE = 256

_NT = (((1,), (1,)), ((), ()))


def _dot(a, b):
    return jnp.dot(a, b, preferred_element_type=F32)


def _pack2(lo, hi):
    lo_bits = lax.bitcast_convert_type(lo.astype(BF16).astype(F32), I32)
    hi_bits = lax.bitcast_convert_type(hi.astype(BF16).astype(F32), I32)
    return lax.shift_right_logical(lo_bits, 16) | (hi_bits & jnp.int32(-65536))


def _unpack2(w):
    lo = lax.bitcast_convert_type(lax.shift_left(w, 16), F32)
    hi = lax.bitcast_convert_type(w & jnp.int32(-65536), F32)
    return lo, hi


def _store_rows(ref, packed, row0=0):
    n = packed.shape[0]
    for s in range(ROW_SUB):
        ref[pl.ds(row0 * ROW_SUB + s, n, stride=ROW_SUB), :] = packed[:, s * LANE:(s + 1) * LANE]


def _load_rows(ref, n, row0=0):
    return [ref[pl.ds(row0 * ROW_SUB + s, n, stride=ROW_SUB), :] for s in range(ROW_SUB)]


def _silu(x):
    return x * jax.nn.sigmoid(x)


def _layer_norm(r, g, b):
    mu = jnp.mean(r, axis=-1, keepdims=True)
    d = r - mu
    var = jnp.mean(d * d, axis=-1, keepdims=True)
    return d * lax.rsqrt(var + LN_EPS) * g + b


def _mod_kernel(c_ref, w_ref, b_ref, o_ref):
    a = _silu(c_ref[...])
    o_ref[...] = jnp.dot(a, w_ref[...], precision=lax.Precision.HIGHEST,
                         preferred_element_type=F32) + b_ref[...]


def _mod_call(cc, w_mod, b_mod):
    n = w_mod.shape[1]
    tn = TN_MOD
    return pl.pallas_call(
        _mod_kernel,
        grid=(n // tn,),
        in_specs=[pl.BlockSpec((MOD_ROWS, D_MODEL), lambda j: (0, 0)),
                  pl.BlockSpec((D_MODEL, tn), lambda j: (0, j)),
                  pl.BlockSpec((1, tn), lambda j: (0, j))],
        out_specs=pl.BlockSpec((MOD_ROWS, tn), lambda j: (0, j)),
        out_shape=jax.ShapeDtypeStruct((MOD_ROWS, n), F32),
        name="mod",
    )(cc, w_mod, b_mod)


def _norm_head(seg, g):
    ms = jnp.sum(seg * seg, axis=-1, keepdims=True) * (1.0 / HEAD_DIM)
    return seg * lax.rsqrt(ms + RMS_EPS) * g


def _norm_rope(seg, gg, g, c, s):
    sq = seg * seg
    hi = sq.astype(BF16)
    lo = (sq - hi.astype(F32)).astype(BF16)
    ms = _dot(jnp.concatenate([hi, lo], axis=1), gg)
    xn = seg * lax.rsqrt(ms + RMS_EPS) * g
    return xn * c + pltpu.roll(xn, LANE // 2, axis=1) * s


def _inproj_kernel(x_ref, mod_ref, w_ref, gg_ref, qg_ref, kg_ref, cq_ref, sq_ref, ck_ref, sk_ref,
                   u_ref, q_ref, k_ref, v_ref):
    sh = mod_ref[0, 0:1, :]
    sc = mod_ref[0, 1:2, :]
    xm = (x_ref[0] * (1.0 + sc) + sh).astype(BF16)
    gg, qg = gg_ref[...], qg_ref[...]
    cq, sq = cq_ref[...], sq_ref[...]
    n_qp = Q_COLS // (2 * LANE)
    n_kp = K_COLS // (2 * LANE)
    n_chunks = 1 + n_qp + n_kp + 1

    def chunk(c):
        return _dot(xm, w_ref[:, c * 2 * LANE:(c + 1) * 2 * LANE])

    nxt = chunk(0)
    for c in range(n_chunks):
        p = nxt
        if c + 1 < n_chunks:
            nxt = chunk(c + 1)
        if c == 0:
            u_ref[0] = p
        elif c <= n_qp:
            for j in range(2):
                grp = (c - 1) * 2 + j
                q = _norm_rope(p[:, j * LANE:(j + 1) * LANE], gg, qg, cq, sq)
                q_ref[0, :, grp * LANE:(grp + 1) * LANE] = q.astype(BF16)
        elif c <= n_qp + n_kp:
            for j in range(2):
                h = (c - 1 - n_qp) * 2 + j
                k = _norm_rope(p[:, j * LANE:(j + 1) * LANE], gg, kg_ref[:, h * LANE:(h + 1) * LANE],
                               ck_ref[:, j * LANE:(j + 1) * LANE], sk_ref[:, j * LANE:(j + 1) * LANE])
                k_ref[0, :, h * LANE:(h + 1) * LANE] = k.astype(BF16)
        else:
            v_ref[0] = p.astype(BF16)


def _inproj_call(x, mod3, w_all, gg, qg, kg, cq, sq, ck, sk):
    B, S, _ = x.shape
    tm = TM_INPROJ
    ncol = w_all.shape[1]
    tab = pl.BlockSpec((tm, LANE), lambda b, i: (i, 0))
    tab2 = pl.BlockSpec((tm, 2 * LANE), lambda b, i: (i, 0))
    vec = pl.BlockSpec((1, LANE), lambda b, i: (0, 0))
    return pl.pallas_call(
        _inproj_kernel,
        grid=(B, S // tm),
        in_specs=[pl.BlockSpec((1, tm, D_MODEL), lambda b, i: (b, i, 0)),
                  pl.BlockSpec((1, 6, D_MODEL), lambda b, i: (b, 0, 0)),
                  pl.BlockSpec((D_MODEL, ncol), lambda b, i: (0, 0)),
                  pl.BlockSpec((2 * LANE, LANE), lambda b, i: (0, 0)),
                  vec, pl.BlockSpec((1, K_COLS), lambda b, i: (0, 0)), tab, tab, tab2, tab2],
        out_specs=[pl.BlockSpec((1, tm, POOL_WIDTH), lambda b, i: (b, i, 0)),
                   pl.BlockSpec((1, tm, Q_COLS), lambda b, i: (b, i, 0)),
                   pl.BlockSpec((1, tm, K_COLS), lambda b, i: (b, i, 0)),
                   pl.BlockSpec((1, tm, KV_WIDTH), lambda b, i: (b, i, 0))],
        out_shape=[jax.ShapeDtypeStruct((B, S, POOL_WIDTH), F32),
                   jax.ShapeDtypeStruct((B, S, Q_COLS), BF16),
                   jax.ShapeDtypeStruct((B, S, K_COLS), BF16),
                   jax.ShapeDtypeStruct((B, S, KV_WIDTH), BF16)],
        name="inproj",
    )(x, mod3, w_all, gg, qg, kg, cq, sq, ck, sk)


def _ctx_kv_kernel(x_ref, mod_ref, w_ref, kg_ref, k_ref, v_ref):
    sh = mod_ref[0, 0:1, :]
    sc = mod_ref[0, 1:2, :]
    xm = (x_ref[0] * (1.0 + sc) + sh).astype(BF16)
    for pair in range(N_KV_HEADS // 2):
        p = _dot(xm, w_ref[:, pair * 2 * LANE:(pair + 1) * 2 * LANE])
        for j in range(2):
            h = pair * 2 + j
            kn = _norm_head(p[:, j * LANE:(j + 1) * LANE], kg_ref[:, h * LANE:(h + 1) * LANE])
            k_ref[0, :, h * LANE:(h + 1) * LANE] = kn.astype(BF16)
    v_ref[0] = _dot(xm, w_ref[:, K_COLS:K_COLS + KV_WIDTH]).astype(BF16)


def _ctx_kv_call(ctx, mod3, w_kv, kg, ctx_row):
    B, C, _ = ctx.shape
    return pl.pallas_call(
        _ctx_kv_kernel,
        grid=(B,),
        in_specs=[pl.BlockSpec((1, C, D_MODEL), lambda b: (b, 0, 0)),
                  pl.BlockSpec((1, 6, D_MODEL), lambda b: (ctx_row, 0, 0)),
                  pl.BlockSpec((D_MODEL, K_COLS + KV_WIDTH), lambda b: (0, 0)),
                  pl.BlockSpec((1, K_COLS), lambda b: (0, 0))],
        out_specs=[pl.BlockSpec((1, C, K_COLS), lambda b: (b, 0, 0)),
                   pl.BlockSpec((1, C, KV_WIDTH), lambda b: (b, 0, 0))],
        out_shape=[jax.ShapeDtypeStruct((B, C, K_COLS), BF16),
                   jax.ShapeDtypeStruct((B, C, KV_WIDTH), BF16)],
        name="ctx_kv",
    )(ctx, mod3, w_kv, kg)


POOL_PAD = 8


def _pool_kernel(u_ref, bd_ref, ps_ref, y_ref):
    S = u_ref.shape[1]
    n = S + 2 * POOL_PAD
    t = lax.broadcasted_iota(I32, (S, LANE), 0)
    lane = lax.broadcasted_iota(I32, (S, LANE), 1)
    zpad = jnp.zeros((POOL_PAD, LANE), F32)
    for half in range(POOL_WIDTH // LANE):
        u = u_ref[0, :, half * LANE:(half + 1) * LANE]
        ue = jnp.concatenate([zpad, u, zpad], axis=0)
        fwd = {1: ue}
        w = 1
        while w < POOL_WINDOWS[2 * half + 1]:
            fwd[2 * w] = fwd[w] + pltpu.roll(fwd[w], n - w, axis=0)
            w *= 2
        ds = []
        for win in POOL_WINDOWS[2 * half: 2 * half + 2]:
            hw = win // 2
            centred = pltpu.roll(fwd[win], hw, axis=0)[POOL_PAD:POOL_PAD + S]
            cnt = (jnp.minimum(t + hw, S) - jnp.maximum(t - hw, 0)).astype(F32)
            ds.append(centred / cnt - u)
        d = jnp.where(lane < POOL_GROUP, ds[0], ds[1]).astype(BF16)
        sl = slice(half * LANE, (half + 1) * LANE)
        y = _dot(d, bd_ref[sl, sl]) * ps_ref[:, sl]
        y_ref[0, :, sl] = y.astype(BF16)


def _pool_call(u, bd, ps):
    B, S, _ = u.shape
    return pl.pallas_call(
        _pool_kernel,
        grid=(B,),
        in_specs=[pl.BlockSpec((1, S, POOL_WIDTH), lambda b: (b, 0, 0)),
                  pl.BlockSpec((POOL_WIDTH, POOL_WIDTH), lambda b: (0, 0)),
                  pl.BlockSpec((1, POOL_WIDTH), lambda b: (0, 0))],
        out_specs=pl.BlockSpec((1, S, POOL_WIDTH), lambda b: (b, 0, 0)),
        out_shape=jax.ShapeDtypeStruct((B, S, POOL_WIDTH), BF16),
        name="pool",
    )(u, bd, ps)


ROW_REDUCE_WAYS = 16


def _reduce_rows(op, x):
    rows, lanes = x.shape
    if rows % (ROW_REDUCE_WAYS * SUBLANE) == 0:
        x = op(x.reshape(ROW_REDUCE_WAYS, rows // ROW_REDUCE_WAYS, lanes), axis=0)
    return op(x, axis=0, keepdims=True)


def _attn_kernel(q_ref, k_ref, vt_ref, o_ref, s_buf, m_buf):
    t = pl.program_id(0)
    ts = s_buf.shape[-1]
    n_sub = q_ref.shape[1] // ts

    @pl.when(t == 0)
    def _():
        s_buf[1] = jnp.zeros_like(s_buf[1])
        m_buf[1] = jnp.zeros_like(m_buf[1])

    def step(slot):
        k = k_ref[0]
        vt = vt_ref[0, 0]
        for u in range(n_sub):
            cols = pl.ds(u * ts, ts)
            outs = []
            for j in range(GQA_GROUP):
                s = lax.dot_general(k, q_ref[0, cols, j * LANE:(j + 1) * LANE], _NT, preferred_element_type=F32)
                s_buf[slot, u, j] = s
                m_buf[slot, u, j] = jnp.broadcast_to(_reduce_rows(jnp.max, s), (SUBLANE, ts))
                p = jnp.exp2(s_buf[1 - slot, u, j] - m_buf[1 - slot, u, j, 0:1, :])
                ol = _dot(vt, p.astype(BF16))
                outs.append(ol[:HEAD_DIM] / ol[HEAD_DIM:HEAD_DIM + 1])
            outs.append(jnp.zeros((2 * LANE - GQA_GROUP * HEAD_DIM, ts), F32))
            o_ref[0, cols, :] = jnp.concatenate(outs, axis=0).T.astype(BF16)

    for slot in range(2):
        pl.when(t % 2 == slot)(functools.partial(step, slot))


def _attn_call(q, k_all, vt_all):
    B, S, _ = q.shape
    Lk = k_all.shape[1]
    tq = TQ_ATTN
    nq = S // tq
    n_items = B * N_KV_HEADS * nq

    def item(t):
        return t // (N_KV_HEADS * nq), (t // nq) % N_KV_HEADS, t % nq

    def cur(t):
        return item(jnp.minimum(t, n_items - 1))

    def prev(t):
        return item(jnp.maximum(t - 1, 0))

    return pl.pallas_call(
        _attn_kernel,
        grid=(n_items + 1,),
        in_specs=[pl.BlockSpec((1, tq, GQA_GROUP * LANE), lambda t: (cur(t)[0], cur(t)[2], cur(t)[1] // 2)),
                  pl.BlockSpec((1, Lk, LANE), lambda t: (cur(t)[0], 0, cur(t)[1])),
                  pl.BlockSpec((1, 1, VT_ROWS, Lk), lambda t: (prev(t)[0], prev(t)[1], 0, 0))],
        out_specs=pl.BlockSpec((1, tq, 2 * LANE), lambda t: (prev(t)[0], prev(t)[2], prev(t)[1])),
        out_shape=jax.ShapeDtypeStruct((B, S, ATTN_PAD), BF16),
        scratch_shapes=[pltpu.VMEM((2, tq // ATTN_SUB, GQA_GROUP, Lk, ATTN_SUB), F32),
                        pltpu.VMEM((2, tq // ATTN_SUB, GQA_GROUP, SUBLANE, ATTN_SUB), F32)],
        compiler_params=pltpu.CompilerParams(dimension_semantics=("arbitrary",)),
        name="attn",
    )(q, k_all, vt_all)


def _mix_kernel(x_ref, yp_ref, at_ref, wop_ref, woa_ref, mod_ref, g_ref, b_ref, wr_ref, rb_ref, tri_ref,
                xmid_ref, hp_ref, idx_ref, wts_ref, rank_ref, cnt_ref, carry):
    i = pl.program_id(0)
    tm = tri_ref.shape[0]
    n_sub = x_ref.shape[0] // tm

    @pl.when(i == 0)
    def _():
        carry[...] = jnp.zeros_like(carry)

    g1 = mod_ref[0, 2:3, :]
    sh2 = mod_ref[0, 3:4, :]
    sc2 = mod_ref[0, 4:5, :]

    def out_proj(s):
        rows = pl.ds(s * tm, tm)
        return _dot(yp_ref[rows, :], wop_ref[...]) + _dot(at_ref[rows, :], woa_ref[...])

    def norm_router(s, mix):
        rows = pl.ds(s * tm, tm)
        xmid = _layer_norm(ALPHA * x_ref[rows, :] + g1 * mix, g_ref[...], b_ref[...])
        xmid_ref[rows, :] = xmid
        h = xmid * (1.0 + sc2) + sh2
        _store_rows(hp_ref, _pack2(h[:, :HALF_D], h[:, HALF_D:]), s * tm)
        h_hi = h.astype(BF16)
        h_lo = (h - h_hi.astype(F32)).astype(BF16)
        a = lax.dot_general(wr_ref[...], h_hi, _NT, preferred_element_type=F32)
        b = lax.dot_general(wr_ref[0:N_EXPERTS, :], h_lo, _NT, preferred_element_type=F32)
        return a[:N_EXPERTS] + a[N_EXPERTS:] + b

    nxt = out_proj(0)
    logits = []
    for s in range(n_sub):
        mix = nxt
        if s + 1 < n_sub:
            nxt = out_proj(s + 1)
        logits.append(norm_router(s, mix))
    base = carry[:, 0:1]
    for s in range(n_sub):
        base = base + _route(logits[s], base, rb_ref, tri_ref, idx_ref, wts_ref, rank_ref, s * tm)
    carry[...] = jnp.broadcast_to(base, carry.shape)
    cnt_ref[...] = carry[...].astype(I32)


def _route(logits, base, rb_ref, tri_ref, idx_ref, wts_ref, rank_ref, tok0):
    tm = logits.shape[1]
    cols = pl.ds(tok0, tm)
    score = jax.nn.sigmoid(logits)
    sel = score + rb_ref[...]
    neg = jnp.float32(-jnp.inf)
    sub = lax.broadcasted_iota(I32, (GROUP_SIZE, tm), 0)
    grp_sel, grp_score = [], []
    for g in range(N_GROUPS):
        sg = sel[g * GROUP_SIZE:(g + 1) * GROUP_SIZE, :]
        m1 = jnp.max(sg, axis=0, keepdims=True)
        i1 = jnp.min(jnp.where(sg == m1, sub, GROUP_SIZE), axis=0, keepdims=True)
        m2 = jnp.max(jnp.where(sub == i1, neg, sg), axis=0, keepdims=True)
        grp_sel.append(sg)
        grp_score.append(m1 + m2)
    masked = []
    for g in range(N_GROUPS):
        ahead = jnp.zeros((1, tm), I32)
        for o in range(N_GROUPS):
            if o == g:
                continue
            beats = (grp_score[o] > grp_score[g]) | ((grp_score[o] == grp_score[g]) & (o < g))
            ahead = ahead + beats.astype(I32)
        masked.append(jnp.where(ahead < TOPK_GROUPS, grp_sel[g], neg))
    cur = jnp.concatenate(masked, axis=0)
    row = lax.broadcasted_iota(I32, (N_EXPERTS, tm), 0)
    member = jnp.zeros((N_EXPERTS, tm), F32)
    picks, wsel = [], []
    for k in range(TOP_K):
        mx = jnp.max(cur, axis=0, keepdims=True)
        ei = jnp.min(jnp.where(cur == mx, row, N_EXPERTS), axis=0, keepdims=True)
        hit = row == ei
        picks.append(ei)
        wsel.append(jnp.sum(jnp.where(hit, score, 0.0), axis=0, keepdims=True))
        cur = jnp.where(hit, neg, cur)
        member = jnp.where(hit, 1.0, member)
    tot = wsel[0]
    for k in range(1, TOP_K):
        tot = tot + wsel[k]
    before = _dot(member.astype(BF16), tri_ref[...]) + base
    for k in range(TOP_K):
        idx_ref[k:k + 1, cols] = picks[k]
        rk = jnp.sum(jnp.where(row == picks[k], before, 0.0), axis=0, keepdims=True)
        rank_ref[k:k + 1, cols] = rk.astype(I32)
    wrows = [wsel[k] / tot * ROUTED_SCALE for k in range(TOP_K)]
    wmat = jnp.concatenate(wrows + [jnp.zeros((LANE - TOP_K, tm), F32)], axis=0)
    wts_ref[cols, :] = wmat.T
    return jnp.sum(member, axis=1, keepdims=True)


def _mix_call(x2, ypool, attn, wop, woa, mod3, g, b, wr_t, rb, tri, tiles_per_seq):
    T = x2.shape[0]
    tm = TM_MIX
    row = lambda i: (i, 0)
    fixed = lambda i: (0, 0)
    tok = lambda i: (0, i)
    return pl.pallas_call(
        _mix_kernel,
        grid=(T // tm,),
        in_specs=[pl.BlockSpec((tm, D_MODEL), row),
                  pl.BlockSpec((tm, POOL_WIDTH), row),
                  pl.BlockSpec((tm, ATTN_PAD), row),
                  pl.BlockSpec((POOL_WIDTH, D_MODEL), fixed),
                  pl.BlockSpec((ATTN_PAD, D_MODEL), fixed),
                  pl.BlockSpec((1, 6, D_MODEL), lambda i: (i // tiles_per_seq, 0, 0)),
                  pl.BlockSpec((1, D_MODEL), fixed),
                  pl.BlockSpec((1, D_MODEL), fixed),
                  pl.BlockSpec((2 * N_EXPERTS, D_MODEL), fixed),
                  pl.BlockSpec((N_EXPERTS, 1), fixed),
                  pl.BlockSpec((MIX_SUB, MIX_SUB), fixed)],
        out_specs=[pl.BlockSpec((tm, D_MODEL), row),
                   pl.BlockSpec((tm * ROW_SUB, LANE), row),
                   pl.BlockSpec((TOP_K, tm), tok),
                   pl.BlockSpec((tm, LANE), row),
                   pl.BlockSpec((TOP_K, tm), tok),
                   pl.BlockSpec((N_EXPERTS, LANE), fixed)],
        out_shape=[jax.ShapeDtypeStruct((T, D_MODEL), F32),
                   jax.ShapeDtypeStruct((T * ROW_SUB, LANE), I32),
                   jax.ShapeDtypeStruct((TOP_K, T), I32),
                   jax.ShapeDtypeStruct((T, LANE), F32),
                   jax.ShapeDtypeStruct((TOP_K, T), I32),
                   jax.ShapeDtypeStruct((N_EXPERTS, LANE), I32)],
        scratch_shapes=[pltpu.VMEM((N_EXPERTS, LANE), F32)],
        compiler_params=pltpu.CompilerParams(dimension_semantics=("arbitrary",)),
        name="mix",
    )(x2, ypool, attn, wop, woa, mod3, g, b, wr_t, rb, tri)


def _slab(r):
    return pl.ds(pl.multiple_of(r * ROW_SUB, ROW_SUB), ROW_SUB)


def _dest_kernel(ps_ref, idx_ref, rank_ref, o_ref):
    idx = idx_ref[...]
    d = rank_ref[...]
    for e in range(N_EXPERTS):
        d = d + jnp.where(idx == e, ps_ref[e], 0)
    tm = o_ref.shape[1]
    for j in range(idx.shape[1] // tm):
        o_ref[j * TOP_K:(j + 1) * TOP_K, :] = d[:, j * tm:(j + 1) * tm]


def _dest_call(pad_start, idx_t, rank_t):
    T = idx_t.shape[1]
    tm = TM_ROUTE
    td = TD_DEST if T % TD_DEST == 0 else tm
    per = td // tm
    grid_spec = pltpu.PrefetchScalarGridSpec(
        num_scalar_prefetch=1,
        grid=(T // td,),
        in_specs=[pl.BlockSpec((TOP_K, td), lambda i, ps: (0, i)),
                  pl.BlockSpec((TOP_K, td), lambda i, ps: (0, i))],
        out_specs=pl.BlockSpec((per * TOP_K, tm), lambda i, ps: (i, 0)),
    )
    return pl.pallas_call(
        _dest_kernel,
        grid_spec=grid_spec,
        out_shape=jax.ShapeDtypeStruct((T // tm * TOP_K, tm), I32),
        name="dest",
    )(pad_start, idx_t, rank_t)


def _dispatch_kernel(pst_ref, plen_ref, nreal_ref, dest_hbm, hp_ref, xs_hbm, dst0, dst1, zbuf, isem, ssem, zsem):
    tm = hp_ref.shape[0] // ROW_SUB
    bm = zbuf.shape[0] // ROW_SUB
    n_idx = TOP_K * tm
    nb = xs_hbm.shape[0] // (bm * ROW_SUB)
    i = pl.program_id(0)
    nt = pl.num_programs(0)
    dst = (dst0, dst1)

    def idx_copy(tile, s):
        off = pl.multiple_of(tile * n_idx, n_idx)
        return pltpu.make_async_copy(dest_hbm.at[pl.ds(off, n_idx)], dst[s], isem.at[s])

    def pad_fill(e):
        n = plen_ref[e] * ROW_SUB
        start = pl.multiple_of(pst_ref[e] * ROW_SUB, ROW_SUB)
        return pltpu.make_async_copy(zbuf.at[pl.ds(0, n)], xs_hbm.at[pl.ds(start, n)], zsem)

    def dead_fill(j):
        return pltpu.make_async_copy(zbuf, xs_hbm.at[pl.ds(j * bm * ROW_SUB, bm * ROW_SUB)], zsem)

    def fills(act):
        for e in range(N_EXPERTS):
            pl.when(plen_ref[e] > 0)(functools.partial(lambda e: act(pad_fill(e)), e))
        for j in range(nb - N_EXPERTS, nb):
            pl.when(j >= nreal_ref[0])(functools.partial(lambda j: act(dead_fill(j)), j))

    @pl.when(i == 0)
    def _():
        zbuf[...] = jnp.zeros_like(zbuf)
        idx_copy(0, 0).start()
        idx_copy(0, 0).wait()
        fills(lambda cp: cp.start())

    def scatter(s):
        @pl.when(i + 1 < nt)
        def _():
            idx_copy(i + 1, 1 - s).start()

        def body(rb, c):
            for rr in range(SUBLANE):
                r = rb * SUBLANE + rr
                for k in range(TOP_K):
                    pltpu.make_async_copy(hp_ref.at[_slab(r)], xs_hbm.at[_slab(dst[s][k * tm + r])],
                                          ssem).start(priority=k % 2)
            return c
        lax.fori_loop(0, tm // SUBLANE, body, 0)
        for k in range(TOP_K):
            pltpu.make_async_copy(hp_ref, xs_hbm.at[pl.ds(0, tm * ROW_SUB)], ssem).wait()

        @pl.when(i + 1 < nt)
        def _():
            idx_copy(i + 1, 1 - s).wait()

    for s in range(2):
        pl.when(i % 2 == s)(functools.partial(scatter, s))

    @pl.when(i == 0)
    def _():
        fills(lambda cp: cp.wait())


def _dispatch_call(pad_row_start, pad_len, n_real, dest, hp, nb):
    tm = TM_ROUTE
    bm = BM_MOE
    T = hp.shape[0] // ROW_SUB
    grid_spec = pltpu.PrefetchScalarGridSpec(
        num_scalar_prefetch=3,
        grid=(T // tm,),
        in_specs=[pl.BlockSpec(memory_space=pl.ANY),
                  pl.BlockSpec((tm * ROW_SUB, LANE), lambda i, a, b, c: (i, 0))],
        out_specs=pl.BlockSpec(memory_space=pl.ANY),
        scratch_shapes=[pltpu.SMEM((TOP_K * tm,), I32),
                        pltpu.SMEM((TOP_K * tm,), I32),
                        pltpu.VMEM((bm * ROW_SUB, LANE), I32),
                        pltpu.SemaphoreType.DMA((2,)),
                        pltpu.SemaphoreType.DMA,
                        pltpu.SemaphoreType.DMA],
    )
    return pl.pallas_call(
        _dispatch_kernel,
        grid_spec=grid_spec,
        out_shape=jax.ShapeDtypeStruct((nb * bm * ROW_SUB, LANE), I32),
        compiler_params=pltpu.CompilerParams(dimension_semantics=("arbitrary",)),
        name="dispatch",
    )(pad_row_start, pad_len, n_real, dest, hp)


def _moe_kernel(be_ref, nreal_ref, x_ref, wg_ref, wu_ref, wd_ref, y_ref, wgb, wub, wdb):
    bm = x_ref.shape[0] // ROW_SUB
    i = pl.program_id(0)
    n = nreal_ref[0]

    @pl.when(i < n)
    def _():
        e_changed = (i == 0) | (be_ref[i] != be_ref[jnp.maximum(i - 1, 0)])

        @pl.when(e_changed)
        def _():
            wgb[...] = wg_ref[0].astype(BF16)
            wub[...] = wu_ref[0].astype(BF16)
            wdb[...] = wd_ref[0].astype(BF16)

        cm = bm // MOE_ROW_CHUNKS

        def up(c):
            halves = [_unpack2(w) for w in _load_rows(x_ref, cm, c * cm)]
            xb = jnp.concatenate([lo.astype(BF16) for lo, _ in halves] + [hi.astype(BF16) for _, hi in halves],
                                 axis=1)
            return (_silu(_dot(xb, wgb[...])) * _dot(xb, wub[...])).astype(BF16)

        nxt = up(0)
        for c in range(MOE_ROW_CHUNKS):
            a = nxt
            if c + 1 < MOE_ROW_CHUNKS:
                nxt = up(c + 1)
            y = _dot(a, wdb[...])
            _store_rows(y_ref, _pack2(y[:, :HALF_D], y[:, HALF_D:]), c * cm)

    @pl.when(i >= n)
    def _():
        y_ref[...] = jnp.zeros_like(y_ref)


def _moe_call(block_expert, n_real, xs, w_gate, w_up, w_down):
    bm = BM_MOE
    nb = block_expert.shape[0]
    wmap = lambda i, be, nr: (be[i], 0, 0)
    rows = pl.BlockSpec((bm * ROW_SUB, LANE), lambda i, be, nr: (i, 0))
    grid_spec = pltpu.PrefetchScalarGridSpec(
        num_scalar_prefetch=2,
        grid=(nb,),
        in_specs=[rows,
                  pl.BlockSpec((1, D_MODEL, EXPERT_FF), wmap),
                  pl.BlockSpec((1, D_MODEL, EXPERT_FF), wmap),
                  pl.BlockSpec((1, EXPERT_FF, D_MODEL), wmap)],
        out_specs=rows,
        scratch_shapes=[pltpu.VMEM((D_MODEL, EXPERT_FF), BF16),
                        pltpu.VMEM((D_MODEL, EXPERT_FF), BF16),
                        pltpu.VMEM((EXPERT_FF, D_MODEL), BF16)],
    )
    return pl.pallas_call(
        _moe_kernel,
        grid_spec=grid_spec,
        out_shape=jax.ShapeDtypeStruct(xs.shape, I32),
        compiler_params=pltpu.CompilerParams(dimension_semantics=("arbitrary",)),
        name="moe",
    )(block_expert, n_real, xs, w_gate, w_up, w_down)


def _combine_kernel(dest_hbm, ys_hbm, wc_ref, xmid_ref, mod_ref, wsg_ref, wsu_ref, wsd_ref, g_ref, b_ref,
                    o_ref, dst0, dst1, rbuf, isem, gsem):
    tm = xmid_ref.shape[0]
    n_idx = TOP_K * tm
    i = pl.program_id(0)
    nt = pl.num_programs(0)
    dst = (dst0, dst1)

    def idx_copy(tile, s):
        off = pl.multiple_of(tile * n_idx, n_idx)
        return pltpu.make_async_copy(dest_hbm.at[pl.ds(off, n_idx)], dst[s], isem.at[s])

    def gather_copy(s, k, r, src_row):
        return pltpu.make_async_copy(ys_hbm.at[_slab(src_row)], rbuf.at[s, k, pl.ds(r * ROW_SUB, ROW_SUB)],
                                     gsem.at[s])

    def start_gather(s):
        for r in range(tm):
            for k in range(TOP_K):
                gather_copy(s, k, r, dst[s][k * tm + r]).start(priority=k % 2)

    def wait_gather(s):
        for k in range(TOP_K):
            pltpu.make_async_copy(ys_hbm.at[pl.ds(0, tm * ROW_SUB)], rbuf.at[s, k], gsem.at[s]).wait()

    @pl.when(i == 0)
    def _():
        for s in range(2):
            idx_copy(jnp.minimum(s, nt - 1), s).start()
        for s in range(2):
            idx_copy(jnp.minimum(s, nt - 1), s).wait()

        def body(rb, c):
            for rr in range(SUBLANE):
                r = rb * SUBLANE + rr
                for k in range(TOP_K):
                    pltpu.make_async_copy(ys_hbm.at[_slab(dst0[k * tm + r])], rbuf.at[0, k, _slab(r)],
                                          gsem.at[0]).start(priority=k % 2)
            return c
        lax.fori_loop(0, tm // SUBLANE, body, 0)

    def finish(s):
        wc = wc_ref[...]
        acc = None
        for k in range(TOP_K):
            parts = [_unpack2(w) for w in _load_rows(rbuf.at[s, k], tm)]
            parts = [lo for lo, _ in parts] + [hi for _, hi in parts]
            wk = wc[:, k:k + 1]
            acc = [wk * p for p in parts] if acc is None else [a + wk * p for a, p in zip(acc, parts)]
        routed = jnp.concatenate(acc, axis=1)
        xmid = xmid_ref[...]
        sh2 = mod_ref[0, 3:4, :]
        sc2 = mod_ref[0, 4:5, :]
        g2 = mod_ref[0, 5:6, :]
        hb = (xmid * (1.0 + sc2) + sh2).astype(BF16)
        a = _silu(_dot(hb, wsg_ref[...])) * _dot(hb, wsu_ref[...])
        shared = _dot(a.astype(BF16), wsd_ref[...])
        o_ref[...] = _layer_norm(ALPHA * xmid + g2 * (routed + shared), g_ref[...], b_ref[...])

    def step(s):
        @pl.when((i >= 1) & (i + 1 < nt))
        def _():
            idx_copy(i + 1, 1 - s).wait()

        wait_gather(s)
        start_gather(1 - s)
        finish(s)

        @pl.when(i + 2 < nt)
        def _():
            idx_copy(i + 2, s).start()

        @pl.when(i == nt - 1)
        def _():
            wait_gather(1 - s)

    for s in range(2):
        pl.when(i % 2 == s)(functools.partial(step, s))


def _combine_call(dest, ys, wcol, xmid, mod3, wsg, wsu, wsd, g, b, tiles_per_seq):
    T = xmid.shape[0]
    tm = TM_ROUTE
    fixed = lambda i: (0, 0)
    row = lambda i: (i, 0)
    return pl.pallas_call(
        _combine_kernel,
        grid=(T // tm,),
        in_specs=[pl.BlockSpec(memory_space=pl.ANY),
                  pl.BlockSpec(memory_space=pl.ANY),
                  pl.BlockSpec((tm, LANE), row),
                  pl.BlockSpec((tm, D_MODEL), row),
                  pl.BlockSpec((1, 6, D_MODEL), lambda i: (i // tiles_per_seq, 0, 0)),
                  pl.BlockSpec((D_MODEL, SHARED_FF), fixed),
                  pl.BlockSpec((D_MODEL, SHARED_FF), fixed),
                  pl.BlockSpec((SHARED_FF, D_MODEL), fixed),
                  pl.BlockSpec((1, D_MODEL), fixed),
                  pl.BlockSpec((1, D_MODEL), fixed)],
        out_specs=pl.BlockSpec((tm, D_MODEL), row),
        out_shape=jax.ShapeDtypeStruct((T, D_MODEL), F32),
        scratch_shapes=[pltpu.SMEM((TOP_K * tm,), I32),
                        pltpu.SMEM((TOP_K * tm,), I32),
                        pltpu.VMEM((2, TOP_K, tm * ROW_SUB, LANE), I32),
                        pltpu.SemaphoreType.DMA((2,)),
                        pltpu.SemaphoreType.DMA((2,))],
        compiler_params=pltpu.CompilerParams(dimension_semantics=("arbitrary",)),
        name="combine",
    )(dest, ys, wcol, xmid, mod3, wsg, wsu, wsd, g, b)


def _group(a, b):
    ref = a if a is not None else b
    z = jnp.zeros(ref.shape[:-1], ref.dtype)
    a0, a1 = (a[..., 0], a[..., 1]) if a is not None else (z, z)
    b0, b1 = (b[..., 0], b[..., 1]) if b is not None else (z, z)
    return jnp.concatenate([a0, b0, a1, b1], axis=-1)


def _q_groups(per_head):
    return jnp.concatenate([_group(per_head(3 * (2 * p) + j), per_head(3 * (2 * p + 1) + j))
                            for p in range(N_KV_HEADS // 2) for j in range(GQA_GROUP)], axis=-1)


def _k_groups(per_head):
    return jnp.concatenate([_group(per_head(h), None) if h % 2 == 0 else _group(None, per_head(h))
                            for h in range(N_KV_HEADS)], axis=-1)


def _rope_pairs(seq):
    inv_freq = ROPE_THETA ** (-jnp.arange(0, HALF, 2, dtype=F32) / HALF)
    pos = jnp.arange(seq)
    rowp = (pos // GRID_W).astype(F32)
    colp = (pos % GRID_W).astype(F32)
    ang = jnp.concatenate([rowp[:, None] * inv_freq, colp[:, None] * inv_freq], axis=-1)
    cos, sin = jnp.cos(ang), jnp.sin(ang)
    return jnp.stack([cos, cos], axis=-1), jnp.stack([-sin, sin], axis=-1)


def kernel(x, c, ctx, c_ctx, w_mod, b_mod, w_in, q_norm, k_norm, pool_w, pool_scale, w_out, ln1_g, ln1_b,
           w_router, router_bias, w_gate, w_up, w_down, ws_gate, ws_up, ws_down, ln2_g, ln2_b):
    B, S, D = x.shape
    C = ctx.shape[1]
    T = B * S
    assert D == D_MODEL and w_mod.shape[0] == DEPTH and B + 1 <= MOD_ROWS
    assert S % TM_INPROJ == 0 and S % TQ_ATTN == 0 and S % TM_MIX == 0 and S % TM_ROUTE == 0
    assert S % GRID_W == 0 and C % SUBLANE == 0

    cc = jnp.concatenate([c, c_ctx[None, :], jnp.zeros((MOD_ROWS - B - 1, D), F32)], axis=0)
    mod3 = _mod_call(cc, w_mod[0], b_mod[0][None, :]).reshape(MOD_ROWS, 6, D)

    w = w_in[0]
    o1, o2, o3 = POOL_WIDTH, POOL_WIDTH + ATTN_WIDTH, POOL_WIDTH + ATTN_WIDTH + KV_WIDTH
    wq4 = w[:, o1:o2].reshape(D, N_HEADS, HALF, 2)
    wk4 = w[:, o2:o3].reshape(D, N_KV_HEADS, HALF, 2)
    wq = _q_groups(lambda h: wq4[:, h])
    wk = _k_groups(lambda h: wk4[:, h])
    w_all = jnp.concatenate([w[:, :o1], wq, wk, w[:, o3:]], axis=1).astype(BF16)
    w_kv = jnp.concatenate([wk, w[:, o3:]], axis=1).astype(BF16)
    qg4, kg4 = q_norm[0].reshape(1, HALF, 2), k_norm[0].reshape(1, HALF, 2)
    qg = _group(qg4, qg4)
    kg = _k_groups(lambda h: kg4)
    cos_p, sin_p = _rope_pairs(S)
    qscale = HEAD_DIM ** -0.5 * LOG2_E
    cq, sq = _group(cos_p, cos_p) * qscale, _group(sin_p, sin_p) * qscale
    ck = jnp.concatenate([_group(cos_p, None), _group(None, cos_p)], axis=-1)
    sk = jnp.concatenate([_group(sin_p, None), _group(None, sin_p)], axis=-1)
    slot = (jnp.arange(LANE) // HALF) % 2
    same = (slot[:, None] == slot[None, :]).astype(F32) * (1.0 / HEAD_DIM)
    gg = jnp.concatenate([same, same], axis=0).astype(BF16)
    u, q, k_l, v_l = _inproj_call(x, mod3, w_all, gg, qg, kg, cq, sq, ck, sk)
    k_c, v_c = _ctx_kv_call(ctx, mod3, w_kv, kg, B)

    k_all = jnp.concatenate([k_c, k_l], axis=1)
    v_all = jnp.concatenate([v_c, v_l], axis=1).reshape(B, C + S, N_KV_HEADS, HEAD_DIM)
    vt = jnp.transpose(v_all, (0, 2, 3, 1))
    ones = jnp.ones((B, N_KV_HEADS, 1, C + S), BF16)
    pad = jnp.zeros((B, N_KV_HEADS, VT_ROWS - HEAD_DIM - 1, C + S), BF16)
    attn = _attn_call(q, k_all, jnp.concatenate([vt, ones, pad], axis=2))

    bd = jax.scipy.linalg.block_diag(*[pool_w[0, g] for g in range(len(POOL_WINDOWS))]).astype(BF16)
    ypool = _pool_call(u, bd, pool_scale[0][None, :])

    wo = w_out[0]
    wop = wo[:POOL_WIDTH].astype(BF16)
    woa = wo[POOL_WIDTH:].reshape(N_KV_HEADS, GQA_GROUP * HEAD_DIM, D)
    woa = jnp.pad(woa, ((0, 0), (0, 2 * LANE - GQA_GROUP * HEAD_DIM), (0, 0))).reshape(ATTN_PAD, D).astype(BF16)
    tri = (jnp.arange(MIX_SUB)[:, None] < jnp.arange(MIX_SUB)[None, :]).astype(BF16)
    wr = w_router[0].T
    wr_hi = wr.astype(BF16)
    wr_lo = (wr - wr_hi.astype(F32)).astype(BF16)
    xmid, hp, idx_t, wcol, rank_t, counts = _mix_call(
        x.reshape(T, D), ypool.reshape(T, POOL_WIDTH), attn.reshape(T, ATTN_PAD), wop, woa, mod3,
        ln1_g[0][None, :], ln1_b[0][None, :], jnp.concatenate([wr_hi, wr_lo], axis=0),
        router_bias[0][:, None], tri, S // TM_MIX)

    bm = BM_MOE
    counts = counts[:, 0]
    padded = ((counts + bm - 1) // bm) * bm
    pad_end = jnp.cumsum(padded)
    pad_start = pad_end - padded
    nb = T * TOP_K // bm + N_EXPERTS
    n_real = (pad_end[-1] // bm).astype(I32).reshape(1)
    blk_row = jnp.arange(nb, dtype=I32) * bm
    block_expert = jnp.minimum(jnp.sum((pad_end[None, :] <= blk_row[:, None]).astype(I32), axis=1), N_EXPERTS - 1)

    dest = _dest_call(pad_start.astype(I32), idx_t, rank_t).reshape(-1)
    xs = _dispatch_call((pad_start + counts).astype(I32), (padded - counts).astype(I32), n_real, dest, hp, nb)
    ys = _moe_call(block_expert.astype(I32), n_real, xs, w_gate[0], w_up[0], w_down[0])
    out = _combine_call(dest, ys, wcol, xmid, mod3, ws_gate[0].astype(BF16), ws_up[0].astype(BF16),
                        ws_down[0].astype(BF16), ln2_g[0][None, :], ln2_b[0][None, :], S // TM_ROUTE)
    return out.reshape(B, S, D)
```

```python
import functools

import jax
import jax.numpy as jnp
from jax import lax
from jax.experimental import pallas as pl
from jax.experimental.pallas import tpu as pltpu

F32 = jnp.float32
BF16 = jnp.bfloat16
I32 = jnp.int32

LANE = 128
SUBLANE = 8

D_MODEL = 1024
GRID_W = 64
POOL_WIDTH = 256
POOL_WINDOWS = (2, 4, 8, 16)
POOL_GROUP = 64
HEAD_DIM = 64
HALF = HEAD_DIM // 2
N_HEADS = 12
N_KV_HEADS = 4
GQA_GROUP = N_HEADS // N_KV_HEADS
ATTN_WIDTH = N_HEADS * HEAD_DIM
KV_WIDTH = N_KV_HEADS * HEAD_DIM
ROPE_THETA = 10000.0
N_EXPERTS = 64
TOP_K = 8
N_GROUPS = 8
GROUP_SIZE = N_EXPERTS // N_GROUPS
TOPK_GROUPS = 4
EXPERT_FF = 256
SHARED_FF = 256
ROUTED_SCALE = 2.5
DEPTH = 1
ALPHA = (2.0 * DEPTH) ** 0.25
LN_EPS = 1e-5
RMS_EPS = 1e-6

MOD_ROWS = 24
HALF_D = D_MODEL // 2
ROW_SUB = HALF_D // LANE
Q_COLS = N_HEADS // 2 * LANE
K_COLS = N_KV_HEADS * LANE
ATTN_PAD = N_KV_HEADS * 2 * LANE
VT_ROWS = HEAD_DIM + 16
LOG2_E = 1.4426950408889634

TN_MOD = 512
TD_DEST = 2048
TM_INPROJ = 512
TQ_ATTN = 512
ATTN_SUB = 256
TM_MIX = 512
MIX_SUB = 256
BM_MOE = 1024
MOE_ROW_CHUNKS = 4
TM_ROUTE = 256

_NT = (((1,), (1,)), ((), ()))


def _dot(a, b):
    return jnp.dot(a, b, preferred_element_type=F32)


def _pack2(lo, hi):
    lo_bits = lax.bitcast_convert_type(lo.astype(BF16).astype(F32), I32)
    hi_bits = lax.bitcast_convert_type(hi.astype(BF16).astype(F32), I32)
    return lax.shift_right_logical(lo_bits, 16) | (hi_bits & jnp.int32(-65536))


def _unpack2(w):
    lo = lax.bitcast_convert_type(lax.shift_left(w, 16), F32)
    hi = lax.bitcast_convert_type(w & jnp.int32(-65536), F32)
    return lo, hi


def _store_rows(ref, packed, row0=0):
    n = packed.shape[0]
    for s in range(ROW_SUB):
        ref[pl.ds(row0 * ROW_SUB + s, n, stride=ROW_SUB), :] = packed[:, s * LANE:(s + 1) * LANE]


def _load_rows(ref, n, row0=0):
    return [ref[pl.ds(row0 * ROW_SUB + s, n, stride=ROW_SUB), :] for s in range(ROW_SUB)]


def _silu(x):
    return x * jax.nn.sigmoid(x)


def _layer_norm(r, g, b):
    mu = jnp.mean(r, axis=-1, keepdims=True)
    d = r - mu
    var = jnp.mean(d * d, axis=-1, keepdims=True)
    return d * lax.rsqrt(var + LN_EPS) * g + b


def _mod_kernel(c_ref, w_ref, b_ref, o_ref):
    a = _silu(c_ref[...])
    o_ref[...] = jnp.dot(a, w_ref[...], precision=lax.Precision.HIGHEST,
                         preferred_element_type=F32) + b_ref[...]


def _mod_call(cc, w_mod, b_mod):
    n = w_mod.shape[1]
    tn = TN_MOD
    return pl.pallas_call(
        _mod_kernel,
        grid=(n // tn,),
        in_specs=[pl.BlockSpec((MOD_ROWS, D_MODEL), lambda j: (0, 0)),
                  pl.BlockSpec((D_MODEL, tn), lambda j: (0, j)),
                  pl.BlockSpec((1, tn), lambda j: (0, j))],
        out_specs=pl.BlockSpec((MOD_ROWS, tn), lambda j: (0, j)),
        out_shape=jax.ShapeDtypeStruct((MOD_ROWS, n), F32),
        name="mod",
    )(cc, w_mod, b_mod)


def _store_vt(vt_ref, vt):
    rows = vt.shape[1]
    tail_row = lax.broadcasted_iota(I32, (VT_ROWS - HEAD_DIM, rows), 0)
    tail = jnp.where(tail_row == 0, 1.0, 0.0).astype(F32)
    for h in range(N_KV_HEADS):
        blk = jnp.concatenate([vt[h * HEAD_DIM:(h + 1) * HEAD_DIM], tail], axis=0)
        vt_ref[0, h * VT_ROWS:(h + 1) * VT_ROWS, :] = blk.astype(BF16)


def _norm_head(seg, g):
    ms = jnp.sum(seg * seg, axis=-1, keepdims=True) * (1.0 / HEAD_DIM)
    return seg * lax.rsqrt(ms + RMS_EPS) * g


def _norm_rope(seg, gg, g, c, s):
    sq = seg * seg
    hi = sq.astype(BF16)
    lo = (sq - hi.astype(F32)).astype(BF16)
    ms = _dot(jnp.concatenate([hi, lo], axis=1), gg)
    xn = seg * lax.rsqrt(ms + RMS_EPS) * g
    return xn * c + pltpu.roll(xn, LANE // 2, axis=1) * s


def _inproj_kernel(x_ref, mod_ref, w_ref, wvt_ref, gg_ref, qg_ref, kg_ref, cq_ref, sq_ref, ck_ref, sk_ref,
                   u_ref, q_ref, k_ref, vt_ref):
    sh = mod_ref[0, 0:1, :]
    sc = mod_ref[0, 1:2, :]
    xm = (x_ref[0] * (1.0 + sc) + sh).astype(BF16)
    gg, qg = gg_ref[...], qg_ref[...]
    cq, sq = cq_ref[...], sq_ref[...]
    n_qp = Q_COLS // (2 * LANE)
    n_kp = K_COLS // (2 * LANE)
    n_chunks = 1 + n_qp + n_kp + 1

    def chunk(c):
        if c == n_chunks - 1:
            return lax.dot_general(wvt_ref[...], xm, _NT, preferred_element_type=F32)
        return _dot(xm, w_ref[:, c * 2 * LANE:(c + 1) * 2 * LANE])

    nxt = chunk(0)
    for c in range(n_chunks):
        p = nxt
        if c + 1 < n_chunks:
            nxt = chunk(c + 1)
        if c == 0:
            u_ref[0] = p
        elif c <= n_qp:
            for j in range(2):
                grp = (c - 1) * 2 + j
                q = _norm_rope(p[:, j * LANE:(j + 1) * LANE], gg, qg, cq, sq)
                q_ref[0, :, grp * LANE:(grp + 1) * LANE] = q.astype(BF16)
        elif c <= n_qp + n_kp:
            for j in range(2):
                h = (c - 1 - n_qp) * 2 + j
                k = _norm_rope(p[:, j * LANE:(j + 1) * LANE], gg, kg_ref[:, h * LANE:(h + 1) * LANE],
                               ck_ref[:, j * LANE:(j + 1) * LANE], sk_ref[:, j * LANE:(j + 1) * LANE])
                k_ref[0, :, h * LANE:(h + 1) * LANE] = k.astype(BF16)
        else:
            _store_vt(vt_ref, p)


def _inproj_call(x, mod3, w_all, wvt, gg, qg, kg, cq, sq, ck, sk):
    B, S, _ = x.shape
    tm = TM_INPROJ
    ncol = w_all.shape[1]
    tab = pl.BlockSpec((tm, LANE), lambda b, i: (i, 0))
    tab2 = pl.BlockSpec((tm, 2 * LANE), lambda b, i: (i, 0))
    vec = pl.BlockSpec((1, LANE), lambda b, i: (0, 0))
    return pl.pallas_call(
        _inproj_kernel,
        grid=(B, S // tm),
        in_specs=[pl.BlockSpec((1, tm, D_MODEL), lambda b, i: (b, i, 0)),
                  pl.BlockSpec((1, 6, D_MODEL), lambda b, i: (b, 0, 0)),
                  pl.BlockSpec((D_MODEL, ncol), lambda b, i: (0, 0)),
                  pl.BlockSpec((KV_WIDTH, D_MODEL), lambda b, i: (0, 0)),
                  pl.BlockSpec((2 * LANE, LANE), lambda b, i: (0, 0)),
                  vec, pl.BlockSpec((1, K_COLS), lambda b, i: (0, 0)), tab, tab, tab2, tab2],
        out_specs=[pl.BlockSpec((1, tm, POOL_WIDTH), lambda b, i: (b, i, 0)),
                   pl.BlockSpec((1, tm, Q_COLS), lambda b, i: (b, i, 0)),
                   pl.BlockSpec((1, tm, K_COLS), lambda b, i: (b, i, 0)),
                   pl.BlockSpec((1, N_KV_HEADS * VT_ROWS, tm), lambda b, i: (b, 0, i))],
        out_shape=[jax.ShapeDtypeStruct((B, S, POOL_WIDTH), F32),
                   jax.ShapeDtypeStruct((B, S, Q_COLS), BF16),
                   jax.ShapeDtypeStruct((B, S, K_COLS), BF16),
                   jax.ShapeDtypeStruct((B, N_KV_HEADS * VT_ROWS, S), BF16)],
        name="inproj",
    )(x, mod3, w_all, wvt, gg, qg, kg, cq, sq, ck, sk)


def _ctx_kv_kernel(x_ref, mod_ref, w_ref, wvt_ref, kg_ref, k_ref, vt_ref):
    sh = mod_ref[0, 0:1, :]
    sc = mod_ref[0, 1:2, :]
    xm = (x_ref[0] * (1.0 + sc) + sh).astype(BF16)
    for pair in range(N_KV_HEADS // 2):
        p = _dot(xm, w_ref[:, pair * 2 * LANE:(pair + 1) * 2 * LANE])
        for j in range(2):
            h = pair * 2 + j
            kn = _norm_head(p[:, j * LANE:(j + 1) * LANE], kg_ref[:, h * LANE:(h + 1) * LANE])
            k_ref[0, :, h * LANE:(h + 1) * LANE] = kn.astype(BF16)
    _store_vt(vt_ref, lax.dot_general(wvt_ref[...], xm, _NT, preferred_element_type=F32))


def _ctx_kv_call(ctx, mod3, wk, wvt, kg, ctx_row):
    B, C, _ = ctx.shape
    return pl.pallas_call(
        _ctx_kv_kernel,
        grid=(B,),
        in_specs=[pl.BlockSpec((1, C, D_MODEL), lambda b: (b, 0, 0)),
                  pl.BlockSpec((1, 6, D_MODEL), lambda b: (ctx_row, 0, 0)),
                  pl.BlockSpec((D_MODEL, K_COLS), lambda b: (0, 0)),
                  pl.BlockSpec((KV_WIDTH, D_MODEL), lambda b: (0, 0)),
                  pl.BlockSpec((1, K_COLS), lambda b: (0, 0))],
        out_specs=[pl.BlockSpec((1, C, K_COLS), lambda b: (b, 0, 0)),
                   pl.BlockSpec((1, N_KV_HEADS * VT_ROWS, C), lambda b: (b, 0, 0))],
        out_shape=[jax.ShapeDtypeStruct((B, C, K_COLS), BF16),
                   jax.ShapeDtypeStruct((B, N_KV_HEADS * VT_ROWS, C), BF16)],
        name="ctx_kv",
    )(ctx, mod3, wk, wvt, kg)


POOL_PAD = 8


def _pool_kernel(u_ref, bd_ref, ps_ref, y_ref):
    S = u_ref.shape[1]
    n = S + 2 * POOL_PAD
    t = lax.broadcasted_iota(I32, (S, LANE), 0)
    lane = lax.broadcasted_iota(I32, (S, LANE), 1)
    zpad = jnp.zeros((POOL_PAD, LANE), F32)
    for half in range(POOL_WIDTH // LANE):
        u = u_ref[0, :, half * LANE:(half + 1) * LANE]
        ue = jnp.concatenate([zpad, u, zpad], axis=0)
        fwd = {1: ue}
        w = 1
        while w < POOL_WINDOWS[2 * half + 1]:
            fwd[2 * w] = fwd[w] + pltpu.roll(fwd[w], n - w, axis=0)
            w *= 2
        ds = []
        for win in POOL_WINDOWS[2 * half: 2 * half + 2]:
            hw = win // 2
            centred = pltpu.roll(fwd[win], hw, axis=0)[POOL_PAD:POOL_PAD + S]
            cnt = (jnp.minimum(t + hw, S) - jnp.maximum(t - hw, 0)).astype(F32)
            ds.append(centred / cnt - u)
        d = jnp.where(lane < POOL_GROUP, ds[0], ds[1]).astype(BF16)
        sl = slice(half * LANE, (half + 1) * LANE)
        y = _dot(d, bd_ref[sl, sl]) * ps_ref[:, sl]
        y_ref[0, :, sl] = y.astype(BF16)


def _pool_call(u, bd, ps):
    B, S, _ = u.shape
    return pl.pallas_call(
        _pool_kernel,
        grid=(B,),
        in_specs=[pl.BlockSpec((1, S, POOL_WIDTH), lambda b: (b, 0, 0)),
                  pl.BlockSpec((POOL_WIDTH, POOL_WIDTH), lambda b: (0, 0)),
                  pl.BlockSpec((1, POOL_WIDTH), lambda b: (0, 0))],
        out_specs=pl.BlockSpec((1, S, POOL_WIDTH), lambda b: (b, 0, 0)),
        out_shape=jax.ShapeDtypeStruct((B, S, POOL_WIDTH), BF16),
        name="pool",
    )(u, bd, ps)


ROW_REDUCE_WAYS = 16


def _reduce_rows(op, x):
    rows, lanes = x.shape
    if rows % (ROW_REDUCE_WAYS * SUBLANE) == 0:
        x = op(x.reshape(ROW_REDUCE_WAYS, rows // ROW_REDUCE_WAYS, lanes), axis=0)
    return op(x, axis=0, keepdims=True)


def _attn_kernel(q_ref, kc_ref, kl_ref, vtc_ref, vtl_ref, o_ref, s_buf, m_buf):
    t = pl.program_id(0)
    ts = s_buf.shape[-1]
    n_sub = q_ref.shape[1] // ts

    @pl.when(t == 0)
    def _():
        s_buf[1] = jnp.zeros_like(s_buf[1])
        m_buf[1] = jnp.zeros_like(m_buf[1])

    def step(slot):
        kc, kl = kc_ref[0], kl_ref[0]
        vtc, vtl = vtc_ref[0], vtl_ref[0]
        nc = kc.shape[0]
        for u in range(n_sub):
            cols = pl.ds(u * ts, ts)
            outs = []
            for j in range(GQA_GROUP):
                q = q_ref[0, cols, j * LANE:(j + 1) * LANE]
                sc = lax.dot_general(kc, q, _NT, preferred_element_type=F32)
                sl = lax.dot_general(kl, q, _NT, preferred_element_type=F32)
                s_buf[slot, u, j, 0:nc, :] = sc
                s_buf[slot, u, j, nc:, :] = sl
                mx = jnp.maximum(_reduce_rows(jnp.max, sc), _reduce_rows(jnp.max, sl))
                m_buf[slot, u, j] = jnp.broadcast_to(mx, (SUBLANE, ts))
                p = jnp.exp2(s_buf[1 - slot, u, j] - m_buf[1 - slot, u, j, 0:1, :]).astype(BF16)
                ol = _dot(vtc, p[0:nc]) + _dot(vtl, p[nc:])
                outs.append(ol[:HEAD_DIM] / ol[HEAD_DIM:HEAD_DIM + 1])
            outs.append(jnp.zeros((2 * LANE - GQA_GROUP * HEAD_DIM, ts), F32))
            o_ref[0, cols, :] = jnp.concatenate(outs, axis=0).T.astype(BF16)

    for slot in range(2):
        pl.when(t % 2 == slot)(functools.partial(step, slot))


def _attn_call(q, k_c, k_l, vt_c, vt_l):
    B, S, _ = q.shape
    C = k_c.shape[1]
    Lk = C + S
    tq = TQ_ATTN
    nq = S // tq
    n_items = B * N_KV_HEADS * nq

    def item(t):
        return t // (N_KV_HEADS * nq), (t // nq) % N_KV_HEADS, t % nq

    def cur(t):
        return item(jnp.minimum(t, n_items - 1))

    def prev(t):
        return item(jnp.maximum(t - 1, 0))

    return pl.pallas_call(
        _attn_kernel,
        grid=(n_items + 1,),
        in_specs=[pl.BlockSpec((1, tq, GQA_GROUP * LANE), lambda t: (cur(t)[0], cur(t)[2], cur(t)[1] // 2)),
                  pl.BlockSpec((1, C, LANE), lambda t: (cur(t)[0], 0, cur(t)[1])),
                  pl.BlockSpec((1, S, LANE), lambda t: (cur(t)[0], 0, cur(t)[1])),
                  pl.BlockSpec((1, VT_ROWS, C), lambda t: (prev(t)[0], prev(t)[1], 0)),
                  pl.BlockSpec((1, VT_ROWS, S), lambda t: (prev(t)[0], prev(t)[1], 0))],
        out_specs=pl.BlockSpec((1, tq, 2 * LANE), lambda t: (prev(t)[0], prev(t)[2], prev(t)[1])),
        out_shape=jax.ShapeDtypeStruct((B, S, ATTN_PAD), BF16),
        scratch_shapes=[pltpu.VMEM((2, tq // ATTN_SUB, GQA_GROUP, Lk, ATTN_SUB), F32),
                        pltpu.VMEM((2, tq // ATTN_SUB, GQA_GROUP, SUBLANE, ATTN_SUB), F32)],
        compiler_params=pltpu.CompilerParams(dimension_semantics=("arbitrary",)),
        name="attn",
    )(q, k_c, k_l, vt_c, vt_l)


def _mix_kernel(x_ref, yp_ref, at_ref, wop_ref, woa_ref, mod_ref, g_ref, b_ref, wr_ref, rb_ref, tri_ref,
                xmid_ref, hp_ref, idx_ref, wts_ref, rank_ref, cnt_ref, carry):
    i = pl.program_id(0)
    tm = tri_ref.shape[0]
    n_sub = x_ref.shape[0] // tm

    @pl.when(i == 0)
    def _():
        carry[...] = jnp.zeros_like(carry)

    g1 = mod_ref[0, 2:3, :]
    sh2 = mod_ref[0, 3:4, :]
    sc2 = mod_ref[0, 4:5, :]

    def out_proj(s):
        rows = pl.ds(s * tm, tm)
        return _dot(yp_ref[rows, :], wop_ref[...]) + _dot(at_ref[rows, :], woa_ref[...])

    def norm_router(s, mix):
        rows = pl.ds(s * tm, tm)
        xmid = _layer_norm(ALPHA * x_ref[rows, :] + g1 * mix, g_ref[...], b_ref[...])
        xmid_ref[rows, :] = xmid
        h = xmid * (1.0 + sc2) + sh2
        _store_rows(hp_ref, _pack2(h[:, :HALF_D], h[:, HALF_D:]), s * tm)
        h_hi = h.astype(BF16)
        h_lo = (h - h_hi.astype(F32)).astype(BF16)
        a = lax.dot_general(wr_ref[...], h_hi, _NT, preferred_element_type=F32)
        b = lax.dot_general(wr_ref[0:N_EXPERTS, :], h_lo, _NT, preferred_element_type=F32)
        return a[:N_EXPERTS] + a[N_EXPERTS:] + b

    nxt = out_proj(0)
    logits = []
    for s in range(n_sub):
        mix = nxt
        if s + 1 < n_sub:
            nxt = out_proj(s + 1)
        logits.append(norm_router(s, mix))
    base = carry[:, 0:1]
    for s in range(n_sub):
        base = base + _route(logits[s], base, rb_ref, tri_ref, idx_ref, wts_ref, rank_ref, s * tm)
    carry[...] = jnp.broadcast_to(base, carry.shape)
    cnt_ref[...] = carry[...].astype(I32)


def _route(logits, base, rb_ref, tri_ref, idx_ref, wts_ref, rank_ref, tok0):
    tm = logits.shape[1]
    cols = pl.ds(tok0, tm)
    score = jax.nn.sigmoid(logits)
    sel = score + rb_ref[...]
    neg = jnp.float32(-jnp.inf)
    sub = lax.broadcasted_iota(I32, (GROUP_SIZE, tm), 0)
    grp_sel, grp_score = [], []
    for g in range(N_GROUPS):
        sg = sel[g * GROUP_SIZE:(g + 1) * GROUP_SIZE, :]
        m1 = jnp.max(sg, axis=0, keepdims=True)
        i1 = jnp.min(jnp.where(sg == m1, sub, GROUP_SIZE), axis=0, keepdims=True)
        m2 = jnp.max(jnp.where(sub == i1, neg, sg), axis=0, keepdims=True)
        grp_sel.append(sg)
        grp_score.append(m1 + m2)
    masked = []
    for g in range(N_GROUPS):
        ahead = jnp.zeros((1, tm), I32)
        for o in range(N_GROUPS):
            if o == g:
                continue
            beats = (grp_score[o] > grp_score[g]) | ((grp_score[o] == grp_score[g]) & (o < g))
            ahead = ahead + beats.astype(I32)
        masked.append(jnp.where(ahead < TOPK_GROUPS, grp_sel[g], neg))
    cur = jnp.concatenate(masked, axis=0)
    row = lax.broadcasted_iota(I32, (N_EXPERTS, tm), 0)
    member = jnp.zeros((N_EXPERTS, tm), F32)
    picks, wsel = [], []
    for k in range(TOP_K):
        mx = jnp.max(cur, axis=0, keepdims=True)
        ei = jnp.min(jnp.where(cur == mx, row, N_EXPERTS), axis=0, keepdims=True)
        hit = row == ei
        picks.append(ei)
        wsel.append(jnp.sum(jnp.where(hit, score, 0.0), axis=0, keepdims=True))
        cur = jnp.where(hit, neg, cur)
        member = jnp.where(hit, 1.0, member)
    tot = wsel[0]
    for k in range(1, TOP_K):
        tot = tot + wsel[k]
    before = _dot(member.astype(BF16), tri_ref[...]) + base
    for k in range(TOP_K):
        idx_ref[k:k + 1, cols] = picks[k]
        rk = jnp.sum(jnp.where(row == picks[k], before, 0.0), axis=0, keepdims=True)
        rank_ref[k:k + 1, cols] = rk.astype(I32)
    wrows = [wsel[k] / tot * ROUTED_SCALE for k in range(TOP_K)]
    wmat = jnp.concatenate(wrows + [jnp.zeros((LANE - TOP_K, tm), F32)], axis=0)
    wts_ref[cols, :] = wmat.T
    return jnp.sum(member, axis=1, keepdims=True)


def _mix_call(x2, ypool, attn, wop, woa, mod3, g, b, wr_t, rb, tri, tiles_per_seq):
    T = x2.shape[0]
    tm = TM_MIX
    row = lambda i: (i, 0)
    fixed = lambda i: (0, 0)
    tok = lambda i: (0, i)
    return pl.pallas_call(
        _mix_kernel,
        grid=(T // tm,),
        in_specs=[pl.BlockSpec((tm, D_MODEL), row),
                  pl.BlockSpec((tm, POOL_WIDTH), row),
                  pl.BlockSpec((tm, ATTN_PAD), row),
                  pl.BlockSpec((POOL_WIDTH, D_MODEL), fixed),
                  pl.BlockSpec((ATTN_PAD, D_MODEL), fixed),
                  pl.BlockSpec((1, 6, D_MODEL), lambda i: (i // tiles_per_seq, 0, 0)),
                  pl.BlockSpec((1, D_MODEL), fixed),
                  pl.BlockSpec((1, D_MODEL), fixed),
                  pl.BlockSpec((2 * N_EXPERTS, D_MODEL), fixed),
                  pl.BlockSpec((N_EXPERTS, 1), fixed),
                  pl.BlockSpec((MIX_SUB, MIX_SUB), fixed)],
        out_specs=[pl.BlockSpec((tm, D_MODEL), row),
                   pl.BlockSpec((tm * ROW_SUB, LANE), row),
                   pl.BlockSpec((TOP_K, tm), tok),
                   pl.BlockSpec((tm, LANE), row),
                   pl.BlockSpec((TOP_K, tm), tok),
                   pl.BlockSpec((N_EXPERTS, LANE), fixed)],
        out_shape=[jax.ShapeDtypeStruct((T, D_MODEL), F32),
                   jax.ShapeDtypeStruct((T * ROW_SUB, LANE), I32),
                   jax.ShapeDtypeStruct((TOP_K, T), I32),
                   jax.ShapeDtypeStruct((T, LANE), F32),
                   jax.ShapeDtypeStruct((TOP_K, T), I32),
                   jax.ShapeDtypeStruct((N_EXPERTS, LANE), I32)],
        scratch_shapes=[pltpu.VMEM((N_EXPERTS, LANE), F32)],
        compiler_params=pltpu.CompilerParams(dimension_semantics=("arbitrary",)),
        name="mix",
    )(x2, ypool, attn, wop, woa, mod3, g, b, wr_t, rb, tri)


def _slab(r):
    return pl.ds(pl.multiple_of(r * ROW_SUB, ROW_SUB), ROW_SUB)


def _dest_kernel(ps_ref, idx_ref, rank_ref, o_ref):
    idx = idx_ref[...]
    d = rank_ref[...]
    for e in range(N_EXPERTS):
        d = d + jnp.where(idx == e, ps_ref[e], 0)
    tm = o_ref.shape[1]
    for j in range(idx.shape[1] // tm):
        o_ref[j * TOP_K:(j + 1) * TOP_K, :] = d[:, j * tm:(j + 1) * tm]


def _dest_call(pad_start, idx_t, rank_t):
    T = idx_t.shape[1]
    tm = TM_ROUTE
    td = TD_DEST if T % TD_DEST == 0 else tm
    per = td // tm
    grid_spec = pltpu.PrefetchScalarGridSpec(
        num_scalar_prefetch=1,
        grid=(T // td,),
        in_specs=[pl.BlockSpec((TOP_K, td), lambda i, ps: (0, i)),
                  pl.BlockSpec((TOP_K, td), lambda i, ps: (0, i))],
        out_specs=pl.BlockSpec((per * TOP_K, tm), lambda i, ps: (i, 0)),
    )
    return pl.pallas_call(
        _dest_kernel,
        grid_spec=grid_spec,
        out_shape=jax.ShapeDtypeStruct((T // tm * TOP_K, tm), I32),
        name="dest",
    )(pad_start, idx_t, rank_t)


def _dispatch_kernel(pst_ref, plen_ref, nreal_ref, dest_hbm, hp_ref, xs_hbm, dst0, dst1, zbuf, isem, ssem, zsem):
    tm = hp_ref.shape[0] // ROW_SUB
    bm = zbuf.shape[0] // ROW_SUB
    n_idx = TOP_K * tm
    nb = xs_hbm.shape[0] // (bm * ROW_SUB)
    i = pl.program_id(0)
    nt = pl.num_programs(0)
    dst = (dst0, dst1)

    def idx_copy(tile, s):
        off = pl.multiple_of(tile * n_idx, n_idx)
        return pltpu.make_async_copy(dest_hbm.at[pl.ds(off, n_idx)], dst[s], isem.at[s])

    def pad_fill(e):
        n = plen_ref[e] * ROW_SUB
        start = pl.multiple_of(pst_ref[e] * ROW_SUB, ROW_SUB)
        return pltpu.make_async_copy(zbuf.at[pl.ds(0, n)], xs_hbm.at[pl.ds(start, n)], zsem)

    def dead_fill(j):
        return pltpu.make_async_copy(zbuf, xs_hbm.at[pl.ds(j * bm * ROW_SUB, bm * ROW_SUB)], zsem)

    def fills(act):
        for e in range(N_EXPERTS):
            pl.when(plen_ref[e] > 0)(functools.partial(lambda e: act(pad_fill(e)), e))
        for j in range(nb - N_EXPERTS, nb):
            pl.when(j >= nreal_ref[0])(functools.partial(lambda j: act(dead_fill(j)), j))

    @pl.when(i == 0)
    def _():
        zbuf[...] = jnp.zeros_like(zbuf)
        idx_copy(0, 0).start()
        idx_copy(0, 0).wait()
        fills(lambda cp: cp.start())

    def scatter(s):
        @pl.when(i + 1 < nt)
        def _():
            idx_copy(i + 1, 1 - s).start()

        def body(rb, c):
            for rr in range(SUBLANE):
                r = rb * SUBLANE + rr
                for k in range(TOP_K):
                    pltpu.make_async_copy(hp_ref.at[_slab(r)], xs_hbm.at[_slab(dst[s][k * tm + r])],
                                          ssem).start(priority=k % 2)
            return c
        lax.fori_loop(0, tm // SUBLANE, body, 0)
        for k in range(TOP_K):
            pltpu.make_async_copy(hp_ref, xs_hbm.at[pl.ds(0, tm * ROW_SUB)], ssem).wait()

        @pl.when(i + 1 < nt)
        def _():
            idx_copy(i + 1, 1 - s).wait()

    for s in range(2):
        pl.when(i % 2 == s)(functools.partial(scatter, s))

    @pl.when(i == 0)
    def _():
        fills(lambda cp: cp.wait())


def _dispatch_call(pad_row_start, pad_len, n_real, dest, hp, nb):
    tm = TM_ROUTE
    bm = BM_MOE
    T = hp.shape[0] // ROW_SUB
    grid_spec = pltpu.PrefetchScalarGridSpec(
        num_scalar_prefetch=3,
        grid=(T // tm,),
        in_specs=[pl.BlockSpec(memory_space=pl.ANY),
                  pl.BlockSpec((tm * ROW_SUB, LANE), lambda i, a, b, c: (i, 0))],
        out_specs=pl.BlockSpec(memory_space=pl.ANY),
        scratch_shapes=[pltpu.SMEM((TOP_K * tm,), I32),
                        pltpu.SMEM((TOP_K * tm,), I32),
                        pltpu.VMEM((bm * ROW_SUB, LANE), I32),
                        pltpu.SemaphoreType.DMA((2,)),
                        pltpu.SemaphoreType.DMA,
                        pltpu.SemaphoreType.DMA],
    )
    return pl.pallas_call(
        _dispatch_kernel,
        grid_spec=grid_spec,
        out_shape=jax.ShapeDtypeStruct((nb * bm * ROW_SUB, LANE), I32),
        compiler_params=pltpu.CompilerParams(dimension_semantics=("arbitrary",)),
        name="dispatch",
    )(pad_row_start, pad_len, n_real, dest, hp)


def _moe_kernel(be_ref, nreal_ref, x_ref, wg_ref, wu_ref, wd_ref, y_ref, wgb, wub, wdb):
    bm = x_ref.shape[0] // ROW_SUB
    i = pl.program_id(0)
    n = nreal_ref[0]

    @pl.when(i < n)
    def _():
        e_changed = (i == 0) | (be_ref[i] != be_ref[jnp.maximum(i - 1, 0)])

        @pl.when(e_changed)
        def _():
            wgb[...] = wg_ref[0].astype(BF16)
            wub[...] = wu_ref[0].astype(BF16)
            wdb[...] = wd_ref[0].astype(BF16)

        cm = bm // MOE_ROW_CHUNKS

        def up(c):
            halves = [_unpack2(w) for w in _load_rows(x_ref, cm, c * cm)]
            xb = jnp.concatenate([lo.astype(BF16) for lo, _ in halves] + [hi.astype(BF16) for _, hi in halves],
                                 axis=1)
            return (_silu(_dot(xb, wgb[...])) * _dot(xb, wub[...])).astype(BF16)

        nxt = up(0)
        for c in range(MOE_ROW_CHUNKS):
            a = nxt
            if c + 1 < MOE_ROW_CHUNKS:
                nxt = up(c + 1)
            y = _dot(a, wdb[...])
            _store_rows(y_ref, _pack2(y[:, :HALF_D], y[:, HALF_D:]), c * cm)

    @pl.when(i >= n)
    def _():
        y_ref[...] = jnp.zeros_like(y_ref)


def _moe_call(block_expert, n_real, xs, w_gate, w_up, w_down):
    bm = BM_MOE
    nb = block_expert.shape[0]
    wmap = lambda i, be, nr: (be[i], 0, 0)
    rows = pl.BlockSpec((bm * ROW_SUB, LANE), lambda i, be, nr: (i, 0))
    grid_spec = pltpu.PrefetchScalarGridSpec(
        num_scalar_prefetch=2,
        grid=(nb,),
        in_specs=[rows,
                  pl.BlockSpec((1, D_MODEL, EXPERT_FF), wmap),
                  pl.BlockSpec((1, D_MODEL, EXPERT_FF), wmap),
                  pl.BlockSpec((1, EXPERT_FF, D_MODEL), wmap)],
        out_specs=rows,
        scratch_shapes=[pltpu.VMEM((D_MODEL, EXPERT_FF), BF16),
                        pltpu.VMEM((D_MODEL, EXPERT_FF), BF16),
                        pltpu.VMEM((EXPERT_FF, D_MODEL), BF16)],
    )
    return pl.pallas_call(
        _moe_kernel,
        grid_spec=grid_spec,
        out_shape=jax.ShapeDtypeStruct(xs.shape, I32),
        compiler_params=pltpu.CompilerParams(dimension_semantics=("arbitrary",)),
        name="moe",
    )(block_expert, n_real, xs, w_gate, w_up, w_down)


def _combine_kernel(dest_hbm, ys_hbm, wc_ref, xmid_ref, mod_ref, wsg_ref, wsu_ref, wsd_ref, g_ref, b_ref,
                    o_ref, dst0, dst1, rbuf, isem, gsem):
    tm = xmid_ref.shape[0]
    n_idx = TOP_K * tm
    i = pl.program_id(0)
    nt = pl.num_programs(0)
    dst = (dst0, dst1)

    def idx_copy(tile, s):
        off = pl.multiple_of(tile * n_idx, n_idx)
        return pltpu.make_async_copy(dest_hbm.at[pl.ds(off, n_idx)], dst[s], isem.at[s])

    def gather_copy(s, k, r, src_row):
        return pltpu.make_async_copy(ys_hbm.at[_slab(src_row)], rbuf.at[s, k, pl.ds(r * ROW_SUB, ROW_SUB)],
                                     gsem.at[s])

    def start_gather(s):
        for r in range(tm):
            for k in range(TOP_K):
                gather_copy(s, k, r, dst[s][k * tm + r]).start(priority=k % 2)

    def wait_gather(s):
        for k in range(TOP_K):
            pltpu.make_async_copy(ys_hbm.at[pl.ds(0, tm * ROW_SUB)], rbuf.at[s, k], gsem.at[s]).wait()

    @pl.when(i == 0)
    def _():
        for s in range(2):
            idx_copy(jnp.minimum(s, nt - 1), s).start()
        for s in range(2):
            idx_copy(jnp.minimum(s, nt - 1), s).wait()

        def body(rb, c):
            for rr in range(SUBLANE):
                r = rb * SUBLANE + rr
                for k in range(TOP_K):
                    pltpu.make_async_copy(ys_hbm.at[_slab(dst0[k * tm + r])], rbuf.at[0, k, _slab(r)],
                                          gsem.at[0]).start(priority=k % 2)
            return c
        lax.fori_loop(0, tm // SUBLANE, body, 0)

    def finish(s):
        wc = wc_ref[...]
        acc = None
        for k in range(TOP_K):
            parts = [_unpack2(w) for w in _load_rows(rbuf.at[s, k], tm)]
            parts = [lo for lo, _ in parts] + [hi for _, hi in parts]
            wk = wc[:, k:k + 1]
            acc = [wk * p for p in parts] if acc is None else [a + wk * p for a, p in zip(acc, parts)]
        routed = jnp.concatenate(acc, axis=1)
        xmid = xmid_ref[...]
        sh2 = mod_ref[0, 3:4, :]
        sc2 = mod_ref[0, 4:5, :]
        g2 = mod_ref[0, 5:6, :]
        hb = (xmid * (1.0 + sc2) + sh2).astype(BF16)
        a = _silu(_dot(hb, wsg_ref[...])) * _dot(hb, wsu_ref[...])
        shared = _dot(a.astype(BF16), wsd_ref[...])
        o_ref[...] = _layer_norm(ALPHA * xmid + g2 * (routed + shared), g_ref[...], b_ref[...])

    def step(s):
        @pl.when((i >= 1) & (i + 1 < nt))
        def _():
            idx_copy(i + 1, 1 - s).wait()

        wait_gather(s)
        start_gather(1 - s)
        finish(s)

        @pl.when(i + 2 < nt)
        def _():
            idx_copy(i + 2, s).start()

        @pl.when(i == nt - 1)
        def _():
            wait_gather(1 - s)

    for s in range(2):
        pl.when(i % 2 == s)(functools.partial(step, s))


def _combine_call(dest, ys, wcol, xmid, mod3, wsg, wsu, wsd, g, b, tiles_per_seq):
    T = xmid.shape[0]
    tm = TM_ROUTE
    fixed = lambda i: (0, 0)
    row = lambda i: (i, 0)
    return pl.pallas_call(
        _combine_kernel,
        grid=(T // tm,),
        in_specs=[pl.BlockSpec(memory_space=pl.ANY),
                  pl.BlockSpec(memory_space=pl.ANY),
                  pl.BlockSpec((tm, LANE), row),
                  pl.BlockSpec((tm, D_MODEL), row),
                  pl.BlockSpec((1, 6, D_MODEL), lambda i: (i // tiles_per_seq, 0, 0)),
                  pl.BlockSpec((D_MODEL, SHARED_FF), fixed),
                  pl.BlockSpec((D_MODEL, SHARED_FF), fixed),
                  pl.BlockSpec((SHARED_FF, D_MODEL), fixed),
                  pl.BlockSpec((1, D_MODEL), fixed),
                  pl.BlockSpec((1, D_MODEL), fixed)],
        out_specs=pl.BlockSpec((tm, D_MODEL), row),
        out_shape=jax.ShapeDtypeStruct((T, D_MODEL), F32),
        scratch_shapes=[pltpu.SMEM((TOP_K * tm,), I32),
                        pltpu.SMEM((TOP_K * tm,), I32),
                        pltpu.VMEM((2, TOP_K, tm * ROW_SUB, LANE), I32),
                        pltpu.SemaphoreType.DMA((2,)),
                        pltpu.SemaphoreType.DMA((2,))],
        compiler_params=pltpu.CompilerParams(dimension_semantics=("arbitrary",)),
        name="combine",
    )(dest, ys, wcol, xmid, mod3, wsg, wsu, wsd, g, b)


def _group(a, b):
    ref = a if a is not None else b
    z = jnp.zeros(ref.shape[:-1], ref.dtype)
    a0, a1 = (a[..., 0], a[..., 1]) if a is not None else (z, z)
    b0, b1 = (b[..., 0], b[..., 1]) if b is not None else (z, z)
    return jnp.concatenate([a0, b0, a1, b1], axis=-1)


def _q_groups(per_head):
    return jnp.concatenate([_group(per_head(3 * (2 * p) + j), per_head(3 * (2 * p + 1) + j))
                            for p in range(N_KV_HEADS // 2) for j in range(GQA_GROUP)], axis=-1)


def _k_groups(per_head):
    return jnp.concatenate([_group(per_head(h), None) if h % 2 == 0 else _group(None, per_head(h))
                            for h in range(N_KV_HEADS)], axis=-1)


def _rope_pairs(seq):
    inv_freq = ROPE_THETA ** (-jnp.arange(0, HALF, 2, dtype=F32) / HALF)
    pos = jnp.arange(seq)
    rowp = (pos // GRID_W).astype(F32)
    colp = (pos % GRID_W).astype(F32)
    ang = jnp.concatenate([rowp[:, None] * inv_freq, colp[:, None] * inv_freq], axis=-1)
    cos, sin = jnp.cos(ang), jnp.sin(ang)
    return jnp.stack([cos, cos], axis=-1), jnp.stack([-sin, sin], axis=-1)


def kernel(x, c, ctx, c_ctx, w_mod, b_mod, w_in, q_norm, k_norm, pool_w, pool_scale, w_out, ln1_g, ln1_b,
           w_router, router_bias, w_gate, w_up, w_down, ws_gate, ws_up, ws_down, ln2_g, ln2_b):
    B, S, D = x.shape
    C = ctx.shape[1]
    T = B * S
    assert D == D_MODEL and w_mod.shape[0] == DEPTH and B + 1 <= MOD_ROWS
    assert S % TM_INPROJ == 0 and S % TQ_ATTN == 0 and S % TM_MIX == 0 and S % TM_ROUTE == 0
    assert S % GRID_W == 0 and C % SUBLANE == 0

    cc = jnp.concatenate([c, c_ctx[None, :], jnp.zeros((MOD_ROWS - B - 1, D), F32)], axis=0)
    mod3 = _mod_call(cc, w_mod[0], b_mod[0][None, :]).reshape(MOD_ROWS, 6, D)

    w = w_in[0]
    o1, o2, o3 = POOL_WIDTH, POOL_WIDTH + ATTN_WIDTH, POOL_WIDTH + ATTN_WIDTH + KV_WIDTH
    wq4 = w[:, o1:o2].reshape(D, N_HEADS, HALF, 2)
    wk4 = w[:, o2:o3].reshape(D, N_KV_HEADS, HALF, 2)
    wq = _q_groups(lambda h: wq4[:, h])
    wk = _k_groups(lambda h: wk4[:, h])
    wk = wk.astype(BF16)
    w_all = jnp.concatenate([w[:, :o1].astype(BF16), wq.astype(BF16), wk], axis=1)
    wvt = w[:, o3:].T.astype(BF16)
    qg4, kg4 = q_norm[0].reshape(1, HALF, 2), k_norm[0].reshape(1, HALF, 2)
    qg = _group(qg4, qg4)
    kg = _k_groups(lambda h: kg4)
    cos_p, sin_p = _rope_pairs(S)
    qscale = HEAD_DIM ** -0.5 * LOG2_E
    cq, sq = _group(cos_p, cos_p) * qscale, _group(sin_p, sin_p) * qscale
    ck = jnp.concatenate([_group(cos_p, None), _group(None, cos_p)], axis=-1)
    sk = jnp.concatenate([_group(sin_p, None), _group(None, sin_p)], axis=-1)
    slot = (jnp.arange(LANE) // HALF) % 2
    same = (slot[:, None] == slot[None, :]).astype(F32) * (1.0 / HEAD_DIM)
    gg = jnp.concatenate([same, same], axis=0).astype(BF16)
    u, q, k_l, vt_l = _inproj_call(x, mod3, w_all, wvt, gg, qg, kg, cq, sq, ck, sk)
    k_c, vt_c = _ctx_kv_call(ctx, mod3, wk, wvt, kg, B)
    attn = _attn_call(q, k_c, k_l, vt_c, vt_l)

    bd = jax.scipy.linalg.block_diag(*[pool_w[0, g] for g in range(len(POOL_WINDOWS))]).astype(BF16)
    ypool = _pool_call(u, bd, pool_scale[0][None, :])

    wo = w_out[0]
    wop = wo[:POOL_WIDTH].astype(BF16)
    woa = wo[POOL_WIDTH:].reshape(N_KV_HEADS, GQA_GROUP * HEAD_DIM, D)
    woa = jnp.pad(woa, ((0, 0), (0, 2 * LANE - GQA_GROUP * HEAD_DIM), (0, 0))).reshape(ATTN_PAD, D).astype(BF16)
    tri = (jnp.arange(MIX_SUB)[:, None] < jnp.arange(MIX_SUB)[None, :]).astype(BF16)
    wr = w_router[0].T
    wr_hi = wr.astype(BF16)
    wr_lo = (wr - wr_hi.astype(F32)).astype(BF16)
    xmid, hp, idx_t, wcol, rank_t, counts = _mix_call(
        x.reshape(T, D), ypool.reshape(T, POOL_WIDTH), attn.reshape(T, ATTN_PAD), wop, woa, mod3,
        ln1_g[0][None, :], ln1_b[0][None, :], jnp.concatenate([wr_hi, wr_lo], axis=0),
        router_bias[0][:, None], tri, S // TM_MIX)

    bm = BM_MOE
    counts = counts[:, 0]
    padded = ((counts + bm - 1) // bm) * bm
    pad_end = jnp.cumsum(padded)
    pad_start = pad_end - padded
    nb = T * TOP_K // bm + N_EXPERTS
    n_real = (pad_end[-1] // bm).astype(I32).reshape(1)
    blk_row = jnp.arange(nb, dtype=I32) * bm
    block_expert = jnp.minimum(jnp.sum((pad_end[None, :] <= blk_row[:, None]).astype(I32), axis=1), N_EXPERTS - 1)

    dest = _dest_call(pad_start.astype(I32), idx_t, rank_t).reshape(-1)
    xs = _dispatch_call((pad_start + counts).astype(I32), (padded - counts).astype(I32), n_real, dest, hp, nb)
    ys = _moe_call(block_expert.astype(I32), n_real, xs, w_gate[0], w_up[0], w_down[0])
    out = _combine_call(dest, ys, wcol, xmid, mod3, ws_gate[0].astype(BF16), ws_up[0].astype(BF16),
                        ws_down[0].astype(BF16), ln2_g[0][None, :], ln2_b[0][None, :], S // TM_ROUTE)
    return out.reshape(B, S, D)
```

```python
import functools

import jax
import jax.numpy as jnp
from jax import lax
from jax.experimental import pallas as pl
from jax.experimental.pallas import tpu as pltpu

F32 = jnp.float32
BF16 = jnp.bfloat16
I32 = jnp.int32

LANE = 128
SUBLANE = 8

D_MODEL = 1024
GRID_W = 64
POOL_WIDTH = 256
POOL_WINDOWS = (2, 4, 8, 16)
POOL_GROUP = 64
HEAD_DIM = 64
HALF = HEAD_DIM // 2
N_HEADS = 12
N_KV_HEADS = 4
GQA_GROUP = N_HEADS // N_KV_HEADS
ATTN_WIDTH = N_HEADS * HEAD_DIM
KV_WIDTH = N_KV_HEADS * HEAD_DIM
ROPE_THETA = 10000.0
N_EXPERTS = 64
TOP_K = 8
N_GROUPS = 8
GROUP_SIZE = N_EXPERTS // N_GROUPS
TOPK_GROUPS = 4
EXPERT_FF = 256
SHARED_FF = 256
ROUTED_SCALE = 2.5
DEPTH = 1
ALPHA = (2.0 * DEPTH) ** 0.25
LN_EPS = 1e-5
RMS_EPS = 1e-6

MOD_ROWS = 24
HALF_D = D_MODEL // 2
ROW_SUB = HALF_D // LANE
Q_COLS = N_HEADS // 2 * LANE
K_COLS = N_KV_HEADS * LANE
ATTN_PAD = N_KV_HEADS * 2 * LANE
VT_ROWS = HEAD_DIM + 16
LOG2_E = 1.4426950408889634

TN_MOD = 512
TM_INPROJ = 512
TQ_ATTN = 512
ATTN_SUB = 256
TM_MIX = 512
MIX_SUB = 256
BM_MOE = 1024
MOE_ROW_CHUNKS = 4
TM_ROUTE = 256

_NT = (((1,), (1,)), ((), ()))


def _dot(a, b):
    return jnp.dot(a, b, preferred_element_type=F32)


def _pack2(lo, hi):
    lo_bits = lax.bitcast_convert_type(lo.astype(BF16).astype(F32), I32)
    hi_bits = lax.bitcast_convert_type(hi.astype(BF16).astype(F32), I32)
    return lax.shift_right_logical(lo_bits, 16) | (hi_bits & jnp.int32(-65536))


def _unpack2(w):
    lo = lax.bitcast_convert_type(lax.shift_left(w, 16), F32)
    hi = lax.bitcast_convert_type(w & jnp.int32(-65536), F32)
    return lo, hi


def _store_rows(ref, packed, row0=0):
    n = packed.shape[0]
    for s in range(ROW_SUB):
        ref[pl.ds(row0 * ROW_SUB + s, n, stride=ROW_SUB), :] = packed[:, s * LANE:(s + 1) * LANE]


def _load_rows(ref, n, row0=0):
    return [ref[pl.ds(row0 * ROW_SUB + s, n, stride=ROW_SUB), :] for s in range(ROW_SUB)]


def _silu(x):
    return x * jax.nn.sigmoid(x)


def _layer_norm(r, g, b):
    mu = jnp.mean(r, axis=-1, keepdims=True)
    d = r - mu
    var = jnp.mean(d * d, axis=-1, keepdims=True)
    return d * lax.rsqrt(var + LN_EPS) * g + b


def _mod_kernel(c_ref, w_ref, b_ref, o_ref):
    a = _silu(c_ref[...])
    o_ref[...] = jnp.dot(a, w_ref[...], precision=lax.Precision.HIGHEST,
                         preferred_element_type=F32) + b_ref[...]


def _mod_call(cc, w_mod, b_mod):
    n = w_mod.shape[1]
    tn = TN_MOD
    return pl.pallas_call(
        _mod_kernel,
        grid=(n // tn,),
        in_specs=[pl.BlockSpec((MOD_ROWS, D_MODEL), lambda j: (0, 0)),
                  pl.BlockSpec((D_MODEL, tn), lambda j: (0, j)),
                  pl.BlockSpec((1, tn), lambda j: (0, j))],
        out_specs=pl.BlockSpec((MOD_ROWS, tn), lambda j: (0, j)),
        out_shape=jax.ShapeDtypeStruct((MOD_ROWS, n), F32),
        name="mod",
    )(cc, w_mod, b_mod)


def _store_vt(vt_ref, vt):
    rows = vt.shape[1]
    tail_row = lax.broadcasted_iota(I32, (VT_ROWS - HEAD_DIM, rows), 0)
    tail = jnp.where(tail_row == 0, 1.0, 0.0).astype(F32)
    for h in range(N_KV_HEADS):
        blk = jnp.concatenate([vt[h * HEAD_DIM:(h + 1) * HEAD_DIM], tail], axis=0)
        vt_ref[0, h * VT_ROWS:(h + 1) * VT_ROWS, :] = blk.astype(BF16)


def _norm_head(seg, g):
    ms = jnp.sum(seg * seg, axis=-1, keepdims=True) * (1.0 / HEAD_DIM)
    return seg * lax.rsqrt(ms + RMS_EPS) * g


def _norm_rope(seg, gg, g, c, s):
    sq = seg * seg
    hi = sq.astype(BF16)
    lo = (sq - hi.astype(F32)).astype(BF16)
    ms = _dot(jnp.concatenate([hi, lo], axis=1), gg)
    xn = seg * lax.rsqrt(ms + RMS_EPS) * g
    return xn * c + pltpu.roll(xn, LANE // 2, axis=1) * s


def _inproj_kernel(x_ref, mod_ref, w_ref, wvt_ref, gg_ref, qg_ref, kg_ref, cq_ref, sq_ref, ck_ref, sk_ref,
                   u_ref, q_ref, k_ref, vt_ref):
    sh = mod_ref[0, 0:1, :]
    sc = mod_ref[0, 1:2, :]
    xm = (x_ref[0] * (1.0 + sc) + sh).astype(BF16)
    gg, qg = gg_ref[...], qg_ref[...]
    cq, sq = cq_ref[...], sq_ref[...]
    n_qp = Q_COLS // (2 * LANE)
    n_kp = K_COLS // (2 * LANE)
    n_chunks = 1 + n_qp + n_kp + 1

    def chunk(c):
        if c == n_chunks - 1:
            return lax.dot_general(wvt_ref[...], xm, _NT, preferred_element_type=F32)
        return _dot(xm, w_ref[:, c * 2 * LANE:(c + 1) * 2 * LANE])

    nxt = chunk(0)
    for c in range(n_chunks):
        p = nxt
        if c + 1 < n_chunks:
            nxt = chunk(c + 1)
        if c == 0:
            u_ref[0] = p
        elif c <= n_qp:
            for j in range(2):
                grp = (c - 1) * 2 + j
                q = _norm_rope(p[:, j * LANE:(j + 1) * LANE], gg, qg, cq, sq)
                q_ref[0, :, grp * LANE:(grp + 1) * LANE] = q.astype(BF16)
        elif c <= n_qp + n_kp:
            for j in range(2):
                h = (c - 1 - n_qp) * 2 + j
                k = _norm_rope(p[:, j * LANE:(j + 1) * LANE], gg, kg_ref[:, h * LANE:(h + 1) * LANE],
                               ck_ref[:, j * LANE:(j + 1) * LANE], sk_ref[:, j * LANE:(j + 1) * LANE])
                k_ref[0, :, h * LANE:(h + 1) * LANE] = k.astype(BF16)
        else:
            _store_vt(vt_ref, p)


def _inproj_call(x, mod3, w_all, wvt, gg, qg, kg, cq, sq, ck, sk):
    B, S, _ = x.shape
    tm = TM_INPROJ
    ncol = w_all.shape[1]
    tab = pl.BlockSpec((tm, LANE), lambda b, i: (i, 0))
    tab2 = pl.BlockSpec((tm, 2 * LANE), lambda b, i: (i, 0))
    vec = pl.BlockSpec((1, LANE), lambda b, i: (0, 0))
    return pl.pallas_call(
        _inproj_kernel,
        grid=(B, S // tm),
        in_specs=[pl.BlockSpec((1, tm, D_MODEL), lambda b, i: (b, i, 0)),
                  pl.BlockSpec((1, 6, D_MODEL), lambda b, i: (b, 0, 0)),
                  pl.BlockSpec((D_MODEL, ncol), lambda b, i: (0, 0)),
                  pl.BlockSpec((KV_WIDTH, D_MODEL), lambda b, i: (0, 0)),
                  pl.BlockSpec((2 * LANE, LANE), lambda b, i: (0, 0)),
                  vec, pl.BlockSpec((1, K_COLS), lambda b, i: (0, 0)), tab, tab, tab2, tab2],
        out_specs=[pl.BlockSpec((1, tm, POOL_WIDTH), lambda b, i: (b, i, 0)),
                   pl.BlockSpec((1, tm, Q_COLS), lambda b, i: (b, i, 0)),
                   pl.BlockSpec((1, tm, K_COLS), lambda b, i: (b, i, 0)),
                   pl.BlockSpec((1, N_KV_HEADS * VT_ROWS, tm), lambda b, i: (b, 0, i))],
        out_shape=[jax.ShapeDtypeStruct((B, S, POOL_WIDTH), F32),
                   jax.ShapeDtypeStruct((B, S, Q_COLS), BF16),
                   jax.ShapeDtypeStruct((B, S, K_COLS), BF16),
                   jax.ShapeDtypeStruct((B, N_KV_HEADS * VT_ROWS, S), BF16)],
        name="inproj",
    )(x, mod3, w_all, wvt, gg, qg, kg, cq, sq, ck, sk)


def _ctx_kv_kernel(x_ref, mod_ref, w_ref, wvt_ref, kg_ref, k_ref, vt_ref):
    sh = mod_ref[0, 0:1, :]
    sc = mod_ref[0, 1:2, :]
    xm = (x_ref[0] * (1.0 + sc) + sh).astype(BF16)
    for pair in range(N_KV_HEADS // 2):
        p = _dot(xm, w_ref[:, pair * 2 * LANE:(pair + 1) * 2 * LANE])
        for j in range(2):
            h = pair * 2 + j
            kn = _norm_head(p[:, j * LANE:(j + 1) * LANE], kg_ref[:, h * LANE:(h + 1) * LANE])
            k_ref[0, :, h * LANE:(h + 1) * LANE] = kn.astype(BF16)
    _store_vt(vt_ref, lax.dot_general(wvt_ref[...], xm, _NT, preferred_element_type=F32))


def _ctx_kv_call(ctx, mod3, wk, wvt, kg, ctx_row):
    B, C, _ = ctx.shape
    return pl.pallas_call(
        _ctx_kv_kernel,
        grid=(B,),
        in_specs=[pl.BlockSpec((1, C, D_MODEL), lambda b: (b, 0, 0)),
                  pl.BlockSpec((1, 6, D_MODEL), lambda b: (ctx_row, 0, 0)),
                  pl.BlockSpec((D_MODEL, K_COLS), lambda b: (0, 0)),
                  pl.BlockSpec((KV_WIDTH, D_MODEL), lambda b: (0, 0)),
                  pl.BlockSpec((1, K_COLS), lambda b: (0, 0))],
        out_specs=[pl.BlockSpec((1, C, K_COLS), lambda b: (b, 0, 0)),
                   pl.BlockSpec((1, N_KV_HEADS * VT_ROWS, C), lambda b: (b, 0, 0))],
        out_shape=[jax.ShapeDtypeStruct((B, C, K_COLS), BF16),
                   jax.ShapeDtypeStruct((B, N_KV_HEADS * VT_ROWS, C), BF16)],
        name="ctx_kv",
    )(ctx, mod3, wk, wvt, kg)


POOL_PAD = 8


def _pool_kernel(u_ref, bd_ref, ps_ref, y_ref):
    S = u_ref.shape[1]
    n = S + 2 * POOL_PAD
    t = lax.broadcasted_iota(I32, (S, LANE), 0)
    lane = lax.broadcasted_iota(I32, (S, LANE), 1)
    zpad = jnp.zeros((POOL_PAD, LANE), F32)
    for half in range(POOL_WIDTH // LANE):
        u = u_ref[0, :, half * LANE:(half + 1) * LANE]
        ue = jnp.concatenate([zpad, u, zpad], axis=0)
        fwd = {1: ue}
        w = 1
        while w < POOL_WINDOWS[2 * half + 1]:
            fwd[2 * w] = fwd[w] + pltpu.roll(fwd[w], n - w, axis=0)
            w *= 2
        ds = []
        for win in POOL_WINDOWS[2 * half: 2 * half + 2]:
            hw = win // 2
            centred = pltpu.roll(fwd[win], hw, axis=0)[POOL_PAD:POOL_PAD + S]
            cnt = (jnp.minimum(t + hw, S) - jnp.maximum(t - hw, 0)).astype(F32)
            ds.append(centred / cnt - u)
        d = jnp.where(lane < POOL_GROUP, ds[0], ds[1]).astype(BF16)
        sl = slice(half * LANE, (half + 1) * LANE)
        y = _dot(d, bd_ref[sl, sl]) * ps_ref[:, sl]
        y_ref[0, :, sl] = y.astype(BF16)


def _pool_call(u, bd, ps):
    B, S, _ = u.shape
    return pl.pallas_call(
        _pool_kernel,
        grid=(B,),
        in_specs=[pl.BlockSpec((1, S, POOL_WIDTH), lambda b: (b, 0, 0)),
                  pl.BlockSpec((POOL_WIDTH, POOL_WIDTH), lambda b: (0, 0)),
                  pl.BlockSpec((1, POOL_WIDTH), lambda b: (0, 0))],
        out_specs=pl.BlockSpec((1, S, POOL_WIDTH), lambda b: (b, 0, 0)),
        out_shape=jax.ShapeDtypeStruct((B, S, POOL_WIDTH), BF16),
        name="pool",
    )(u, bd, ps)


ROW_REDUCE_WAYS = 16


def _reduce_rows(op, x):
    rows, lanes = x.shape
    if rows % (ROW_REDUCE_WAYS * SUBLANE) == 0:
        x = op(x.reshape(ROW_REDUCE_WAYS, rows // ROW_REDUCE_WAYS, lanes), axis=0)
    return op(x, axis=0, keepdims=True)


def _attn_kernel(q_ref, kc_ref, kl_ref, vtc_ref, vtl_ref, o_ref, s_buf, m_buf):
    t = pl.program_id(0)
    ts = s_buf.shape[-1]
    n_sub = q_ref.shape[1] // ts

    @pl.when(t == 0)
    def _():
        s_buf[1] = jnp.zeros_like(s_buf[1])
        m_buf[1] = jnp.zeros_like(m_buf[1])

    def step(slot):
        kc, kl = kc_ref[0], kl_ref[0]
        vtc, vtl = vtc_ref[0], vtl_ref[0]
        nc = kc.shape[0]
        for u in range(n_sub):
            cols = pl.ds(u * ts, ts)
            outs = []
            for j in range(GQA_GROUP):
                q = q_ref[0, cols, j * LANE:(j + 1) * LANE]
                sc = lax.dot_general(kc, q, _NT, preferred_element_type=F32)
                sl = lax.dot_general(kl, q, _NT, preferred_element_type=F32)
                s_buf[slot, u, j, 0:nc, :] = sc
                s_buf[slot, u, j, nc:, :] = sl
                mx = jnp.maximum(_reduce_rows(jnp.max, sc), _reduce_rows(jnp.max, sl))
                m_buf[slot, u, j] = jnp.broadcast_to(mx, (SUBLANE, ts))
                p = jnp.exp2(s_buf[1 - slot, u, j] - m_buf[1 - slot, u, j, 0:1, :]).astype(BF16)
                ol = _dot(vtc, p[0:nc]) + _dot(vtl, p[nc:])
                outs.append(ol[:HEAD_DIM] / ol[HEAD_DIM:HEAD_DIM + 1])
            outs.append(jnp.zeros((2 * LANE - GQA_GROUP * HEAD_DIM, ts), F32))
            o_ref[0, cols, :] = jnp.concatenate(outs, axis=0).T.astype(BF16)

    for slot in range(2):
        pl.when(t % 2 == slot)(functools.partial(step, slot))


def _attn_call(q, k_c, k_l, vt_c, vt_l):
    B, S, _ = q.shape
    C = k_c.shape[1]
    Lk = C + S
    tq = TQ_ATTN
    nq = S // tq
    n_items = B * N_KV_HEADS * nq

    def item(t):
        return t // (N_KV_HEADS * nq), (t // nq) % N_KV_HEADS, t % nq

    def cur(t):
        return item(jnp.minimum(t, n_items - 1))

    def prev(t):
        return item(jnp.maximum(t - 1, 0))

    return pl.pallas_call(
        _attn_kernel,
        grid=(n_items + 1,),
        in_specs=[pl.BlockSpec((1, tq, GQA_GROUP * LANE), lambda t: (cur(t)[0], cur(t)[2], cur(t)[1] // 2)),
                  pl.BlockSpec((1, C, LANE), lambda t: (cur(t)[0], 0, cur(t)[1])),
                  pl.BlockSpec((1, S, LANE), lambda t: (cur(t)[0], 0, cur(t)[1])),
                  pl.BlockSpec((1, VT_ROWS, C), lambda t: (prev(t)[0], prev(t)[1], 0)),
                  pl.BlockSpec((1, VT_ROWS, S), lambda t: (prev(t)[0], prev(t)[1], 0))],
        out_specs=pl.BlockSpec((1, tq, 2 * LANE), lambda t: (prev(t)[0], prev(t)[2], prev(t)[1])),
        out_shape=jax.ShapeDtypeStruct((B, S, ATTN_PAD), BF16),
        scratch_shapes=[pltpu.VMEM((2, tq // ATTN_SUB, GQA_GROUP, Lk, ATTN_SUB), F32),
                        pltpu.VMEM((2, tq // ATTN_SUB, GQA_GROUP, SUBLANE, ATTN_SUB), F32)],
        compiler_params=pltpu.CompilerParams(dimension_semantics=("arbitrary",)),
        name="attn",
    )(q, k_c, k_l, vt_c, vt_l)


def _mix_kernel(x_ref, yp_ref, at_ref, wop_ref, woa_ref, mod_ref, g_ref, b_ref, wr_ref, rb_ref, tri_ref, trie_ref,
                xmid_ref, hp_ref, lprow_ref, lpcol_ref, wts_ref, segn_ref, segb_ref, cnt_ref, carry):
    i = pl.program_id(0)
    tm = tri_ref.shape[0]
    n_sub = x_ref.shape[0] // tm

    @pl.when(i == 0)
    def _():
        carry[...] = jnp.zeros_like(carry)

    g1 = mod_ref[0, 2:3, :]
    sh2 = mod_ref[0, 3:4, :]
    sc2 = mod_ref[0, 4:5, :]

    def out_proj(s):
        rows = pl.ds(s * tm, tm)
        return _dot(yp_ref[rows, :], wop_ref[...]) + _dot(at_ref[rows, :], woa_ref[...])

    def norm_router(s, mix):
        rows = pl.ds(s * tm, tm)
        xmid = _layer_norm(ALPHA * x_ref[rows, :] + g1 * mix, g_ref[...], b_ref[...])
        xmid_ref[rows, :] = xmid
        h = xmid * (1.0 + sc2) + sh2
        _store_rows(hp_ref, _pack2(h[:, :HALF_D], h[:, HALF_D:]), s * tm)
        h_hi = h.astype(BF16)
        h_lo = (h - h_hi.astype(F32)).astype(BF16)
        a = lax.dot_general(wr_ref[...], h_hi, _NT, preferred_element_type=F32)
        b = lax.dot_general(wr_ref[0:N_EXPERTS, :], h_lo, _NT, preferred_element_type=F32)
        return a[:N_EXPERTS] + a[N_EXPERTS:] + b

    nxt = out_proj(0)
    logits = []
    for s in range(n_sub):
        mix = nxt
        if s + 1 < n_sub:
            nxt = out_proj(s + 1)
        logits.append(norm_router(s, mix))
    base = carry[:, 0:1]
    for s in range(n_sub):
        segb_ref[s] = jnp.broadcast_to(base, carry.shape).astype(I32)
        base = base + _route(logits[s], rb_ref, tri_ref, trie_ref, lprow_ref, lpcol_ref, wts_ref, segn_ref, s)
    carry[...] = jnp.broadcast_to(base, carry.shape)
    cnt_ref[...] = carry[...].astype(I32)


def _route(logits, rb_ref, tri_ref, trie_ref, lprow_ref, lpcol_ref, wts_ref, segn_ref, s):
    tm = logits.shape[1]
    cols = pl.ds(s * tm, tm)
    score = jax.nn.sigmoid(logits)
    sel = score + rb_ref[...]
    neg = jnp.float32(-jnp.inf)
    sub = lax.broadcasted_iota(I32, (GROUP_SIZE, tm), 0)
    grp_sel, grp_score = [], []
    for g in range(N_GROUPS):
        sg = sel[g * GROUP_SIZE:(g + 1) * GROUP_SIZE, :]
        m1 = jnp.max(sg, axis=0, keepdims=True)
        i1 = jnp.min(jnp.where(sg == m1, sub, GROUP_SIZE), axis=0, keepdims=True)
        m2 = jnp.max(jnp.where(sub == i1, neg, sg), axis=0, keepdims=True)
        grp_sel.append(sg)
        grp_score.append(m1 + m2)
    masked = []
    for g in range(N_GROUPS):
        ahead = jnp.zeros((1, tm), I32)
        for o in range(N_GROUPS):
            if o == g:
                continue
            beats = (grp_score[o] > grp_score[g]) | ((grp_score[o] == grp_score[g]) & (o < g))
            ahead = ahead + beats.astype(I32)
        masked.append(jnp.where(ahead < TOPK_GROUPS, grp_sel[g], neg))
    cur = jnp.concatenate(masked, axis=0)
    row = lax.broadcasted_iota(I32, (N_EXPERTS, tm), 0)
    member = jnp.zeros((N_EXPERTS, tm), F32)
    picks, wsel = [], []
    for k in range(TOP_K):
        mx = jnp.max(cur, axis=0, keepdims=True)
        ei = jnp.min(jnp.where(cur == mx, row, N_EXPERTS), axis=0, keepdims=True)
        hit = row == ei
        picks.append(ei)
        wsel.append(jnp.sum(jnp.where(hit, score, 0.0), axis=0, keepdims=True))
        cur = jnp.where(hit, neg, cur)
        member = jnp.where(hit, 1.0, member)
    tot = wsel[0]
    for k in range(1, TOP_K):
        tot = tot + wsel[k]
    n_e = jnp.sum(member, axis=1, keepdims=True)
    n_b = jnp.broadcast_to(n_e, (N_EXPERTS, LANE))
    pos = _dot(member.astype(BF16), tri_ref[...]) + _dot(trie_ref[...], n_b.astype(BF16))[:, 0:1]
    segn_ref[s] = n_b.astype(I32)
    lrows = []
    for k in range(TOP_K):
        lp = jnp.sum(jnp.where(row == picks[k], pos, 0.0), axis=0, keepdims=True)
        lprow_ref[k:k + 1, cols] = lp.astype(I32)
        lrows.append(lp)
    fill = [jnp.zeros((LANE - TOP_K, tm), F32)]
    lpcol_ref[cols, :] = jnp.concatenate(lrows + fill, axis=0).T.astype(I32)
    wrows = [wsel[k] / tot * ROUTED_SCALE for k in range(TOP_K)]
    wts_ref[cols, :] = jnp.concatenate(wrows + fill, axis=0).T
    return n_e


def _mix_call(x2, ypool, attn, wop, woa, mod3, g, b, wr_t, rb, tri, tri_e, tiles_per_seq):
    T = x2.shape[0]
    tm = TM_MIX
    n_sub = tm // MIX_SUB
    row = lambda i: (i, 0)
    fixed = lambda i: (0, 0)
    tok = lambda i: (0, i)
    seg = pl.BlockSpec((n_sub, N_EXPERTS, LANE), lambda i: (i, 0, 0))
    seg_shape = jax.ShapeDtypeStruct((T // MIX_SUB, N_EXPERTS, LANE), I32)
    return pl.pallas_call(
        _mix_kernel,
        grid=(T // tm,),
        in_specs=[pl.BlockSpec((tm, D_MODEL), row),
                  pl.BlockSpec((tm, POOL_WIDTH), row),
                  pl.BlockSpec((tm, ATTN_PAD), row),
                  pl.BlockSpec((POOL_WIDTH, D_MODEL), fixed),
                  pl.BlockSpec((ATTN_PAD, D_MODEL), fixed),
                  pl.BlockSpec((1, 6, D_MODEL), lambda i: (i // tiles_per_seq, 0, 0)),
                  pl.BlockSpec((1, D_MODEL), fixed),
                  pl.BlockSpec((1, D_MODEL), fixed),
                  pl.BlockSpec((2 * N_EXPERTS, D_MODEL), fixed),
                  pl.BlockSpec((N_EXPERTS, 1), fixed),
                  pl.BlockSpec((MIX_SUB, MIX_SUB), fixed),
                  pl.BlockSpec((N_EXPERTS, N_EXPERTS), fixed)],
        out_specs=[pl.BlockSpec((tm, D_MODEL), row),
                   pl.BlockSpec((tm * ROW_SUB, LANE), row),
                   pl.BlockSpec((TOP_K, tm), tok),
                   pl.BlockSpec((tm, LANE), row),
                   pl.BlockSpec((tm, LANE), row),
                   seg, seg,
                   pl.BlockSpec((N_EXPERTS, LANE), fixed)],
        out_shape=[jax.ShapeDtypeStruct((T, D_MODEL), F32),
                   jax.ShapeDtypeStruct((T * ROW_SUB, LANE), I32),
                   jax.ShapeDtypeStruct((TOP_K, T), I32),
                   jax.ShapeDtypeStruct((T, LANE), I32),
                   jax.ShapeDtypeStruct((T, LANE), F32),
                   seg_shape, seg_shape,
                   jax.ShapeDtypeStruct((N_EXPERTS, LANE), I32)],
        scratch_shapes=[pltpu.VMEM((N_EXPERTS, LANE), F32)],
        compiler_params=pltpu.CompilerParams(dimension_semantics=("arbitrary",)),
        name="mix",
    )(x2, ypool, attn, wop, woa, mod3, g, b, wr_t, rb, tri, tri_e)


def _segment_copies(seg_ref, tile, make):
    copies = []
    for e in range(N_EXPERTS):
        j = tile * N_EXPERTS + e
        n = seg_ref[1, j]
        copies.append((n > 0, make(pl.multiple_of(seg_ref[2, j] * ROW_SUB, ROW_SUB),
                                   pl.multiple_of(seg_ref[0, j] * ROW_SUB, ROW_SUB), n * ROW_SUB)))
    return copies


def _run(copies, act):
    for cond, cp in copies:
        pl.when(cond)(functools.partial(act, cp))


def _sort_matrix(lp_ref, n_rows):
    tm = lp_ref.shape[1]
    j = lax.broadcasted_iota(I32, (n_rows, tm), 0)
    hit = j == lp_ref[0:1, :]
    for k in range(1, TOP_K):
        hit = hit | (j == lp_ref[k:k + 1, :])
    return jnp.where(hit, 1.0, 0.0).astype(BF16)


def _dispatch_kernel(pst_ref, plen_ref, nreal_ref, seg_ref, hp_ref, lp_ref, xs_hbm, xt, zbuf, ssem, zsem):
    tm = hp_ref.shape[0] // ROW_SUB
    bm = zbuf.shape[0] // ROW_SUB
    nb = xs_hbm.shape[0] // (bm * ROW_SUB)
    i = pl.program_id(0)
    nt = pl.num_programs(0)

    def seg_copies(tile, s):
        return _segment_copies(seg_ref, tile, lambda t_row, s_row, rows: pltpu.make_async_copy(
            xt.at[s, pl.ds(t_row, rows)], xs_hbm.at[pl.ds(s_row, rows)], ssem.at[s]))

    def pad_fill(e):
        n = plen_ref[e] * ROW_SUB
        start = pl.multiple_of(pst_ref[e] * ROW_SUB, ROW_SUB)
        return pltpu.make_async_copy(zbuf.at[pl.ds(0, n)], xs_hbm.at[pl.ds(start, n)], zsem)

    def dead_fill(j):
        return pltpu.make_async_copy(zbuf, xs_hbm.at[pl.ds(j * bm * ROW_SUB, bm * ROW_SUB)], zsem)

    def fills(act):
        for e in range(N_EXPERTS):
            pl.when(plen_ref[e] > 0)(functools.partial(lambda e: act(pad_fill(e)), e))
        for j in range(nb - N_EXPERTS, nb):
            pl.when(j >= nreal_ref[0])(functools.partial(lambda j: act(dead_fill(j)), j))

    @pl.when(i == 0)
    def _():
        zbuf[...] = jnp.zeros_like(zbuf)
        fills(lambda cp: cp.start())

    def scatter(s):
        halves = [_unpack2(w) for w in _load_rows(hp_ref, tm)]
        h_lo = jnp.concatenate([lo.astype(BF16) for lo, _ in halves], axis=1)
        h_hi = jnp.concatenate([hi.astype(BF16) for _, hi in halves], axis=1)
        sort = _sort_matrix(lp_ref, TOP_K * tm)
        _store_rows(xt.at[s], _pack2(_dot(sort, h_lo), _dot(sort, h_hi)))
        _run(seg_copies(i, s), lambda cp: cp.start())

        @pl.when(i >= 1)
        def _():
            _run(seg_copies(i - 1, 1 - s), lambda cp: cp.wait())

        @pl.when(i == nt - 1)
        def _():
            _run(seg_copies(i, s), lambda cp: cp.wait())

    for s in range(2):
        pl.when(i % 2 == s)(functools.partial(scatter, s))

    @pl.when(i == 0)
    def _():
        fills(lambda cp: cp.wait())


def _dispatch_call(pad_row_start, pad_len, n_real, seg, hp, lprow, nb):
    tm = TM_ROUTE
    bm = BM_MOE
    T = hp.shape[0] // ROW_SUB
    grid_spec = pltpu.PrefetchScalarGridSpec(
        num_scalar_prefetch=4,
        grid=(T // tm,),
        in_specs=[pl.BlockSpec((tm * ROW_SUB, LANE), lambda i, a, b, c, d: (i, 0)),
                  pl.BlockSpec((TOP_K, tm), lambda i, a, b, c, d: (0, i))],
        out_specs=pl.BlockSpec(memory_space=pl.ANY),
        scratch_shapes=[pltpu.VMEM((2, TOP_K * tm * ROW_SUB, LANE), I32),
                        pltpu.VMEM((bm * ROW_SUB, LANE), I32),
                        pltpu.SemaphoreType.DMA((2,)),
                        pltpu.SemaphoreType.DMA],
    )
    return pl.pallas_call(
        _dispatch_kernel,
        grid_spec=grid_spec,
        out_shape=jax.ShapeDtypeStruct((nb * bm * ROW_SUB, LANE), I32),
        compiler_params=pltpu.CompilerParams(dimension_semantics=("arbitrary",)),
        name="dispatch",
    )(pad_row_start, pad_len, n_real, seg, hp, lprow)


def _moe_kernel(be_ref, nreal_ref, x_ref, wg_ref, wu_ref, wd_ref, y_ref, wgb, wub, wdb):
    bm = x_ref.shape[0] // ROW_SUB
    i = pl.program_id(0)
    n = nreal_ref[0]

    @pl.when(i < n)
    def _():
        e_changed = (i == 0) | (be_ref[i] != be_ref[jnp.maximum(i - 1, 0)])

        @pl.when(e_changed)
        def _():
            wgb[...] = wg_ref[0].astype(BF16)
            wub[...] = wu_ref[0].astype(BF16)
            wdb[...] = wd_ref[0].astype(BF16)

        cm = bm // MOE_ROW_CHUNKS

        def up(c):
            halves = [_unpack2(w) for w in _load_rows(x_ref, cm, c * cm)]
            xb = jnp.concatenate([lo.astype(BF16) for lo, _ in halves] + [hi.astype(BF16) for _, hi in halves],
                                 axis=1)
            return (_silu(_dot(xb, wgb[...])) * _dot(xb, wub[...])).astype(BF16)

        nxt = up(0)
        for c in range(MOE_ROW_CHUNKS):
            a = nxt
            if c + 1 < MOE_ROW_CHUNKS:
                nxt = up(c + 1)
            y = _dot(a, wdb[...])
            _store_rows(y_ref, _pack2(y[:, :HALF_D], y[:, HALF_D:]), c * cm)

    @pl.when(i >= n)
    def _():
        y_ref[...] = jnp.zeros_like(y_ref)


def _moe_call(block_expert, n_real, xs, w_gate, w_up, w_down):
    bm = BM_MOE
    nb = block_expert.shape[0]
    wmap = lambda i, be, nr: (be[i], 0, 0)
    rows = pl.BlockSpec((bm * ROW_SUB, LANE), lambda i, be, nr: (i, 0))
    grid_spec = pltpu.PrefetchScalarGridSpec(
        num_scalar_prefetch=2,
        grid=(nb,),
        in_specs=[rows,
                  pl.BlockSpec((1, D_MODEL, EXPERT_FF), wmap),
                  pl.BlockSpec((1, D_MODEL, EXPERT_FF), wmap),
                  pl.BlockSpec((1, EXPERT_FF, D_MODEL), wmap)],
        out_specs=rows,
        scratch_shapes=[pltpu.VMEM((D_MODEL, EXPERT_FF), BF16),
                        pltpu.VMEM((D_MODEL, EXPERT_FF), BF16),
                        pltpu.VMEM((EXPERT_FF, D_MODEL), BF16)],
    )
    return pl.pallas_call(
        _moe_kernel,
        grid_spec=grid_spec,
        out_shape=jax.ShapeDtypeStruct(xs.shape, I32),
        compiler_params=pltpu.CompilerParams(dimension_semantics=("arbitrary",)),
        name="moe",
    )(block_expert, n_real, xs, w_gate, w_up, w_down)


def _combine_kernel(seg_ref, ys_hbm, lp_ref, wc_ref, xmid_ref, mod_ref, wsg_ref, wsu_ref, wsd_ref, g_ref, b_ref,
                    o_ref, yt, gsem):
    tm = xmid_ref.shape[0]
    n_rows = TOP_K * tm
    i = pl.program_id(0)
    nt = pl.num_programs(0)

    def seg_copies(tile, s):
        return _segment_copies(seg_ref, tile, lambda t_row, s_row, rows: pltpu.make_async_copy(
            ys_hbm.at[pl.ds(s_row, rows)], yt.at[s, pl.ds(t_row, rows)], gsem.at[s]))

    @pl.when(i == 0)
    def _():
        _run(seg_copies(0, 0), lambda cp: cp.start())

    def finish(s):
        lane = lax.broadcasted_iota(I32, (tm, n_rows), 1)
        lp, wc = lp_ref[...], wc_ref[...]
        wsel = jnp.zeros((tm, n_rows), F32)
        for k in range(TOP_K):
            wsel = jnp.where(lane == lp[:, k:k + 1], wc[:, k:k + 1], wsel)
        w_hi = wsel.astype(BF16)
        w_lo = (wsel - w_hi.astype(F32)).astype(BF16)
        halves = [_unpack2(w) for w in _load_rows(yt.at[s], n_rows)]
        y = jnp.concatenate([lo.astype(BF16) for lo, _ in halves] + [hi.astype(BF16) for _, hi in halves], axis=1)
        both = _dot(jnp.concatenate([w_hi, w_lo], axis=0), y)
        routed = both[:tm] + both[tm:]
        xmid = xmid_ref[...]
        sh2 = mod_ref[0, 3:4, :]
        sc2 = mod_ref[0, 4:5, :]
        g2 = mod_ref[0, 5:6, :]
        hb = (xmid * (1.0 + sc2) + sh2).astype(BF16)
        a = _silu(_dot(hb, wsg_ref[...])) * _dot(hb, wsu_ref[...])
        shared = _dot(a.astype(BF16), wsd_ref[...])
        o_ref[...] = _layer_norm(ALPHA * xmid + g2 * (routed + shared), g_ref[...], b_ref[...])

    def step(s):
        @pl.when(i + 1 < nt)
        def _():
            _run(seg_copies(i + 1, 1 - s), lambda cp: cp.start())

        _run(seg_copies(i, s), lambda cp: cp.wait())
        finish(s)

    for s in range(2):
        pl.when(i % 2 == s)(functools.partial(step, s))


def _combine_call(seg, ys, lpcol, wcol, xmid, mod3, wsg, wsu, wsd, g, b, tiles_per_seq):
    T = xmid.shape[0]
    tm = TM_ROUTE
    fixed = lambda i, sg: (0, 0)
    row = lambda i, sg: (i, 0)
    grid_spec = pltpu.PrefetchScalarGridSpec(
        num_scalar_prefetch=1,
        grid=(T // tm,),
        in_specs=[pl.BlockSpec(memory_space=pl.ANY),
                  pl.BlockSpec((tm, LANE), row),
                  pl.BlockSpec((tm, LANE), row),
                  pl.BlockSpec((tm, D_MODEL), row),
                  pl.BlockSpec((1, 6, D_MODEL), lambda i, sg: (i // tiles_per_seq, 0, 0)),
                  pl.BlockSpec((D_MODEL, SHARED_FF), fixed),
                  pl.BlockSpec((D_MODEL, SHARED_FF), fixed),
                  pl.BlockSpec((SHARED_FF, D_MODEL), fixed),
                  pl.BlockSpec((1, D_MODEL), fixed),
                  pl.BlockSpec((1, D_MODEL), fixed)],
        out_specs=pl.BlockSpec((tm, D_MODEL), row),
        scratch_shapes=[pltpu.VMEM((2, TOP_K * tm * ROW_SUB, LANE), I32),
                        pltpu.SemaphoreType.DMA((2,))],
    )
    return pl.pallas_call(
        _combine_kernel,
        grid_spec=grid_spec,
        out_shape=jax.ShapeDtypeStruct((T, D_MODEL), F32),
        compiler_params=pltpu.CompilerParams(dimension_semantics=("arbitrary",)),
        name="combine",
    )(seg, ys, lpcol, wcol, xmid, mod3, wsg, wsu, wsd, g, b)


def _group(a, b):
    ref = a if a is not None else b
    z = jnp.zeros(ref.shape[:-1], ref.dtype)
    a0, a1 = (a[..., 0], a[..., 1]) if a is not None else (z, z)
    b0, b1 = (b[..., 0], b[..., 1]) if b is not None else (z, z)
    return jnp.concatenate([a0, b0, a1, b1], axis=-1)


def _q_groups(per_head):
    return jnp.concatenate([_group(per_head(3 * (2 * p) + j), per_head(3 * (2 * p + 1) + j))
                            for p in range(N_KV_HEADS // 2) for j in range(GQA_GROUP)], axis=-1)


def _k_groups(per_head):
    return jnp.concatenate([_group(per_head(h), None) if h % 2 == 0 else _group(None, per_head(h))
                            for h in range(N_KV_HEADS)], axis=-1)


def _rope_pairs(seq):
    inv_freq = ROPE_THETA ** (-jnp.arange(0, HALF, 2, dtype=F32) / HALF)
    pos = jnp.arange(seq)
    rowp = (pos // GRID_W).astype(F32)
    colp = (pos % GRID_W).astype(F32)
    ang = jnp.concatenate([rowp[:, None] * inv_freq, colp[:, None] * inv_freq], axis=-1)
    cos, sin = jnp.cos(ang), jnp.sin(ang)
    return jnp.stack([cos, cos], axis=-1), jnp.stack([-sin, sin], axis=-1)


def kernel(x, c, ctx, c_ctx, w_mod, b_mod, w_in, q_norm, k_norm, pool_w, pool_scale, w_out, ln1_g, ln1_b,
           w_router, router_bias, w_gate, w_up, w_down, ws_gate, ws_up, ws_down, ln2_g, ln2_b):
    B, S, D = x.shape
    C = ctx.shape[1]
    T = B * S
    assert D == D_MODEL and w_mod.shape[0] == DEPTH and B + 1 <= MOD_ROWS
    assert S % TM_INPROJ == 0 and S % TQ_ATTN == 0 and S % TM_MIX == 0 and S % TM_ROUTE == 0
    assert S % GRID_W == 0 and C % SUBLANE == 0

    cc = jnp.concatenate([c, c_ctx[None, :], jnp.zeros((MOD_ROWS - B - 1, D), F32)], axis=0)
    mod3 = _mod_call(cc, w_mod[0], b_mod[0][None, :]).reshape(MOD_ROWS, 6, D)

    w = w_in[0]
    o1, o2, o3 = POOL_WIDTH, POOL_WIDTH + ATTN_WIDTH, POOL_WIDTH + ATTN_WIDTH + KV_WIDTH
    wq4 = w[:, o1:o2].reshape(D, N_HEADS, HALF, 2)
    wk4 = w[:, o2:o3].reshape(D, N_KV_HEADS, HALF, 2)
    wq = _q_groups(lambda h: wq4[:, h])
    wk = _k_groups(lambda h: wk4[:, h])
    wk = wk.astype(BF16)
    w_all = jnp.concatenate([w[:, :o1].astype(BF16), wq.astype(BF16), wk], axis=1)
    wvt = w[:, o3:].T.astype(BF16)
    qg4, kg4 = q_norm[0].reshape(1, HALF, 2), k_norm[0].reshape(1, HALF, 2)
    qg = _group(qg4, qg4)
    kg = _k_groups(lambda h: kg4)
    cos_p, sin_p = _rope_pairs(S)
    qscale = HEAD_DIM ** -0.5 * LOG2_E
    cq, sq = _group(cos_p, cos_p) * qscale, _group(sin_p, sin_p) * qscale
    ck = jnp.concatenate([_group(cos_p, None), _group(None, cos_p)], axis=-1)
    sk = jnp.concatenate([_group(sin_p, None), _group(None, sin_p)], axis=-1)
    slot = (jnp.arange(LANE) // HALF) % 2
    same = (slot[:, None] == slot[None, :]).astype(F32) * (1.0 / HEAD_DIM)
    gg = jnp.concatenate([same, same], axis=0).astype(BF16)
    u, q, k_l, vt_l = _inproj_call(x, mod3, w_all, wvt, gg, qg, kg, cq, sq, ck, sk)
    k_c, vt_c = _ctx_kv_call(ctx, mod3, wk, wvt, kg, B)
    attn = _attn_call(q, k_c, k_l, vt_c, vt_l)

    bd = jax.scipy.linalg.block_diag(*[pool_w[0, g] for g in range(len(POOL_WINDOWS))]).astype(BF16)
    ypool = _pool_call(u, bd, pool_scale[0][None, :])

    wo = w_out[0]
    wop = wo[:POOL_WIDTH].astype(BF16)
    woa = wo[POOL_WIDTH:].reshape(N_KV_HEADS, GQA_GROUP * HEAD_DIM, D)
    woa = jnp.pad(woa, ((0, 0), (0, 2 * LANE - GQA_GROUP * HEAD_DIM), (0, 0))).reshape(ATTN_PAD, D).astype(BF16)
    tri = (jnp.arange(MIX_SUB)[:, None] < jnp.arange(MIX_SUB)[None, :]).astype(BF16)
    wr = w_router[0].T
    wr_hi = wr.astype(BF16)
    wr_lo = (wr - wr_hi.astype(F32)).astype(BF16)
    tri_e = (jnp.arange(N_EXPERTS)[None, :] < jnp.arange(N_EXPERTS)[:, None]).astype(BF16)
    xmid, hp, lprow, lpcol, wcol, seg_n, seg_base, counts = _mix_call(
        x.reshape(T, D), ypool.reshape(T, POOL_WIDTH), attn.reshape(T, ATTN_PAD), wop, woa, mod3,
        ln1_g[0][None, :], ln1_b[0][None, :], jnp.concatenate([wr_hi, wr_lo], axis=0),
        router_bias[0][:, None], tri, tri_e, S // TM_MIX)

    bm = BM_MOE
    counts = counts[:, 0]
    padded = ((counts + bm - 1) // bm) * bm
    pad_end = jnp.cumsum(padded)
    pad_start = pad_end - padded
    nb = T * TOP_K // bm + N_EXPERTS
    n_real = (pad_end[-1] // bm).astype(I32).reshape(1)
    blk_row = jnp.arange(nb, dtype=I32) * bm
    block_expert = jnp.minimum(jnp.sum((pad_end[None, :] <= blk_row[:, None]).astype(I32), axis=1), N_EXPERTS - 1)

    seg_n = seg_n[:, :, 0]
    seg = jnp.stack([pad_start[None, :] + seg_base[:, :, 0], seg_n, jnp.cumsum(seg_n, axis=1) - seg_n])
    seg = seg.reshape(3, -1).astype(I32)
    xs = _dispatch_call((pad_start + counts).astype(I32), (padded - counts).astype(I32), n_real, seg, hp, lprow, nb)
    ys = _moe_call(block_expert.astype(I32), n_real, xs, w_gate[0], w_up[0], w_down[0])
    out = _combine_call(seg, ys, lpcol, wcol, xmid, mod3, ws_gate[0].astype(BF16), ws_up[0].astype(BF16),
                        ws_down[0].astype(BF16), ln2_g[0][None, :], ln2_b[0][None, :], S // TM_ROUTE)
    return out.reshape(B, S, D)
```

```python
import functools

import jax
import jax.numpy as jnp
from jax import lax
from jax.experimental import pallas as pl
from jax.experimental.pallas import tpu as pltpu

F32 = jnp.float32
BF16 = jnp.bfloat16
I32 = jnp.int32

LANE = 128
SUBLANE = 8

D_MODEL = 1024
GRID_W = 64
POOL_WIDTH = 256
POOL_WINDOWS = (2, 4, 8, 16)
POOL_GROUP = 64
HEAD_DIM = 64
HALF = HEAD_DIM // 2
N_HEADS = 12
N_KV_HEADS = 4
GQA_GROUP = N_HEADS // N_KV_HEADS
ATTN_WIDTH = N_HEADS * HEAD_DIM
KV_WIDTH = N_KV_HEADS * HEAD_DIM
ROPE_THETA = 10000.0
N_EXPERTS = 64
TOP_K = 8
N_GROUPS = 8
GROUP_SIZE = N_EXPERTS // N_GROUPS
TOPK_GROUPS = 4
EXPERT_FF = 256
SHARED_FF = 256
ROUTED_SCALE = 2.5
DEPTH = 1
ALPHA = (2.0 * DEPTH) ** 0.25
LN_EPS = 1e-5
RMS_EPS = 1e-6

MOD_ROWS = 24
HALF_D = D_MODEL // 2
ROW_SUB = HALF_D // LANE
Q_COLS = N_HEADS // 2 * LANE
K_COLS = N_KV_HEADS * LANE
ATTN_PAD = N_KV_HEADS * 2 * LANE
VT_ROWS = HEAD_DIM + 16
LOG2_E = 1.4426950408889634

TN_MOD = 512
TM_INPROJ = 512
TQ_ATTN = 512
ATTN_SUB = 256
TM_MIX = 512
MIX_SUB = 256
BM_MOE = 1024
MOE_ROW_CHUNKS = 4
TM_ROUTE = 256
SORT_CHUNKS = 4

_NT = (((1,), (1,)), ((), ()))


def _dot(a, b):
    return jnp.dot(a, b, preferred_element_type=F32)


def _pack2(lo, hi):
    lo_bits = lax.bitcast_convert_type(lo.astype(BF16).astype(F32), I32)
    hi_bits = lax.bitcast_convert_type(hi.astype(BF16).astype(F32), I32)
    return lax.shift_right_logical(lo_bits, 16) | (hi_bits & jnp.int32(-65536))


def _unpack2(w):
    lo = lax.bitcast_convert_type(lax.shift_left(w, 16), F32)
    hi = lax.bitcast_convert_type(w & jnp.int32(-65536), F32)
    return lo, hi


def _store_rows(ref, packed, row0=0):
    n = packed.shape[0]
    for s in range(ROW_SUB):
        ref[pl.ds(row0 * ROW_SUB + s, n, stride=ROW_SUB), :] = packed[:, s * LANE:(s + 1) * LANE]


def _load_rows(ref, n, row0=0):
    return [ref[pl.ds(row0 * ROW_SUB + s, n, stride=ROW_SUB), :] for s in range(ROW_SUB)]


def _silu(x):
    return x * jax.nn.sigmoid(x)


def _layer_norm(r, g, b):
    mu = jnp.mean(r, axis=-1, keepdims=True)
    d = r - mu
    var = jnp.mean(d * d, axis=-1, keepdims=True)
    return d * lax.rsqrt(var + LN_EPS) * g + b


def _mod_kernel(c_ref, w_ref, b_ref, o_ref):
    a = _silu(c_ref[...])
    o_ref[...] = jnp.dot(a, w_ref[...], precision=lax.Precision.HIGHEST,
                         preferred_element_type=F32) + b_ref[...]


def _mod_call(cc, w_mod, b_mod):
    n = w_mod.shape[1]
    tn = TN_MOD
    return pl.pallas_call(
        _mod_kernel,
        grid=(n // tn,),
        in_specs=[pl.BlockSpec((MOD_ROWS, D_MODEL), lambda j: (0, 0)),
                  pl.BlockSpec((D_MODEL, tn), lambda j: (0, j)),
                  pl.BlockSpec((1, tn), lambda j: (0, j))],
        out_specs=pl.BlockSpec((MOD_ROWS, tn), lambda j: (0, j)),
        out_shape=jax.ShapeDtypeStruct((MOD_ROWS, n), F32),
        name="mod",
    )(cc, w_mod, b_mod)


def _store_vt(vt_ref, vt):
    rows = vt.shape[1]
    tail_row = lax.broadcasted_iota(I32, (VT_ROWS - HEAD_DIM, rows), 0)
    tail = jnp.where(tail_row == 0, 1.0, 0.0).astype(F32)
    for h in range(N_KV_HEADS):
        blk = jnp.concatenate([vt[h * HEAD_DIM:(h + 1) * HEAD_DIM], tail], axis=0)
        vt_ref[0, h * VT_ROWS:(h + 1) * VT_ROWS, :] = blk.astype(BF16)


def _norm_head(seg, g):
    ms = jnp.sum(seg * seg, axis=-1, keepdims=True) * (1.0 / HEAD_DIM)
    return seg * lax.rsqrt(ms + RMS_EPS) * g


def _norm_rope(seg, gg, g, c, s):
    sq = seg * seg
    hi = sq.astype(BF16)
    lo = (sq - hi.astype(F32)).astype(BF16)
    ms = _dot(jnp.concatenate([hi, lo], axis=1), gg)
    xn = seg * lax.rsqrt(ms + RMS_EPS) * g
    return xn * c + pltpu.roll(xn, LANE // 2, axis=1) * s


def _inproj_kernel(x_ref, mod_ref, w_ref, wvt_ref, gg_ref, qg_ref, kg_ref, cq_ref, sq_ref, ck_ref, sk_ref,
                   u_ref, q_ref, k_ref, vt_ref):
    sh = mod_ref[0, 0:1, :]
    sc = mod_ref[0, 1:2, :]
    xm = (x_ref[0] * (1.0 + sc) + sh).astype(BF16)
    gg, qg = gg_ref[...], qg_ref[...]
    cq, sq = cq_ref[...], sq_ref[...]
    n_qp = Q_COLS // (2 * LANE)
    n_kp = K_COLS // (2 * LANE)
    n_chunks = 1 + n_qp + n_kp + 1

    def chunk(c):
        if c == n_chunks - 1:
            return lax.dot_general(wvt_ref[...], xm, _NT, preferred_element_type=F32)
        return _dot(xm, w_ref[:, c * 2 * LANE:(c + 1) * 2 * LANE])

    nxt = chunk(0)
    for c in range(n_chunks):
        p = nxt
        if c + 1 < n_chunks:
            nxt = chunk(c + 1)
        if c == 0:
            u_ref[0] = p
        elif c <= n_qp:
            for j in range(2):
                grp = (c - 1) * 2 + j
                q = _norm_rope(p[:, j * LANE:(j + 1) * LANE], gg, qg, cq, sq)
                q_ref[0, :, grp * LANE:(grp + 1) * LANE] = q.astype(BF16)
        elif c <= n_qp + n_kp:
            for j in range(2):
                h = (c - 1 - n_qp) * 2 + j
                k = _norm_rope(p[:, j * LANE:(j + 1) * LANE], gg, kg_ref[:, h * LANE:(h + 1) * LANE],
                               ck_ref[:, j * LANE:(j + 1) * LANE], sk_ref[:, j * LANE:(j + 1) * LANE])
                k_ref[0, :, h * LANE:(h + 1) * LANE] = k.astype(BF16)
        else:
            _store_vt(vt_ref, p)


def _inproj_call(x, mod3, w_all, wvt, gg, qg, kg, cq, sq, ck, sk):
    B, S, _ = x.shape
    tm = TM_INPROJ
    ncol = w_all.shape[1]
    tab = pl.BlockSpec((tm, LANE), lambda b, i: (i, 0))
    tab2 = pl.BlockSpec((tm, 2 * LANE), lambda b, i: (i, 0))
    vec = pl.BlockSpec((1, LANE), lambda b, i: (0, 0))
    return pl.pallas_call(
        _inproj_kernel,
        grid=(B, S // tm),
        in_specs=[pl.BlockSpec((1, tm, D_MODEL), lambda b, i: (b, i, 0)),
                  pl.BlockSpec((1, 6, D_MODEL), lambda b, i: (b, 0, 0)),
                  pl.BlockSpec((D_MODEL, ncol), lambda b, i: (0, 0)),
                  pl.BlockSpec((KV_WIDTH, D_MODEL), lambda b, i: (0, 0)),
                  pl.BlockSpec((2 * LANE, LANE), lambda b, i: (0, 0)),
                  vec, pl.BlockSpec((1, K_COLS), lambda b, i: (0, 0)), tab, tab, tab2, tab2],
        out_specs=[pl.BlockSpec((1, tm, POOL_WIDTH), lambda b, i: (b, i, 0)),
                   pl.BlockSpec((1, tm, Q_COLS), lambda b, i: (b, i, 0)),
                   pl.BlockSpec((1, tm, K_COLS), lambda b, i: (b, i, 0)),
                   pl.BlockSpec((1, N_KV_HEADS * VT_ROWS, tm), lambda b, i: (b, 0, i))],
        out_shape=[jax.ShapeDtypeStruct((B, S, POOL_WIDTH), F32),
                   jax.ShapeDtypeStruct((B, S, Q_COLS), BF16),
                   jax.ShapeDtypeStruct((B, S, K_COLS), BF16),
                   jax.ShapeDtypeStruct((B, N_KV_HEADS * VT_ROWS, S), BF16)],
        name="inproj",
    )(x, mod3, w_all, wvt, gg, qg, kg, cq, sq, ck, sk)


def _ctx_kv_kernel(x_ref, mod_ref, w_ref, wvt_ref, kg_ref, k_ref, vt_ref):
    sh = mod_ref[0, 0:1, :]
    sc = mod_ref[0, 1:2, :]
    xm = (x_ref[0] * (1.0 + sc) + sh).astype(BF16)
    for pair in range(N_KV_HEADS // 2):
        p = _dot(xm, w_ref[:, pair * 2 * LANE:(pair + 1) * 2 * LANE])
        for j in range(2):
            h = pair * 2 + j
            kn = _norm_head(p[:, j * LANE:(j + 1) * LANE], kg_ref[:, h * LANE:(h + 1) * LANE])
            k_ref[0, :, h * LANE:(h + 1) * LANE] = kn.astype(BF16)
    _store_vt(vt_ref, lax.dot_general(wvt_ref[...], xm, _NT, preferred_element_type=F32))


def _ctx_kv_call(ctx, mod3, wk, wvt, kg, ctx_row):
    B, C, _ = ctx.shape
    return pl.pallas_call(
        _ctx_kv_kernel,
        grid=(B,),
        in_specs=[pl.BlockSpec((1, C, D_MODEL), lambda b: (b, 0, 0)),
                  pl.BlockSpec((1, 6, D_MODEL), lambda b: (ctx_row, 0, 0)),
                  pl.BlockSpec((D_MODEL, K_COLS), lambda b: (0, 0)),
                  pl.BlockSpec((KV_WIDTH, D_MODEL), lambda b: (0, 0)),
                  pl.BlockSpec((1, K_COLS), lambda b: (0, 0))],
        out_specs=[pl.BlockSpec((1, C, K_COLS), lambda b: (b, 0, 0)),
                   pl.BlockSpec((1, N_KV_HEADS * VT_ROWS, C), lambda b: (b, 0, 0))],
        out_shape=[jax.ShapeDtypeStruct((B, C, K_COLS), BF16),
                   jax.ShapeDtypeStruct((B, N_KV_HEADS * VT_ROWS, C), BF16)],
        name="ctx_kv",
    )(ctx, mod3, wk, wvt, kg)


POOL_PAD = 8


def _pool_kernel(u_ref, bd_ref, ps_ref, y_ref):
    S = u_ref.shape[1]
    n = S + 2 * POOL_PAD
    t = lax.broadcasted_iota(I32, (S, LANE), 0)
    lane = lax.broadcasted_iota(I32, (S, LANE), 1)
    zpad = jnp.zeros((POOL_PAD, LANE), F32)
    for half in range(POOL_WIDTH // LANE):
        u = u_ref[0, :, half * LANE:(half + 1) * LANE]
        ue = jnp.concatenate([zpad, u, zpad], axis=0)
        fwd = {1: ue}
        w = 1
        while w < POOL_WINDOWS[2 * half + 1]:
            fwd[2 * w] = fwd[w] + pltpu.roll(fwd[w], n - w, axis=0)
            w *= 2
        ds = []
        for win in POOL_WINDOWS[2 * half: 2 * half + 2]:
            hw = win // 2
            centred = pltpu.roll(fwd[win], hw, axis=0)[POOL_PAD:POOL_PAD + S]
            cnt = (jnp.minimum(t + hw, S) - jnp.maximum(t - hw, 0)).astype(F32)
            ds.append(centred / cnt - u)
        d = jnp.where(lane < POOL_GROUP, ds[0], ds[1]).astype(BF16)
        sl = slice(half * LANE, (half + 1) * LANE)
        y = _dot(d, bd_ref[sl, sl]) * ps_ref[:, sl]
        y_ref[0, :, sl] = y.astype(BF16)


def _pool_call(u, bd, ps):
    B, S, _ = u.shape
    return pl.pallas_call(
        _pool_kernel,
        grid=(B,),
        in_specs=[pl.BlockSpec((1, S, POOL_WIDTH), lambda b: (b, 0, 0)),
                  pl.BlockSpec((POOL_WIDTH, POOL_WIDTH), lambda b: (0, 0)),
                  pl.BlockSpec((1, POOL_WIDTH), lambda b: (0, 0))],
        out_specs=pl.BlockSpec((1, S, POOL_WIDTH), lambda b: (b, 0, 0)),
        out_shape=jax.ShapeDtypeStruct((B, S, POOL_WIDTH), BF16),
        name="pool",
    )(u, bd, ps)


ROW_REDUCE_WAYS = 16


def _reduce_rows(op, x):
    rows, lanes = x.shape
    if rows % (ROW_REDUCE_WAYS * SUBLANE) == 0:
        x = op(x.reshape(ROW_REDUCE_WAYS, rows // ROW_REDUCE_WAYS, lanes), axis=0)
    return op(x, axis=0, keepdims=True)


def _attn_kernel(q_ref, kc_ref, kl_ref, vtc_ref, vtl_ref, o_ref, s_buf, m_buf):
    t = pl.program_id(0)
    ts = s_buf.shape[-1]
    n_sub = q_ref.shape[1] // ts

    @pl.when(t == 0)
    def _():
        s_buf[1] = jnp.zeros_like(s_buf[1])
        m_buf[1] = jnp.zeros_like(m_buf[1])

    def step(slot):
        kc, kl = kc_ref[0], kl_ref[0]
        vtc, vtl = vtc_ref[0], vtl_ref[0]
        nc = kc.shape[0]
        for u in range(n_sub):
            cols = pl.ds(u * ts, ts)
            outs = []
            for j in range(GQA_GROUP):
                q = q_ref[0, cols, j * LANE:(j + 1) * LANE]
                sc = lax.dot_general(kc, q, _NT, preferred_element_type=F32)
                sl = lax.dot_general(kl, q, _NT, preferred_element_type=F32)
                s_buf[slot, u, j, 0:nc, :] = sc
                s_buf[slot, u, j, nc:, :] = sl
                mx = jnp.maximum(_reduce_rows(jnp.max, sc), _reduce_rows(jnp.max, sl))
                m_buf[slot, u, j] = jnp.broadcast_to(mx, (SUBLANE, ts))
                p = jnp.exp2(s_buf[1 - slot, u, j] - m_buf[1 - slot, u, j, 0:1, :]).astype(BF16)
                ol = _dot(vtc, p[0:nc]) + _dot(vtl, p[nc:])
                outs.append(ol[:HEAD_DIM] / ol[HEAD_DIM:HEAD_DIM + 1])
            outs.append(jnp.zeros((2 * LANE - GQA_GROUP * HEAD_DIM, ts), F32))
            o_ref[0, cols, :] = jnp.concatenate(outs, axis=0).T.astype(BF16)

    for slot in range(2):
        pl.when(t % 2 == slot)(functools.partial(step, slot))


def _attn_call(q, k_c, k_l, vt_c, vt_l):
    B, S, _ = q.shape
    C = k_c.shape[1]
    Lk = C + S
    tq = TQ_ATTN
    nq = S // tq
    n_items = B * N_KV_HEADS * nq

    def item(t):
        return t // (N_KV_HEADS * nq), (t // nq) % N_KV_HEADS, t % nq

    def cur(t):
        return item(jnp.minimum(t, n_items - 1))

    def prev(t):
        return item(jnp.maximum(t - 1, 0))

    return pl.pallas_call(
        _attn_kernel,
        grid=(n_items + 1,),
        in_specs=[pl.BlockSpec((1, tq, GQA_GROUP * LANE), lambda t: (cur(t)[0], cur(t)[2], cur(t)[1] // 2)),
                  pl.BlockSpec((1, C, LANE), lambda t: (cur(t)[0], 0, cur(t)[1])),
                  pl.BlockSpec((1, S, LANE), lambda t: (cur(t)[0], 0, cur(t)[1])),
                  pl.BlockSpec((1, VT_ROWS, C), lambda t: (prev(t)[0], prev(t)[1], 0)),
                  pl.BlockSpec((1, VT_ROWS, S), lambda t: (prev(t)[0], prev(t)[1], 0))],
        out_specs=pl.BlockSpec((1, tq, 2 * LANE), lambda t: (prev(t)[0], prev(t)[2], prev(t)[1])),
        out_shape=jax.ShapeDtypeStruct((B, S, ATTN_PAD), BF16),
        scratch_shapes=[pltpu.VMEM((2, tq // ATTN_SUB, GQA_GROUP, Lk, ATTN_SUB), F32),
                        pltpu.VMEM((2, tq // ATTN_SUB, GQA_GROUP, SUBLANE, ATTN_SUB), F32)],
        compiler_params=pltpu.CompilerParams(dimension_semantics=("arbitrary",)),
        name="attn",
    )(q, k_c, k_l, vt_c, vt_l)


def _mix_kernel(x_ref, yp_ref, at_ref, wop_ref, woa_ref, mod_ref, g_ref, b_ref, wr_ref, rb_ref, tri_ref, trie_ref,
                xmid_ref, hp_ref, lprow_ref, lpcol_ref, wts_ref, segn_ref, segb_ref, cnt_ref, carry):
    i = pl.program_id(0)
    tm = tri_ref.shape[0]
    n_sub = x_ref.shape[0] // tm

    @pl.when(i == 0)
    def _():
        carry[...] = jnp.zeros_like(carry)

    g1 = mod_ref[0, 2:3, :]
    sh2 = mod_ref[0, 3:4, :]
    sc2 = mod_ref[0, 4:5, :]

    def out_proj(s):
        rows = pl.ds(s * tm, tm)
        return _dot(yp_ref[rows, :], wop_ref[...]) + _dot(at_ref[rows, :], woa_ref[...])

    def norm_router(s, mix):
        rows = pl.ds(s * tm, tm)
        xmid = _layer_norm(ALPHA * x_ref[rows, :] + g1 * mix, g_ref[...], b_ref[...])
        xmid_ref[rows, :] = xmid
        h = xmid * (1.0 + sc2) + sh2
        _store_rows(hp_ref, _pack2(h[:, :HALF_D], h[:, HALF_D:]), s * tm)
        h_hi = h.astype(BF16)
        h_lo = (h - h_hi.astype(F32)).astype(BF16)
        a = lax.dot_general(wr_ref[...], h_hi, _NT, preferred_element_type=F32)
        b = lax.dot_general(wr_ref[0:N_EXPERTS, :], h_lo, _NT, preferred_element_type=F32)
        return a[:N_EXPERTS] + a[N_EXPERTS:] + b

    nxt = out_proj(0)
    logits = []
    for s in range(n_sub):
        mix = nxt
        if s + 1 < n_sub:
            nxt = out_proj(s + 1)
        logits.append(norm_router(s, mix))
    base = carry[:, 0:1]
    for s in range(n_sub):
        segb_ref[s] = jnp.broadcast_to(base, carry.shape).astype(I32)
        base = base + _route(logits[s], rb_ref, tri_ref, trie_ref, lprow_ref, lpcol_ref, wts_ref, segn_ref, s)
    carry[...] = jnp.broadcast_to(base, carry.shape)
    cnt_ref[...] = carry[...].astype(I32)


def _route(logits, rb_ref, tri_ref, trie_ref, lprow_ref, lpcol_ref, wts_ref, segn_ref, s):
    tm = logits.shape[1]
    cols = pl.ds(s * tm, tm)
    score = jax.nn.sigmoid(logits)
    sel = score + rb_ref[...]
    neg = jnp.float32(-jnp.inf)
    sub = lax.broadcasted_iota(I32, (GROUP_SIZE, tm), 0)
    grp_sel, grp_score = [], []
    for g in range(N_GROUPS):
        sg = sel[g * GROUP_SIZE:(g + 1) * GROUP_SIZE, :]
        m1 = jnp.max(sg, axis=0, keepdims=True)
        i1 = jnp.min(jnp.where(sg == m1, sub, GROUP_SIZE), axis=0, keepdims=True)
        m2 = jnp.max(jnp.where(sub == i1, neg, sg), axis=0, keepdims=True)
        grp_sel.append(sg)
        grp_score.append(m1 + m2)
    masked = []
    for g in range(N_GROUPS):
        ahead = jnp.zeros((1, tm), I32)
        for o in range(N_GROUPS):
            if o == g:
                continue
            beats = (grp_score[o] > grp_score[g]) | ((grp_score[o] == grp_score[g]) & (o < g))
            ahead = ahead + beats.astype(I32)
        masked.append(jnp.where(ahead < TOPK_GROUPS, grp_sel[g], neg))
    cur = jnp.concatenate(masked, axis=0)
    row = lax.broadcasted_iota(I32, (N_EXPERTS, tm), 0)
    member = jnp.zeros((N_EXPERTS, tm), F32)
    picks, wsel = [], []
    for k in range(TOP_K):
        mx = jnp.max(cur, axis=0, keepdims=True)
        ei = jnp.min(jnp.where(cur == mx, row, N_EXPERTS), axis=0, keepdims=True)
        hit = row == ei
        picks.append(ei)
        wsel.append(jnp.sum(jnp.where(hit, score, 0.0), axis=0, keepdims=True))
        cur = jnp.where(hit, neg, cur)
        member = jnp.where(hit, 1.0, member)
    tot = wsel[0]
    for k in range(1, TOP_K):
        tot = tot + wsel[k]
    n_e = jnp.sum(member, axis=1, keepdims=True)
    n_b = jnp.broadcast_to(n_e, (N_EXPERTS, LANE))
    pos = _dot(member.astype(BF16), tri_ref[...]) + _dot(trie_ref[...], n_b.astype(BF16))[:, 0:1]
    segn_ref[s] = n_b.astype(I32)
    lrows = []
    for k in range(TOP_K):
        lp = jnp.sum(jnp.where(row == picks[k], pos, 0.0), axis=0, keepdims=True)
        lprow_ref[k:k + 1, cols] = lp.astype(I32)
        lrows.append(lp)
    fill = [jnp.zeros((LANE - TOP_K, tm), F32)]
    lpcol_ref[cols, :] = jnp.concatenate(lrows + fill, axis=0).T.astype(I32)
    wrows = [wsel[k] / tot * ROUTED_SCALE for k in range(TOP_K)]
    wts_ref[cols, :] = jnp.concatenate(wrows + fill, axis=0).T
    return n_e


def _mix_call(x2, ypool, attn, wop, woa, mod3, g, b, wr_t, rb, tri, tri_e, tiles_per_seq):
    T = x2.shape[0]
    tm = TM_MIX
    n_sub = tm // MIX_SUB
    row = lambda i: (i, 0)
    fixed = lambda i: (0, 0)
    tok = lambda i: (0, i)
    seg = pl.BlockSpec((n_sub, N_EXPERTS, LANE), lambda i: (i, 0, 0))
    seg_shape = jax.ShapeDtypeStruct((T // MIX_SUB, N_EXPERTS, LANE), I32)
    return pl.pallas_call(
        _mix_kernel,
        grid=(T // tm,),
        in_specs=[pl.BlockSpec((tm, D_MODEL), row),
                  pl.BlockSpec((tm, POOL_WIDTH), row),
                  pl.BlockSpec((tm, ATTN_PAD), row),
                  pl.BlockSpec((POOL_WIDTH, D_MODEL), fixed),
                  pl.BlockSpec((ATTN_PAD, D_MODEL), fixed),
                  pl.BlockSpec((1, 6, D_MODEL), lambda i: (i // tiles_per_seq, 0, 0)),
                  pl.BlockSpec((1, D_MODEL), fixed),
                  pl.BlockSpec((1, D_MODEL), fixed),
                  pl.BlockSpec((2 * N_EXPERTS, D_MODEL), fixed),
                  pl.BlockSpec((N_EXPERTS, 1), fixed),
                  pl.BlockSpec((MIX_SUB, MIX_SUB), fixed),
                  pl.BlockSpec((N_EXPERTS, N_EXPERTS), fixed)],
        out_specs=[pl.BlockSpec((tm, D_MODEL), row),
                   pl.BlockSpec((tm * ROW_SUB, LANE), row),
                   pl.BlockSpec((TOP_K, tm), tok),
                   pl.BlockSpec((tm, LANE), row),
                   pl.BlockSpec((tm, LANE), row),
                   seg, seg,
                   pl.BlockSpec((N_EXPERTS, LANE), fixed)],
        out_shape=[jax.ShapeDtypeStruct((T, D_MODEL), F32),
                   jax.ShapeDtypeStruct((T * ROW_SUB, LANE), I32),
                   jax.ShapeDtypeStruct((TOP_K, T), I32),
                   jax.ShapeDtypeStruct((T, LANE), I32),
                   jax.ShapeDtypeStruct((T, LANE), F32),
                   seg_shape, seg_shape,
                   jax.ShapeDtypeStruct((N_EXPERTS, LANE), I32)],
        scratch_shapes=[pltpu.VMEM((N_EXPERTS, LANE), F32)],
        compiler_params=pltpu.CompilerParams(dimension_semantics=("arbitrary",)),
        name="mix",
    )(x2, ypool, attn, wop, woa, mod3, g, b, wr_t, rb, tri, tri_e)


def _segment_copies(seg_ref, tile, make):
    copies = []
    for e in range(N_EXPERTS):
        j = tile * N_EXPERTS + e
        n = seg_ref[1, j]
        copies.append((n > 0, make(pl.multiple_of(seg_ref[2, j] * ROW_SUB, ROW_SUB),
                                   pl.multiple_of(seg_ref[0, j] * ROW_SUB, ROW_SUB), n * ROW_SUB)))
    return copies


def _run(copies, act):
    for cond, cp in copies:
        pl.when(cond)(functools.partial(act, cp))


def _sort_matrix(lp_ref, row0, n_rows):
    tm = lp_ref.shape[1]
    j = lax.broadcasted_iota(I32, (n_rows, tm), 0) + row0
    hit = j == lp_ref[0:1, :]
    for k in range(1, TOP_K):
        hit = hit | (j == lp_ref[k:k + 1, :])
    return jnp.where(hit, 1.0, 0.0).astype(BF16)


def _dispatch_kernel(pst_ref, plen_ref, nreal_ref, seg_ref, hp_ref, lp_ref, xs_hbm, xt, zbuf, ssem, zsem):
    tm = hp_ref.shape[0] // ROW_SUB
    bm = zbuf.shape[0] // ROW_SUB
    nb = xs_hbm.shape[0] // (bm * ROW_SUB)
    i = pl.program_id(0)
    nt = pl.num_programs(0)

    def seg_copies(tile, s):
        return _segment_copies(seg_ref, tile, lambda t_row, s_row, rows: pltpu.make_async_copy(
            xt.at[s, pl.ds(t_row, rows)], xs_hbm.at[pl.ds(s_row, rows)], ssem.at[s]))

    def pad_fill(e):
        n = plen_ref[e] * ROW_SUB
        start = pl.multiple_of(pst_ref[e] * ROW_SUB, ROW_SUB)
        return pltpu.make_async_copy(zbuf.at[pl.ds(0, n)], xs_hbm.at[pl.ds(start, n)], zsem)

    def dead_fill(j):
        return pltpu.make_async_copy(zbuf, xs_hbm.at[pl.ds(j * bm * ROW_SUB, bm * ROW_SUB)], zsem)

    def fills(act):
        for e in range(N_EXPERTS):
            pl.when(plen_ref[e] > 0)(functools.partial(lambda e: act(pad_fill(e)), e))
        for j in range(nb - N_EXPERTS, nb):
            pl.when(j >= nreal_ref[0])(functools.partial(lambda j: act(dead_fill(j)), j))

    @pl.when(i == 0)
    def _():
        zbuf[...] = jnp.zeros_like(zbuf)
        fills(lambda cp: cp.start())

    def scatter(s):
        halves = [_unpack2(w) for w in _load_rows(hp_ref, tm)]
        h_lo = jnp.concatenate([lo.astype(BF16) for lo, _ in halves], axis=1)
        h_hi = jnp.concatenate([hi.astype(BF16) for _, hi in halves], axis=1)
        cr = TOP_K * tm // SORT_CHUNKS

        def sorted_rows(c):
            sort = _sort_matrix(lp_ref, c * cr, cr)
            return _dot(sort, h_lo), _dot(sort, h_hi)

        nxt = sorted_rows(0)
        for c in range(SORT_CHUNKS):
            x_lo, x_hi = nxt
            if c + 1 < SORT_CHUNKS:
                nxt = sorted_rows(c + 1)
            _store_rows(xt.at[s], _pack2(x_lo, x_hi), c * cr)
        _run(seg_copies(i, s), lambda cp: cp.start())

        def wait_tile(slot):
            pltpu.make_async_copy(xt.at[slot], xs_hbm.at[pl.ds(0, TOP_K * tm * ROW_SUB)], ssem.at[slot]).wait()

        @pl.when(i >= 1)
        def _():
            wait_tile(1 - s)

        @pl.when(i == nt - 1)
        def _():
            wait_tile(s)

    for s in range(2):
        pl.when(i % 2 == s)(functools.partial(scatter, s))

    @pl.when(i == 0)
    def _():
        fills(lambda cp: cp.wait())


def _dispatch_call(pad_row_start, pad_len, n_real, seg, hp, lprow, nb):
    tm = TM_ROUTE
    bm = BM_MOE
    T = hp.shape[0] // ROW_SUB
    grid_spec = pltpu.PrefetchScalarGridSpec(
        num_scalar_prefetch=4,
        grid=(T // tm,),
        in_specs=[pl.BlockSpec((tm * ROW_SUB, LANE), lambda i, a, b, c, d: (i, 0)),
                  pl.BlockSpec((TOP_K, tm), lambda i, a, b, c, d: (0, i))],
        out_specs=pl.BlockSpec(memory_space=pl.ANY),
        scratch_shapes=[pltpu.VMEM((2, TOP_K * tm * ROW_SUB, LANE), I32),
                        pltpu.VMEM((bm * ROW_SUB, LANE), I32),
                        pltpu.SemaphoreType.DMA((2,)),
                        pltpu.SemaphoreType.DMA],
    )
    return pl.pallas_call(
        _dispatch_kernel,
        grid_spec=grid_spec,
        out_shape=jax.ShapeDtypeStruct((nb * bm * ROW_SUB, LANE), I32),
        compiler_params=pltpu.CompilerParams(dimension_semantics=("arbitrary",)),
        name="dispatch",
    )(pad_row_start, pad_len, n_real, seg, hp, lprow)


def _moe_kernel(be_ref, nreal_ref, x_ref, wg_ref, wu_ref, wd_ref, y_ref, wgb, wub, wdb):
    bm = x_ref.shape[0] // ROW_SUB
    i = pl.program_id(0)
    n = nreal_ref[0]

    @pl.when(i < n)
    def _():
        e_changed = (i == 0) | (be_ref[i] != be_ref[jnp.maximum(i - 1, 0)])

        @pl.when(e_changed)
        def _():
            wgb[...] = wg_ref[0].astype(BF16)
            wub[...] = wu_ref[0].astype(BF16)
            wdb[...] = wd_ref[0].astype(BF16)

        cm = bm // MOE_ROW_CHUNKS

        def up(c):
            halves = [_unpack2(w) for w in _load_rows(x_ref, cm, c * cm)]
            xb = jnp.concatenate([lo.astype(BF16) for lo, _ in halves] + [hi.astype(BF16) for _, hi in halves],
                                 axis=1)
            return (_silu(_dot(xb, wgb[...])) * _dot(xb, wub[...])).astype(BF16)

        nxt = up(0)
        for c in range(MOE_ROW_CHUNKS):
            a = nxt
            if c + 1 < MOE_ROW_CHUNKS:
                nxt = up(c + 1)
            y = _dot(a, wdb[...])
            _store_rows(y_ref, _pack2(y[:, :HALF_D], y[:, HALF_D:]), c * cm)

    @pl.when(i >= n)
    def _():
        y_ref[...] = jnp.zeros_like(y_ref)


def _moe_call(block_expert, n_real, xs, w_gate, w_up, w_down):
    bm = BM_MOE
    nb = block_expert.shape[0]
    wmap = lambda i, be, nr: (be[i], 0, 0)
    rows = pl.BlockSpec((bm * ROW_SUB, LANE), lambda i, be, nr: (i, 0))
    grid_spec = pltpu.PrefetchScalarGridSpec(
        num_scalar_prefetch=2,
        grid=(nb,),
        in_specs=[rows,
                  pl.BlockSpec((1, D_MODEL, EXPERT_FF), wmap),
                  pl.BlockSpec((1, D_MODEL, EXPERT_FF), wmap),
                  pl.BlockSpec((1, EXPERT_FF, D_MODEL), wmap)],
        out_specs=rows,
        scratch_shapes=[pltpu.VMEM((D_MODEL, EXPERT_FF), BF16),
                        pltpu.VMEM((D_MODEL, EXPERT_FF), BF16),
                        pltpu.VMEM((EXPERT_FF, D_MODEL), BF16)],
    )
    return pl.pallas_call(
        _moe_kernel,
        grid_spec=grid_spec,
        out_shape=jax.ShapeDtypeStruct(xs.shape, I32),
        compiler_params=pltpu.CompilerParams(dimension_semantics=("arbitrary",)),
        name="moe",
    )(block_expert, n_real, xs, w_gate, w_up, w_down)


def _combine_kernel(seg_ref, ys_hbm, lp_ref, wc_ref, xmid_ref, mod_ref, wsg_ref, wsu_ref, wsd_ref, g_ref, b_ref,
                    o_ref, yt, gsem):
    tm = xmid_ref.shape[0]
    n_rows = TOP_K * tm
    i = pl.program_id(0)
    nt = pl.num_programs(0)

    def seg_copies(tile, s):
        return _segment_copies(seg_ref, tile, lambda t_row, s_row, rows: pltpu.make_async_copy(
            ys_hbm.at[pl.ds(s_row, rows)], yt.at[s, pl.ds(t_row, rows)], gsem.at[s]))

    @pl.when(i == 0)
    def _():
        _run(seg_copies(0, 0), lambda cp: cp.start())

    def finish(s):
        xmid = xmid_ref[...]
        sh2 = mod_ref[0, 3:4, :]
        sc2 = mod_ref[0, 4:5, :]
        g2 = mod_ref[0, 5:6, :]
        hb = (xmid * (1.0 + sc2) + sh2).astype(BF16)
        a = _silu(_dot(hb, wsg_ref[...])) * _dot(hb, wsu_ref[...])
        ffn = _dot(a.astype(BF16), wsd_ref[...])
        lp, wc = lp_ref[...], wc_ref[...]
        cr = n_rows // SORT_CHUNKS

        def chunk(c):
            lane = lax.broadcasted_iota(I32, (tm, cr), 1) + c * cr
            wsel = jnp.zeros((tm, cr), F32)
            for k in range(TOP_K):
                wsel = jnp.where(lane == lp[:, k:k + 1], wc[:, k:k + 1], wsel)
            halves = [_unpack2(w) for w in _load_rows(yt.at[s], cr, c * cr)]
            y = jnp.concatenate([lo.astype(BF16) for lo, _ in halves] + [hi.astype(BF16) for _, hi in halves],
                                axis=1)
            return wsel.astype(BF16), y

        nxt = chunk(0)
        for c in range(SORT_CHUNKS):
            wsel, y = nxt
            if c + 1 < SORT_CHUNKS:
                nxt = chunk(c + 1)
            ffn = ffn + _dot(wsel, y)
        o_ref[...] = _layer_norm(ALPHA * xmid + g2 * ffn, g_ref[...], b_ref[...])

    def step(s):
        @pl.when(i + 1 < nt)
        def _():
            _run(seg_copies(i + 1, 1 - s), lambda cp: cp.start())

        pltpu.make_async_copy(ys_hbm.at[pl.ds(0, n_rows * ROW_SUB)], yt.at[s], gsem.at[s]).wait()
        finish(s)

    for s in range(2):
        pl.when(i % 2 == s)(functools.partial(step, s))


def _combine_call(seg, ys, lpcol, wcol, xmid, mod3, wsg, wsu, wsd, g, b, tiles_per_seq):
    T = xmid.shape[0]
    tm = TM_ROUTE
    fixed = lambda i, sg: (0, 0)
    row = lambda i, sg: (i, 0)
    grid_spec = pltpu.PrefetchScalarGridSpec(
        num_scalar_prefetch=1,
        grid=(T // tm,),
        in_specs=[pl.BlockSpec(memory_space=pl.ANY),
                  pl.BlockSpec((tm, LANE), row),
                  pl.BlockSpec((tm, LANE), row),
                  pl.BlockSpec((tm, D_MODEL), row),
                  pl.BlockSpec((1, 6, D_MODEL), lambda i, sg: (i // tiles_per_seq, 0, 0)),
                  pl.BlockSpec((D_MODEL, SHARED_FF), fixed),
                  pl.BlockSpec((D_MODEL, SHARED_FF), fixed),
                  pl.BlockSpec((SHARED_FF, D_MODEL), fixed),
                  pl.BlockSpec((1, D_MODEL), fixed),
                  pl.BlockSpec((1, D_MODEL), fixed)],
        out_specs=pl.BlockSpec((tm, D_MODEL), row),
        scratch_shapes=[pltpu.VMEM((2, TOP_K * tm * ROW_SUB, LANE), I32),
                        pltpu.SemaphoreType.DMA((2,))],
    )
    return pl.pallas_call(
        _combine_kernel,
        grid_spec=grid_spec,
        out_shape=jax.ShapeDtypeStruct((T, D_MODEL), F32),
        compiler_params=pltpu.CompilerParams(dimension_semantics=("arbitrary",)),
        name="combine",
    )(seg, ys, lpcol, wcol, xmid, mod3, wsg, wsu, wsd, g, b)


def _group(a, b):
    ref = a if a is not None else b
    z = jnp.zeros(ref.shape[:-1], ref.dtype)
    a0, a1 = (a[..., 0], a[..., 1]) if a is not None else (z, z)
    b0, b1 = (b[..., 0], b[..., 1]) if b is not None else (z, z)
    return jnp.concatenate([a0, b0, a1, b1], axis=-1)


def _q_groups(per_head):
    return jnp.concatenate([_group(per_head(3 * (2 * p) + j), per_head(3 * (2 * p + 1) + j))
                            for p in range(N_KV_HEADS // 2) for j in range(GQA_GROUP)], axis=-1)


def _k_groups(per_head):
    return jnp.concatenate([_group(per_head(h), None) if h % 2 == 0 else _group(None, per_head(h))
                            for h in range(N_KV_HEADS)], axis=-1)


def _rope_pairs(seq):
    inv_freq = ROPE_THETA ** (-jnp.arange(0, HALF, 2, dtype=F32) / HALF)
    pos = jnp.arange(seq)
    rowp = (pos // GRID_W).astype(F32)
    colp = (pos % GRID_W).astype(F32)
    ang = jnp.concatenate([rowp[:, None] * inv_freq, colp[:, None] * inv_freq], axis=-1)
    cos, sin = jnp.cos(ang), jnp.sin(ang)
    return jnp.stack([cos, cos], axis=-1), jnp.stack([-sin, sin], axis=-1)


def kernel(x, c, ctx, c_ctx, w_mod, b_mod, w_in, q_norm, k_norm, pool_w, pool_scale, w_out, ln1_g, ln1_b,
           w_router, router_bias, w_gate, w_up, w_down, ws_gate, ws_up, ws_down, ln2_g, ln2_b):
    B, S, D = x.shape
    C = ctx.shape[1]
    T = B * S
    assert D == D_MODEL and w_mod.shape[0] == DEPTH and B + 1 <= MOD_ROWS
    assert S % TM_INPROJ == 0 and S % TQ_ATTN == 0 and S % TM_MIX == 0 and S % TM_ROUTE == 0
    assert S % GRID_W == 0 and C % SUBLANE == 0

    cc = jnp.concatenate([c, c_ctx[None, :], jnp.zeros((MOD_ROWS - B - 1, D), F32)], axis=0)
    mod3 = _mod_call(cc, w_mod[0], b_mod[0][None, :]).reshape(MOD_ROWS, 6, D)

    w = w_in[0]
    o1, o2, o3 = POOL_WIDTH, POOL_WIDTH + ATTN_WIDTH, POOL_WIDTH + ATTN_WIDTH + KV_WIDTH
    wq4 = w[:, o1:o2].reshape(D, N_HEADS, HALF, 2)
    wk4 = w[:, o2:o3].reshape(D, N_KV_HEADS, HALF, 2)
    wq = _q_groups(lambda h: wq4[:, h])
    wk = _k_groups(lambda h: wk4[:, h])
    wk = wk.astype(BF16)
    w_all = jnp.concatenate([w[:, :o1].astype(BF16), wq.astype(BF16), wk], axis=1)
    wvt = w[:, o3:].T.astype(BF16)
    qg4, kg4 = q_norm[0].reshape(1, HALF, 2), k_norm[0].reshape(1, HALF, 2)
    qg = _group(qg4, qg4)
    kg = _k_groups(lambda h: kg4)
    cos_p, sin_p = _rope_pairs(S)
    qscale = HEAD_DIM ** -0.5 * LOG2_E
    cq, sq = _group(cos_p, cos_p) * qscale, _group(sin_p, sin_p) * qscale
    ck = jnp.concatenate([_group(cos_p, None), _group(None, cos_p)], axis=-1)
    sk = jnp.concatenate([_group(sin_p, None), _group(None, sin_p)], axis=-1)
    slot = (jnp.arange(LANE) // HALF) % 2
    same = (slot[:, None] == slot[None, :]).astype(F32) * (1.0 / HEAD_DIM)
    gg = jnp.concatenate([same, same], axis=0).astype(BF16)
    u, q, k_l, vt_l = _inproj_call(x, mod3, w_all, wvt, gg, qg, kg, cq, sq, ck, sk)
    k_c, vt_c = _ctx_kv_call(ctx, mod3, wk, wvt, kg, B)
    attn = _attn_call(q, k_c, k_l, vt_c, vt_l)

    bd = jax.scipy.linalg.block_diag(*[pool_w[0, g] for g in range(len(POOL_WINDOWS))]).astype(BF16)
    ypool = _pool_call(u, bd, pool_scale[0][None, :])

    wo = w_out[0]
    wop = wo[:POOL_WIDTH].astype(BF16)
    woa = wo[POOL_WIDTH:].reshape(N_KV_HEADS, GQA_GROUP * HEAD_DIM, D)
    woa = jnp.pad(woa, ((0, 0), (0, 2 * LANE - GQA_GROUP * HEAD_DIM), (0, 0))).reshape(ATTN_PAD, D).astype(BF16)
    tri = (jnp.arange(MIX_SUB)[:, None] < jnp.arange(MIX_SUB)[None, :]).astype(BF16)
    wr = w_router[0].T
    wr_hi = wr.astype(BF16)
    wr_lo = (wr - wr_hi.astype(F32)).astype(BF16)
    tri_e = (jnp.arange(N_EXPERTS)[None, :] < jnp.arange(N_EXPERTS)[:, None]).astype(BF16)
    xmid, hp, lprow, lpcol, wcol, seg_n, seg_base, counts = _mix_call(
        x.reshape(T, D), ypool.reshape(T, POOL_WIDTH), attn.reshape(T, ATTN_PAD), wop, woa, mod3,
        ln1_g[0][None, :], ln1_b[0][None, :], jnp.concatenate([wr_hi, wr_lo], axis=0),
        router_bias[0][:, None], tri, tri_e, S // TM_MIX)

    bm = BM_MOE
    counts = counts[:, 0]
    padded = ((counts + bm - 1) // bm) * bm
    pad_end = jnp.cumsum(padded)
    pad_start = pad_end - padded
    nb = T * TOP_K // bm + N_EXPERTS
    n_real = (pad_end[-1] // bm).astype(I32).reshape(1)
    blk_row = jnp.arange(nb, dtype=I32) * bm
    block_expert = jnp.minimum(jnp.sum((pad_end[None, :] <= blk_row[:, None]).astype(I32), axis=1), N_EXPERTS - 1)

    seg_n = seg_n[:, :, 0]
    seg = jnp.stack([pad_start[None, :] + seg_base[:, :, 0], seg_n, jnp.cumsum(seg_n, axis=1) - seg_n])
    seg = seg.reshape(3, -1).astype(I32)
    xs = _dispatch_call((pad_start + counts).astype(I32), (padded - counts).astype(I32), n_real, seg, hp, lprow, nb)
    ys = _moe_call(block_expert.astype(I32), n_real, xs, w_gate[0], w_up[0], w_down[0])
    out = _combine_call(seg, ys, lpcol, wcol, xmid, mod3, ws_gate[0].astype(BF16), ws_up[0].astype(BF16),
                        ws_down[0].astype(BF16), ln2_g[0][None, :], ln2_b[0][None, :], S // TM_ROUTE)
    return out.reshape(B, S, D)
```
